```python
import math
import jax, jax.numpy as jnp
from jax import lax
import numpy as np

D_MODEL = 2048
BATCH = 4
SEQ = 2048
DEPTH = 4
DEC_BATCH = 32
DEC_SEQ = 1
PAST_LEN = 16384
PAGE_SIZE = 128

HEAD_DIM = D_MODEL // 16
N_LIN_HEADS = 8
LIN_WIDTH = N_LIN_HEADS * HEAD_DIM
N_ATT_HEADS = 8
N_KV_HEADS = 2
GROUP = N_ATT_HEADS // N_KV_HEADS
ATT_WIDTH = N_ATT_HEADS * HEAD_DIM
KV_WIDTH = N_KV_HEADS * HEAD_DIM
MIX_WIDTH = LIN_WIDTH + ATT_WIDTH
WINDOW = 128
D_FF = ((8 * D_MODEL // 3 + 255) // 256) * 256
CHUNK = 64
IN_WIDTH = 4 * LIN_WIDTH + ATT_WIDTH + 2 * KV_WIDTH
SPLIT_IDX = (LIN_WIDTH, 2 * LIN_WIDTH, 3 * LIN_WIDTH, 4 * LIN_WIDTH,
             4 * LIN_WIDTH + ATT_WIDTH, 4 * LIN_WIDTH + ATT_WIDTH + KV_WIDTH)
EPS = 1e-6
ATT_SCALE = HEAD_DIM ** -0.5

kernel_name = "hymba_hgrn2_swa_sink_adaln_decoder_step"


def _rms(x, g):
    xf = x.astype(jnp.float32)
    y = xf * lax.rsqrt(jnp.mean(xf * xf, axis=-1, keepdims=True) + EPS)
    return (y * g.astype(jnp.float32)).astype(x.dtype)


def _hgrn2_chunked(q, k, v, logf, s0):
    B, T, H, _ = q.shape
    C = math.gcd(T, CHUNK)
    n = T // C

    def to_chunks(a):
        return a.reshape(B, n, C, H, a.shape[-1]).transpose(1, 0, 3, 2, 4)

    qc, kc, vc, gc = to_chunks(q), to_chunks(k), to_chunks(v), to_chunks(logf)
    causal = jnp.tril(jnp.ones((C, C), dtype=bool))

    def step(S, inp):
        qi, ki, vi, gi = inp
        b = jnp.cumsum(gi, axis=-2)
        o_inter = jnp.einsum('bhtk,bhkv->bhtv', qi * jnp.exp(b), S)
        diff = b[:, :, :, None, :] - b[:, :, None, :, :]
        decay = jnp.exp(jnp.where(causal[:, :, None], diff, -jnp.inf))
        A = jnp.einsum('bhtk,bhtsk,bhsk->bhts', qi, decay, ki)
        o_intra = jnp.einsum('bhts,bhsv->bhtv', A, vi)
        b_end = b[:, :, -1:, :]
        S_new = S * jnp.exp(b_end[:, :, 0, :, None]) + jnp.einsum(
            'bhsk,bhsv->bhkv', ki * jnp.exp(b_end - b), vi)
        return S_new, o_inter + o_intra

    S_T, o = lax.scan(step, s0.astype(jnp.float32), (qc, kc, vc, gc))
    o = o.transpose(1, 0, 3, 2, 4).reshape(B, T, H, v.shape[-1])
    return o.astype(v.dtype), S_T.astype(s0.dtype)


def _sink_attention(q, k, v, dist, valid, sinks):
    slopes = 2.0 ** (-8.0 * jnp.arange(1, N_ATT_HEADS + 1) / N_ATT_HEADS)
    slopes = slopes.reshape(N_KV_HEADS, GROUP)[None, None, :, :, None, None]
    s = jnp.einsum('bnqhgd,bnshd->bnhgqs', q, k).astype(jnp.float32) * ATT_SCALE
    s = s - slopes * dist[None, :, None, None, :, :].astype(jnp.float32)
    s = jnp.where(valid[None, :, None, None, :, :], s, -jnp.inf)
    sink = jnp.broadcast_to(
        sinks.astype(jnp.float32).reshape(N_KV_HEADS, GROUP)[None, None, :, :, None, None],
        s.shape[:-1] + (1,))
    p = jax.nn.softmax(jnp.concatenate([s, sink], axis=-1), axis=-1)[..., :-1]
    return jnp.einsum('bnhgqs,bnshd->bnqhgd', p.astype(v.dtype), v)


def _layer(x, c, s0, k_past, v_past, lb, w_ada, b_ada, norm_mix, w_in, q_norm, k_norm,
           sinks, g_norm, w_out, norm_ffn, w_gate, w_up, w_down):
    B, T, _ = x.shape
    mod = jnp.einsum('bd,de->be', jax.nn.silu(c), w_ada) + b_ada
    sh1, sc1, gt1, sh2, sc2, gt2 = jnp.split(mod[:, None, :], 6, axis=-1)

    h = _rms(x, norm_mix) * (1.0 + sc1) + sh1
    proj = jnp.einsum('btd,de->bte', h, w_in)
    q_l, f_l, i_l, g_l, q_a, k_a, v_a = jnp.split(proj, SPLIT_IDX, axis=-1)

    def heads(a, n):
        return a.reshape(B, T, n, HEAD_DIM)

    ql = jax.nn.silu(heads(q_l, N_LIN_HEADS))
    f_raw = heads(f_l, N_LIN_HEADS).astype(jnp.float32)
    lbh = lb.reshape(N_LIN_HEADS, HEAD_DIM).astype(jnp.float32)
    logf = jnp.logaddexp(jnp.log(lbh), jnp.log1p(-lbh) + jax.nn.log_sigmoid(f_raw))
    kl = -jnp.expm1(logf)
    o_l, s_new = _hgrn2_chunked(ql, kl, heads(i_l, N_LIN_HEADS), logf, s0)
    o_l = _rms(o_l, g_norm) * jax.nn.silu(heads(g_l, N_LIN_HEADS))

    qa = _rms(heads(q_a, N_ATT_HEADS), q_norm)
    ka = _rms(heads(k_a, N_KV_HEADS), k_norm)
    va = heads(v_a, N_KV_HEADS)
    if k_past is None:
        nb = T // WINDOW
        qb = qa.reshape(B, nb, WINDOW, N_KV_HEADS, GROUP, HEAD_DIM)
        kb = ka.reshape(B, nb, WINDOW, N_KV_HEADS, HEAD_DIM)
        vb = va.reshape(B, nb, WINDOW, N_KV_HEADS, HEAD_DIM)
        padw = ((0, 0), (1, 0), (0, 0), (0, 0), (0, 0))
        kk = jnp.concatenate([jnp.pad(kb, padw)[:, :-1], kb], axis=2)
        vv = jnp.concatenate([jnp.pad(vb, padw)[:, :-1], vb], axis=2)
        kpos = jnp.arange(2 * WINDOW)
        dist = (WINDOW + jnp.arange(WINDOW))[:, None] - kpos[None, :]
        valid = ((dist >= 0) & (dist <= WINDOW))[None] & (
            (jnp.arange(nb)[:, None, None] > 0) | (kpos[None, None, :] >= WINDOW))
        dist = dist[None]
        k_new, v_new = ka[:, -WINDOW:], va[:, -WINDOW:]
    else:
        qb = qa.reshape(B, 1, T, N_KV_HEADS, GROUP, HEAD_DIM)
        kcat = jnp.concatenate([k_past, ka], axis=1)
        vcat = jnp.concatenate([v_past, va], axis=1)
        kk, vv = kcat[:, None], vcat[:, None]
        dist = (WINDOW + jnp.arange(T))[:, None] - jnp.arange(WINDOW + T)[None, :]
        valid = ((dist >= 0) & (dist <= WINDOW))[None]
        dist = dist[None]
        k_new, v_new = kcat[:, -WINDOW:], vcat[:, -WINDOW:]
    o_a = _sink_attention(qb, kk, vv, dist, valid, sinks).reshape(B, T, ATT_WIDTH)

    mixed = jnp.concatenate([o_l.reshape(B, T, LIN_WIDTH), o_a], axis=-1)
    x = x + gt1 * jnp.einsum('bte,ed->btd', mixed, w_out)

    h2 = _rms(x, norm_ffn) * (1.0 + sc2) + sh2
    ff = jax.nn.silu(jnp.einsum('btd,df->btf', h2, w_gate)) * jnp.einsum('btd,df->btf', h2, w_up)
    x = x + gt2 * jnp.einsum('btf,fd->btd', ff, w_down)
    return x, s_new, k_new, v_new


def setup_inputs(seed: int = 0) -> dict:
    key = jax.random.key(seed)
    ks = jax.random.split(key, 24)

    def nrm(k, shape, s):
        return jax.random.normal(k, shape, jnp.float32) * s

    return {
        "x_prompt": nrm(ks[0], (BATCH, SEQ, D_MODEL), 1.0),
        "x_sample": nrm(ks[1], (DEC_BATCH, DEC_SEQ, D_MODEL), 1.0),
        "cache_k": nrm(ks[2], (DEPTH, DEC_BATCH, WINDOW, N_KV_HEADS, HEAD_DIM), 1.0),
        "cache_v": nrm(ks[3], (DEPTH, DEC_BATCH, WINDOW, N_KV_HEADS, HEAD_DIM), 1.0),
        "state_hgrn": nrm(ks[4], (DEPTH, DEC_BATCH, N_LIN_HEADS, HEAD_DIM, HEAD_DIM), 1.0),
        "c_prompt": nrm(ks[5], (BATCH, D_MODEL), 1.0),
        "c_sample": nrm(ks[6], (DEC_BATCH, D_MODEL), 1.0),
        "lower_bounds": nrm(ks[7], (DEPTH, LIN_WIDTH), 1.0),
        "w_ada": nrm(ks[8], (DEPTH, D_MODEL, 6 * D_MODEL), 0.5 * D_MODEL ** -0.5),
        "b_ada": nrm(ks[9], (DEPTH, 6 * D_MODEL), 0.02),
        "norm_mix": 1.0 + nrm(ks[10], (DEPTH, D_MODEL), 0.01),
        "w_in": nrm(ks[11], (DEPTH, D_MODEL, IN_WIDTH), D_MODEL ** -0.5),
        "q_norm": 1.0 + nrm(ks[12], (DEPTH, HEAD_DIM), 0.01),
        "k_norm": 1.0 + nrm(ks[13], (DEPTH, HEAD_DIM), 0.01),
        "attn_sinks": nrm(ks[14], (DEPTH, N_ATT_HEADS), 0.5),
        "g_norm": 1.0 + nrm(ks[15], (DEPTH, HEAD_DIM), 0.01),
        "w_out": nrm(ks[16], (DEPTH, MIX_WIDTH, D_MODEL), MIX_WIDTH ** -0.5),
        "norm_ffn": 1.0 + nrm(ks[17], (DEPTH, D_MODEL), 0.01),
        "w_gate": nrm(ks[18], (DEPTH, D_MODEL, D_FF), D_MODEL ** -0.5),
        "w_up": nrm(ks[19], (DEPTH, D_MODEL, D_FF), D_MODEL ** -0.5),
        "w_down": nrm(ks[20], (DEPTH, D_FF, D_MODEL), D_FF ** -0.5),
    }


def reference(x_prompt, x_sample, cache_k, cache_v, state_hgrn, c_prompt, c_sample,
              lower_bounds, w_ada, b_ada, norm_mix, w_in, q_norm, k_norm, attn_sinks,
              g_norm, w_out, norm_ffn, w_gate, w_up, w_down):
    lb_all = jnp.cumsum(jax.nn.softmax(lower_bounds.astype(jnp.float32), axis=0), axis=0)
    lb_all = lb_all - lb_all[:1]
    s0_prompt = jnp.zeros((x_prompt.shape[0], N_LIN_HEADS, HEAD_DIM, HEAD_DIM), state_hgrn.dtype)
    yp, ys = x_prompt, x_sample
    pk, pv, ps, sk, sv, ss = [], [], [], [], [], []
    for l in range(DEPTH):
        params = (lb_all[l], w_ada[l], b_ada[l], norm_mix[l], w_in[l], q_norm[l], k_norm[l],
                  attn_sinks[l], g_norm[l], w_out[l], norm_ffn[l], w_gate[l], w_up[l], w_down[l])
        yp, s_p, k_p, v_p = _layer(yp, c_prompt, s0_prompt, None, None, *params)
        ys, s_s, k_s, v_s = _layer(ys, c_sample, state_hgrn[l], cache_k[l], cache_v[l], *params)
        pk.append(k_p); pv.append(v_p); ps.append(s_p)
        sk.append(k_s); sv.append(v_s); ss.append(s_s)
    return (yp, ys, jnp.stack(pk), jnp.stack(pv), jnp.stack(ps),
            jnp.stack(sk), jnp.stack(sv), jnp.stack(ss))
```

```python
import functools
import math

import jax
import jax.numpy as jnp
from jax import lax
from jax.experimental import pallas as pl
from jax.experimental.pallas import tpu as pltpu

F32 = jnp.float32
BF16 = jnp.bfloat16

D_MODEL = 2048
DEPTH = 4
HEAD_DIM = 128
N_LIN_HEADS = 8
LIN_WIDTH = N_LIN_HEADS * HEAD_DIM
N_ATT_HEADS = 8
N_KV_HEADS = 2
GROUP = N_ATT_HEADS // N_KV_HEADS
ATT_WIDTH = N_ATT_HEADS * HEAD_DIM
KV_WIDTH = N_KV_HEADS * HEAD_DIM
MIX_WIDTH = LIN_WIDTH + ATT_WIDTH
WINDOW = 128
D_FF = 5632
IN_WIDTH = 4 * LIN_WIDTH + ATT_WIDTH + 2 * KV_WIDTH
EPS = 1e-6
ATT_SCALE = HEAD_DIM ** -0.5
ALIBI_SLOPES = tuple(2.0 ** (-8.0 * (a + 1) / N_ATT_HEADS) for a in range(N_ATT_HEADS))

COL_Q_L = 0
COL_F_L = LIN_WIDTH // HEAD_DIM
COL_I_L = 2 * LIN_WIDTH // HEAD_DIM
COL_G_L = 3 * LIN_WIDTH // HEAD_DIM
OFF_Q_A = 4 * LIN_WIDTH
OFF_K_A = OFF_Q_A + ATT_WIDTH
OFF_V_A = OFF_K_A + KV_WIDTH

MOD_ROWS = 40
HGRN_CHUNK = 128
HGRN_SUB = 8
NEG_BIG = -1e30

VMEM_LIMIT = 56 * 1024 * 1024


def _cparams(sem):
    return pltpu.CompilerParams(dimension_semantics=sem, vmem_limit_bytes=VMEM_LIMIT)


def _silu(x):
    return x / (1.0 + jnp.exp(-x))


def _rms_rows(x, g):
    ms = jnp.mean(x * x, axis=-1, keepdims=True)
    return x * lax.rsqrt(ms + EPS) * g


def _log_forget(f_raw, log_lb, log1m_lb):
    log_sig = jnp.minimum(f_raw, 0.0) - jnp.log1p(jnp.exp(-jnp.abs(f_raw)))
    b2 = log1m_lb + log_sig
    m = jnp.maximum(log_lb, b2)
    return m + jnp.log1p(jnp.exp(-jnp.abs(log_lb - b2)))


def _lb_kernel(lb_ref, loglb_ref, log1m_ref):
    x = lb_ref[...]
    m = jnp.max(x, axis=0, keepdims=True)
    e = jnp.exp(x - m)
    p = e / jnp.sum(e, axis=0, keepdims=True)
    acc = jnp.zeros_like(p[0:1])
    for l in range(DEPTH):
        if l > 0:
            acc = acc + p[l:l + 1]
        loglb_ref[l:l + 1, :] = jnp.log(acc)
        log1m_ref[l:l + 1, :] = jnp.log1p(-acc)


def _lower_bounds(lower_bounds):
    return pl.pallas_call(
        _lb_kernel,
        out_shape=(jax.ShapeDtypeStruct((DEPTH, LIN_WIDTH), F32),
                   jax.ShapeDtypeStruct((DEPTH, LIN_WIDTH), F32)),
        name="lower_bounds",
    )(lower_bounds.astype(F32))


ADA_TN = 1024


def _ada_kernel(c_ref, w_ref, b_ref, o_ref):
    s = _silu(c_ref[...]).astype(BF16)
    acc = jnp.dot(s, w_ref[...].astype(BF16), preferred_element_type=F32)
    o_ref[...] = acc + b_ref[...]


def _modulation(c_all, w_ada, b_ada):
    n = 6 * D_MODEL
    return pl.pallas_call(
        _ada_kernel,
        grid=(DEPTH, n // ADA_TN),
        in_specs=[
            pl.BlockSpec((MOD_ROWS, D_MODEL), lambda l, j: (0, 0)),
            pl.BlockSpec((None, D_MODEL, ADA_TN), lambda l, j: (l, 0, j)),
            pl.BlockSpec((None, 1, ADA_TN), lambda l, j: (l, 0, j)),
        ],
        out_specs=pl.BlockSpec((None, MOD_ROWS, ADA_TN), lambda l, j: (l, 0, j)),
        out_shape=jax.ShapeDtypeStruct((DEPTH, MOD_ROWS, n), F32),
        compiler_params=_cparams(("parallel", "parallel")),
        name="modulation",
    )(c_all, w_ada, b_ada.reshape(DEPTH, 1, n))


NORM_ROWS = 256


def _modulated_norm_into(h_scr, x_ref, sh_ref, sc_ref, g_ref):
    rows = x_ref.shape[0]
    ch = min(rows, NORM_ROWS)
    per_row = sc_ref.shape[0] != 1
    gain = g_ref[...]

    def body(c, carry):
        r0 = pl.multiple_of(c * ch, ch)
        rs = pl.ds(r0, ch)
        sc = sc_ref[rs, :] if per_row else sc_ref[...]
        sh = sh_ref[rs, :] if per_row else sh_ref[...]
        y = _rms_rows(x_ref[rs, :], gain)
        h_scr[rs, :] = (y * (1.0 + sc) + sh).astype(BF16)
        return carry

    lax.fori_loop(0, rows // ch, body, 0)


def _norm_mm_kernel(x_ref, sh_ref, sc_ref, g_ref, w_ref, o_ref, h_scr):
    @pl.when(pl.program_id(2) == 0)
    def _():
        _modulated_norm_into(h_scr, x_ref, sh_ref, sc_ref, g_ref)

    o_ref[...] = jnp.dot(h_scr[...], w_ref[...], preferred_element_type=F32).astype(o_ref.dtype)


def _norm_matmul(x, mod, mod_col, gain, w, tm, tn):
    G, R, _ = x.shape
    r = mod.shape[1]
    rb = 1 if r == 1 else tm
    n = w.shape[1]
    row_idx = (lambda b, i: 0) if r == 1 else (lambda b, i: i)
    return pl.pallas_call(
        _norm_mm_kernel,
        grid=(G, R // tm, n // tn),
        in_specs=[
            pl.BlockSpec((None, tm, D_MODEL), lambda b, i, j: (b, i, 0)),
            pl.BlockSpec((None, rb, D_MODEL), lambda b, i, j: (b, row_idx(b, i), mod_col)),
            pl.BlockSpec((None, rb, D_MODEL), lambda b, i, j: (b, row_idx(b, i), mod_col + 1)),
            pl.BlockSpec((1, D_MODEL), lambda b, i, j: (0, 0)),
            pl.BlockSpec((D_MODEL, tn), lambda b, i, j: (0, j)),
        ],
        out_specs=pl.BlockSpec((None, tm, tn), lambda b, i, j: (b, i, j)),
        out_shape=jax.ShapeDtypeStruct((G, R, n), F32),
        scratch_shapes=[pltpu.VMEM((tm, D_MODEL), BF16)],
        compiler_params=_cparams(("parallel", "parallel", "arbitrary")),
        name="norm_in_proj",
    )(x, mod, mod, gain, w)


def _norm_gu_kernel(x_ref, sh_ref, sc_ref, g_ref, wg_ref, wu_ref, o_ref, h_scr):
    @pl.when(pl.program_id(2) == 0)
    def _():
        _modulated_norm_into(h_scr, x_ref, sh_ref, sc_ref, g_ref)

    h = h_scr[...]
    gate = jnp.dot(h, wg_ref[...], preferred_element_type=F32)
    up = jnp.dot(h, wu_ref[...], preferred_element_type=F32)
    o_ref[...] = (_silu(gate) * up).astype(o_ref.dtype)


def _norm_gate_up(x, mod, mod_col, gain, wg, wu, tm, tn):
    G, R, _ = x.shape
    r = mod.shape[1]
    rb = 1 if r == 1 else tm
    row_idx = (lambda b, i: 0) if r == 1 else (lambda b, i: i)
    return pl.pallas_call(
        _norm_gu_kernel,
        grid=(G, R // tm, D_FF // tn),
        in_specs=[
            pl.BlockSpec((None, tm, D_MODEL), lambda b, i, j: (b, i, 0)),
            pl.BlockSpec((None, rb, D_MODEL), lambda b, i, j: (b, row_idx(b, i), mod_col)),
            pl.BlockSpec((None, rb, D_MODEL), lambda b, i, j: (b, row_idx(b, i), mod_col + 1)),
            pl.BlockSpec((1, D_MODEL), lambda b, i, j: (0, 0)),
            pl.BlockSpec((D_MODEL, tn), lambda b, i, j: (0, j)),
            pl.BlockSpec((D_MODEL, tn), lambda b, i, j: (0, j)),
        ],
        out_specs=pl.BlockSpec((None, tm, tn), lambda b, i, j: (b, i, j)),
        out_shape=jax.ShapeDtypeStruct((G, R, D_FF), BF16),
        scratch_shapes=[pltpu.VMEM((tm, D_MODEL), BF16)],
        compiler_params=_cparams(("parallel", "parallel", "arbitrary")),
        name="norm_gate_up",
    )(x, mod, mod, gain, wg, wu)


def _mm_res_kernel(a_ref, w_ref, x_ref, gt_ref, o_ref):
    acc = jnp.dot(a_ref[...].astype(BF16), w_ref[...], preferred_element_type=F32)
    o_ref[...] = x_ref[...] + gt_ref[...] * acc


def _matmul_residual(a, w, x, mod, mod_off, tm, tn):
    G, R, K = a.shape
    r = mod.shape[1]
    rb = 1 if r == 1 else tm
    row_idx = (lambda b, i: 0) if r == 1 else (lambda b, i: i)
    gcol = mod_off // tn
    return pl.pallas_call(
        _mm_res_kernel,
        grid=(G, R // tm, D_MODEL // tn),
        in_specs=[
            pl.BlockSpec((None, tm, K), lambda b, i, j: (b, i, 0)),
            pl.BlockSpec((K, tn), lambda b, i, j: (0, j)),
            pl.BlockSpec((None, tm, tn), lambda b, i, j: (b, i, j)),
            pl.BlockSpec((None, rb, tn), lambda b, i, j: (b, row_idx(b, i), gcol + j)),
        ],
        out_specs=pl.BlockSpec((None, tm, tn), lambda b, i, j: (b, i, j)),
        out_shape=jax.ShapeDtypeStruct((G, R, D_MODEL), F32),
        compiler_params=_cparams(("parallel", "parallel", "arbitrary")),
        name="matmul_residual",
    )(a, w, x, mod)


def _split3_bf16(x):
    hi = x.astype(BF16)
    r1 = x - hi.astype(F32)
    mid = r1.astype(BF16)
    lo = (r1 - mid.astype(F32)).astype(BF16)
    return hi, mid, lo


def _hgrn_prompt_kernel(q_ref, f_ref, i_ref, g_ref, loglb_ref, log1m_ref, gn_ref,
                        o_ref, s_ref, st_scr, q_scr, k_scr, b_scr, a_scr):
    C = HGRN_CHUNK
    n_chunks = q_ref.shape[0] // C
    row = lax.broadcasted_iota(jnp.int32, (C, C), 0)
    col = lax.broadcasted_iota(jnp.int32, (C, C), 1)
    tri = (row >= col).astype(BF16)
    row_k = lax.broadcasted_iota(jnp.int32, (C, HEAD_DIM), 0)
    sub_row = lax.broadcasted_iota(jnp.int32, (HGRN_SUB, HEAD_DIM), 0)
    sub_lane = lax.broadcasted_iota(jnp.int32, (HGRN_SUB, C), 1)
    log_lb = loglb_ref[...]
    log1m_lb = log1m_ref[...]
    g_norm = gn_ref[...]

    levels = []
    L = C // 2
    while L >= HGRN_SUB:
        levels.append(L)
        L //= 2

    st_scr[...] = jnp.zeros_like(st_scr)

    def chunk_body(c, carry):
        r0 = pl.multiple_of(c * C, C)
        q = _silu(q_ref[pl.ds(r0, C), :])
        logf = _log_forget(f_ref[pl.ds(r0, C), :], log_lb, log1m_lb)
        k = 1.0 - jnp.exp(logf)
        v = i_ref[pl.ds(r0, C), :]
        v16 = v.astype(BF16)

        hi, mid, lo = _split3_bf16(logf)
        b = (jnp.dot(tri, hi, preferred_element_type=F32)
             + jnp.dot(tri, mid, preferred_element_type=F32)
             + jnp.dot(tri, lo, preferred_element_type=F32))
        b_end = b[C - 1:C, :]

        q_scr[...] = q
        k_scr[...] = k
        b_scr[...] = b

        st = st_scr[...]
        o = lax.dot_general((q * jnp.exp(b)).astype(BF16), st.astype(BF16),
                            (((1,), (1,)), ((), ())), preferred_element_type=F32)

        a_off = jnp.zeros((C, C), F32)
        for L in levels:
            odd = ((row_k // L) % 2) == 1
            ref_rows = [jnp.broadcast_to(b[p * 2 * L + L - 1:p * 2 * L + L, :], (2 * L, HEAD_DIM))
                        for p in range(C // (2 * L))]
            ref = ref_rows[0] if len(ref_rows) == 1 else jnp.concatenate(ref_rows, axis=0)
            e = jnp.exp(jnp.where(odd, b - ref, ref - b))
            qs = jnp.where(odd, q * e, 0.0).astype(BF16)
            ks = jnp.where(odd, 0.0, k * e).astype(BF16)
            a_l = lax.dot_general(qs, ks, (((1,), (1,)), ((), ())), preferred_element_type=F32)
            keep = ((row // L) % 2 == 1) & ((col // L) == (row // L) - 1)
            a_off = a_off + jnp.where(keep, a_l, 0.0)
        a_scr[...] = a_off

        def sub_body(sb, carry2):
            s0 = pl.multiple_of(sb * HGRN_SUB, HGRN_SUB)
            qb = q_scr[pl.ds(s0, HGRN_SUB), :]
            bb = b_scr[pl.ds(s0, HGRN_SUB), :]
            blk = a_scr[pl.ds(s0, HGRN_SUB), :]
            for s in range(HGRN_SUB):
                bs = b_scr[pl.ds(s0 + s, 1), :]
                ks_row = k_scr[pl.ds(s0 + s, 1), :]
                e = jnp.exp(jnp.where(sub_row >= s, bb - bs, NEG_BIG))
                colsum = jnp.sum(qb * ks_row * e, axis=-1, keepdims=True)
                blk = jnp.where(sub_lane == s0 + s, colsum, blk)
            a_scr[pl.ds(s0, HGRN_SUB), :] = blk
            return carry2

        lax.fori_loop(0, C // HGRN_SUB, sub_body, 0)

        o = o + jnp.dot(a_scr[...].astype(BF16), v16, preferred_element_type=F32)

        k_end = (k * jnp.exp(b_end - b)).astype(BF16)
        upd = lax.dot_general(v16, k_end, (((0,), (0,)), ((), ())), preferred_element_type=F32)
        st_scr[...] = st * jnp.exp(b_end) + upd

        o = _rms_rows(o, g_norm) * _silu(g_ref[pl.ds(r0, C), :])
        o_ref[pl.ds(r0, C), :] = o.astype(o_ref.dtype)
        return carry

    lax.fori_loop(0, n_chunks, chunk_body, 0)
    s_ref[...] = st_scr[...].T


def _hgrn_prompt(proj, log_lb, log1m_lb, g_norm):
    B, T, _ = proj.shape
    seq = lambda off: pl.BlockSpec((None, T, HEAD_DIM), lambda b, h: (b, 0, off + h))
    return pl.pallas_call(
        _hgrn_prompt_kernel,
        grid=(B, N_LIN_HEADS),
        in_specs=[
            seq(COL_Q_L), seq(COL_F_L), seq(COL_I_L), seq(COL_G_L),
            pl.BlockSpec((1, HEAD_DIM), lambda b, h: (0, h)),
            pl.BlockSpec((1, HEAD_DIM), lambda b, h: (0, h)),
            pl.BlockSpec((1, HEAD_DIM), lambda b, h: (0, 0)),
        ],
        out_specs=(
            pl.BlockSpec((None, T, HEAD_DIM), lambda b, h: (b, 0, h)),
            pl.BlockSpec((None, None, HEAD_DIM, HEAD_DIM), lambda b, h: (b, h, 0, 0)),
        ),
        out_shape=(
            jax.ShapeDtypeStruct((B, T, LIN_WIDTH), BF16),
            jax.ShapeDtypeStruct((B, N_LIN_HEADS, HEAD_DIM, HEAD_DIM), F32),
        ),
        scratch_shapes=[
            pltpu.VMEM((HEAD_DIM, HEAD_DIM), F32),
            pltpu.VMEM((HGRN_CHUNK, HEAD_DIM), F32),
            pltpu.VMEM((HGRN_CHUNK, HEAD_DIM), F32),
            pltpu.VMEM((HGRN_CHUNK, HEAD_DIM), F32),
            pltpu.VMEM((HGRN_CHUNK, HGRN_CHUNK), F32),
        ],
        compiler_params=_cparams(("parallel", "parallel")),
        name="hgrn_prompt",
    )(proj, proj, proj, proj, log_lb, log1m_lb, g_norm)


def _attn_prompt_kernel(sink_ref, q_ref, kc_ref, vc_ref, kp_ref, vp_ref, qn_ref, kn_ref,
                        o_ref, nk_ref, nv_ref):
    n = pl.program_id(1)
    W = WINDOW
    qi = lax.broadcasted_iota(jnp.int32, (W, 2 * W), 0)
    kj = lax.broadcasted_iota(jnp.int32, (W, 2 * W), 1)
    dist = W + qi - kj
    valid = (dist >= 0) & (dist <= W) & ((n > 0) | (kj >= W))
    dist_f = dist.astype(F32)
    q_norm = qn_ref[...]
    k_norm = kn_ref[...]

    for h in range(N_KV_HEADS):
        hs = slice(h * HEAD_DIM, (h + 1) * HEAD_DIM)
        kc = _rms_rows(kc_ref[:, hs], k_norm)
        kp = _rms_rows(kp_ref[:, hs], k_norm)
        k2 = jnp.concatenate([kp, kc], axis=0).astype(BF16)
        v2 = jnp.concatenate([vp_ref[:, hs], vc_ref[:, hs]], axis=0).astype(BF16)

        @pl.when(n == pl.num_programs(1) - 1)
        def _():
            nk_ref[:, hs] = kc
            nv_ref[:, hs] = vc_ref[:, hs]

        for g in range(GROUP):
            a = h * GROUP + g
            cs = slice(a * HEAD_DIM, (a + 1) * HEAD_DIM)
            qh = _rms_rows(q_ref[:, cs], q_norm).astype(BF16)
            s = lax.dot_general(qh, k2, (((1,), (1,)), ((), ())), preferred_element_type=F32)
            s = s * ATT_SCALE - ALIBI_SLOPES[a] * dist_f
            s = jnp.where(valid, s, -jnp.inf)
            sink = sink_ref[a]
            m = jnp.maximum(jnp.max(s, axis=-1, keepdims=True), sink)
            p = jnp.exp(s - m)
            den = jnp.sum(p, axis=-1, keepdims=True) + jnp.exp(sink - m)
            o = jnp.dot(p.astype(BF16), v2, preferred_element_type=F32) / den
            o_ref[:, cs] = o.astype(o_ref.dtype)


def _attn_prompt(proj, sinks, q_norm, k_norm):
    B, T, _ = proj.shape
    nb = T // WINDOW
    kcol = OFF_K_A // KV_WIDTH
    vcol = OFF_V_A // KV_WIDTH
    prev = lambda n: jnp.maximum(n - 1, 0)
    return pl.pallas_call(
        _attn_prompt_kernel,
        grid=(B, nb),
        in_specs=[
            pl.BlockSpec(memory_space=pltpu.SMEM),
            pl.BlockSpec((None, WINDOW, ATT_WIDTH), lambda b, n: (b, n, OFF_Q_A // ATT_WIDTH)),
            pl.BlockSpec((None, WINDOW, KV_WIDTH), lambda b, n: (b, n, kcol)),
            pl.BlockSpec((None, WINDOW, KV_WIDTH), lambda b, n: (b, n, vcol)),
            pl.BlockSpec((None, WINDOW, KV_WIDTH), lambda b, n: (b, prev(n), kcol)),
            pl.BlockSpec((None, WINDOW, KV_WIDTH), lambda b, n: (b, prev(n), vcol)),
            pl.BlockSpec((1, HEAD_DIM), lambda b, n: (0, 0)),
            pl.BlockSpec((1, HEAD_DIM), lambda b, n: (0, 0)),
        ],
        out_specs=(
            pl.BlockSpec((None, WINDOW, ATT_WIDTH), lambda b, n: (b, n, 0)),
            pl.BlockSpec((None, WINDOW, KV_WIDTH), lambda b, n: (b, 0, 0)),
            pl.BlockSpec((None, WINDOW, KV_WIDTH), lambda b, n: (b, 0, 0)),
        ),
        out_shape=(
            jax.ShapeDtypeStruct((B, T, ATT_WIDTH), BF16),
            jax.ShapeDtypeStruct((B, WINDOW, KV_WIDTH), F32),
            jax.ShapeDtypeStruct((B, WINDOW, KV_WIDTH), F32),
        ),
        compiler_params=_cparams(("parallel", "arbitrary")),
        name="attn_prompt",
    )(sinks, proj, proj, proj, proj, proj, q_norm, k_norm)


def _row_to_col(x_row, eye):
    return jnp.sum(jnp.where(eye, x_row, 0.0), axis=1, keepdims=True)


def _mix_sample_kernel(sink_ref, p_ref, s_ref, ck_ref, cv_ref, loglb_ref, log1m_ref,
                       gn_ref, qn_ref, kn_ref, o_ref, ns_ref, nk_ref, nv_ref):
    W = WINDOW
    er = lax.broadcasted_iota(jnp.int32, (HEAD_DIM, HEAD_DIM), 0)
    ec = lax.broadcasted_iota(jnp.int32, (HEAD_DIM, HEAD_DIM), 1)
    eye = er == ec
    g_norm = gn_ref[...]
    q_norm = qn_ref[...]
    k_norm = kn_ref[...]

    def cols(off, width=HEAD_DIM):
        return p_ref[:, off:off + width]

    for h in range(N_LIN_HEADS):
        c0 = h * HEAD_DIM
        q = _silu(cols(c0))
        logf = _log_forget(cols(LIN_WIDTH + c0), loglb_ref[:, c0:c0 + HEAD_DIM],
                           log1m_ref[:, c0:c0 + HEAD_DIM])
        f_col = _row_to_col(jnp.exp(logf), eye)
        k_col = 1.0 - f_col
        v = cols(2 * LIN_WIDTH + c0)
        s_new = s_ref[h] * f_col + k_col * v
        ns_ref[h] = s_new
        q8 = jnp.broadcast_to(q, (8, HEAD_DIM)).astype(BF16)
        o = jnp.dot(q8, s_new.astype(BF16), preferred_element_type=F32)[0:1, :]
        o = _rms_rows(o, g_norm) * _silu(cols(3 * LIN_WIDTH + c0))
        o_ref[:, c0:c0 + HEAD_DIM] = o

    row8 = lax.broadcasted_iota(jnp.int32, (8, 1), 0)
    row8_k = lax.broadcasted_iota(jnp.int32, (8, HEAD_DIM), 0)
    lane = lax.broadcasted_iota(jnp.int32, (8, W), 1)
    dist_c = (W - lane).astype(F32)
    rows_w = lax.broadcasted_iota(jnp.int32, (W, HEAD_DIM), 0)
    for h in range(N_KV_HEADS):
        hs = slice(h * HEAD_DIM, (h + 1) * HEAD_DIM)
        k_new = _rms_rows(cols(OFF_K_A + h * HEAD_DIM), k_norm)
        v_new = cols(OFF_V_A + h * HEAD_DIM)
        kc = ck_ref[:, hs]
        vc = cv_ref[:, hs]
        nk_ref[:, hs] = jnp.where(rows_w == W - 1, k_new, pltpu.roll(kc, W - 1, 0))
        nv_ref[:, hs] = jnp.where(rows_w == W - 1, v_new, pltpu.roll(vc, W - 1, 0))

        q4 = jnp.zeros((8, HEAD_DIM), F32)
        slope = jnp.zeros((8, 1), F32)
        sink = jnp.zeros((8, 1), F32)
        for g in range(GROUP):
            a = h * GROUP + g
            qg = _rms_rows(cols(OFF_Q_A + a * HEAD_DIM), q_norm)
            q4 = jnp.where(row8_k == g, qg, q4)
            slope = jnp.where(row8 == g, ALIBI_SLOPES[a], slope)
            sink = jnp.where(row8 == g, sink_ref[a], sink)
        q4b = q4.astype(BF16)
        s_c = lax.dot_general(q4b, kc.astype(BF16), (((1,), (1,)), ((), ())),
                              preferred_element_type=F32)
        s_c = s_c * ATT_SCALE - slope * dist_c
        s_n = jnp.sum(q4 * k_new, axis=-1, keepdims=True) * ATT_SCALE
        m = jnp.maximum(jnp.maximum(jnp.max(s_c, axis=-1, keepdims=True), s_n), sink)
        p_c = jnp.exp(s_c - m)
        p_n = jnp.exp(s_n - m)
        den = jnp.sum(p_c, axis=-1, keepdims=True) + p_n + jnp.exp(sink - m)
        o = jnp.dot(p_c.astype(BF16), vc.astype(BF16), preferred_element_type=F32)
        o = (o + p_n * v_new) / den
        for g in range(GROUP):
            a = h * GROUP + g
            o_ref[:, LIN_WIDTH + a * HEAD_DIM:LIN_WIDTH + (a + 1) * HEAD_DIM] = o[g:g + 1, :]


def _mix_sample(proj, state, cache_k, cache_v, sinks, log_lb, log1m_lb, g_norm, q_norm, k_norm):
    B = proj.shape[0]
    vec = pl.BlockSpec((1, HEAD_DIM), lambda b: (0, 0))
    lbs = pl.BlockSpec((1, LIN_WIDTH), lambda b: (0, 0))
    cache = pl.BlockSpec((None, WINDOW, KV_WIDTH), lambda b: (b, 0, 0))
    st = pl.BlockSpec((None, N_LIN_HEADS, HEAD_DIM, HEAD_DIM), lambda b: (b, 0, 0, 0))
    return pl.pallas_call(
        _mix_sample_kernel,
        grid=(B,),
        in_specs=[
            pl.BlockSpec(memory_space=pltpu.SMEM),
            pl.BlockSpec((None, 1, IN_WIDTH), lambda b: (b, 0, 0)),
            st, cache, cache, lbs, lbs, vec, vec, vec,
        ],
        out_specs=(
            pl.BlockSpec((None, 1, MIX_WIDTH), lambda b: (b, 0, 0)),
            st, cache, cache,
        ),
        out_shape=(
            jax.ShapeDtypeStruct((B, 1, MIX_WIDTH), F32),
            jax.ShapeDtypeStruct(state.shape, F32),
            jax.ShapeDtypeStruct((B, WINDOW, KV_WIDTH), F32),
            jax.ShapeDtypeStruct((B, WINDOW, KV_WIDTH), F32),
        ),
        compiler_params=_cparams(("parallel",)),
        name="mix_sample",
    )(sinks, proj, state, cache_k, cache_v, log_lb, log1m_lb, g_norm, q_norm, k_norm)


def _out_proj_kernel(ol_ref, oa_ref, w1_ref, w2_ref, x_ref, gt_ref, o_ref):
    acc = jnp.dot(ol_ref[...], w1_ref[...], preferred_element_type=F32)
    acc = acc + jnp.dot(oa_ref[...], w2_ref[...], preferred_element_type=F32)
    o_ref[...] = x_ref[...] + gt_ref[...] * acc


def _out_proj_prompt(o_l, o_a, w, x, mod, mod_off, tm, tn):
    G, R, _ = x.shape
    gcol = mod_off // tn
    return pl.pallas_call(
        _out_proj_kernel,
        grid=(G, R // tm, D_MODEL // tn),
        in_specs=[
            pl.BlockSpec((None, tm, LIN_WIDTH), lambda b, i, j: (b, i, 0)),
            pl.BlockSpec((None, tm, ATT_WIDTH), lambda b, i, j: (b, i, 0)),
            pl.BlockSpec((LIN_WIDTH, tn), lambda b, i, j: (0, j)),
            pl.BlockSpec((ATT_WIDTH, tn), lambda b, i, j: (1, j)),
            pl.BlockSpec((None, tm, tn), lambda b, i, j: (b, i, j)),
            pl.BlockSpec((None, 1, tn), lambda b, i, j: (b, 0, gcol + j)),
        ],
        out_specs=pl.BlockSpec((None, tm, tn), lambda b, i, j: (b, i, j)),
        out_shape=jax.ShapeDtypeStruct((G, R, D_MODEL), F32),
        compiler_params=_cparams(("parallel", "parallel", "arbitrary")),
        name="out_proj",
    )(o_l, o_a, w, w, x, mod)


P_TM = 1024
P_TN_IN = 512
P_TN_OUT = 512
P_TN_FF = 512
P_TN_DOWN = 256
S_TN = 512


def kernel(x_prompt, x_sample, cache_k, cache_v, state_hgrn, c_prompt, c_sample, lower_bounds,
           w_ada, b_ada, norm_mix, w_in, q_norm, k_norm, attn_sinks, g_norm, w_out, norm_ffn,
           w_gate, w_up, w_down):
    B, T, _ = x_prompt.shape
    BS = x_sample.shape[0]

    log_lb, log1m_lb = _lower_bounds(lower_bounds)

    c_all = jnp.concatenate(
        [c_prompt, c_sample, jnp.zeros((MOD_ROWS - B - BS, D_MODEL), F32)], axis=0)
    mod = _modulation(c_all, w_ada, b_ada)

    w_in_b = w_in.astype(BF16)
    w_out_b = w_out.astype(BF16)
    w_gate_b = w_gate.astype(BF16)
    w_up_b = w_up.astype(BF16)
    w_down_b = w_down.astype(BF16)

    yp = x_prompt
    ys = x_sample.reshape(1, BS, D_MODEL)
    pk, pv, ps, sk, sv, ss = [], [], [], [], [], []
    for l in range(DEPTH):
        mod_p = mod[l, :B].reshape(B, 1, 6 * D_MODEL)
        mod_s = mod[l, B:B + BS].reshape(1, BS, 6 * D_MODEL)
        gain_mix = norm_mix[l].reshape(1, D_MODEL)
        gain_ffn = norm_ffn[l].reshape(1, D_MODEL)
        llb = log_lb[l].reshape(1, LIN_WIDTH)
        l1m = log1m_lb[l].reshape(1, LIN_WIDTH)
        gn = g_norm[l].reshape(1, HEAD_DIM)
        qn = q_norm[l].reshape(1, HEAD_DIM)
        kn = k_norm[l].reshape(1, HEAD_DIM)
        sinks = attn_sinks[l].astype(F32)

        proj = _norm_matmul(yp, mod_p, 0, gain_mix, w_in_b[l], P_TM, P_TN_IN)
        o_l, s_p = _hgrn_prompt(proj, llb, l1m, gn)
        o_a, k_p, v_p = _attn_prompt(proj, sinks, qn, kn)
        yp = _out_proj_prompt(o_l, o_a, w_out_b[l], yp, mod_p, 2 * D_MODEL, P_TM, P_TN_OUT)
        ff = _norm_gate_up(yp, mod_p, 3, gain_ffn, w_gate_b[l], w_up_b[l], P_TM, P_TN_FF)
        yp = _matmul_residual(ff, w_down_b[l], yp, mod_p, 5 * D_MODEL, P_TM, P_TN_DOWN)
        pk.append(k_p.reshape(B, WINDOW, N_KV_HEADS, HEAD_DIM))
        pv.append(v_p.reshape(B, WINDOW, N_KV_HEADS, HEAD_DIM))
        ps.append(s_p)

        proj_s = _norm_matmul(ys, mod_s, 0, gain_mix, w_in_b[l], BS, S_TN)
        mixed_s, s_s, k_s, v_s = _mix_sample(
            proj_s.reshape(BS, 1, IN_WIDTH), state_hgrn[l],
            cache_k[l].reshape(BS, WINDOW, KV_WIDTH), cache_v[l].reshape(BS, WINDOW, KV_WIDTH),
            sinks, llb, l1m, gn, qn, kn)
        ys = _matmul_residual(mixed_s.reshape(1, BS, MIX_WIDTH), w_out_b[l], ys, mod_s,
                              2 * D_MODEL, BS, S_TN)
        ff_s = _norm_gate_up(ys, mod_s, 3, gain_ffn, w_gate_b[l], w_up_b[l], BS, S_TN)
        ys = _matmul_residual(ff_s, w_down_b[l], ys, mod_s, 5 * D_MODEL, BS, S_TN)
        sk.append(k_s.reshape(BS, WINDOW, N_KV_HEADS, HEAD_DIM))
        sv.append(v_s.reshape(BS, WINDOW, N_KV_HEADS, HEAD_DIM))
        ss.append(s_s)

    return (yp, ys.reshape(BS, 1, D_MODEL), jnp.stack(pk), jnp.stack(pv), jnp.stack(ps),
            jnp.stack(sk), jnp.stack(sv), jnp.stack(ss))
```

```python
import functools
import math

import jax
import jax.numpy as jnp
from jax import lax
from jax.experimental import pallas as pl
from jax.experimental.pallas import tpu as pltpu

F32 = jnp.float32
BF16 = jnp.bfloat16

D_MODEL = 2048
DEPTH = 4
HEAD_DIM = 128
N_LIN_HEADS = 8
LIN_WIDTH = N_LIN_HEADS * HEAD_DIM
N_ATT_HEADS = 8
N_KV_HEADS = 2
GROUP = N_ATT_HEADS // N_KV_HEADS
ATT_WIDTH = N_ATT_HEADS * HEAD_DIM
KV_WIDTH = N_KV_HEADS * HEAD_DIM
MIX_WIDTH = LIN_WIDTH + ATT_WIDTH
WINDOW = 128
D_FF = 5632
IN_WIDTH = 4 * LIN_WIDTH + ATT_WIDTH + 2 * KV_WIDTH
EPS = 1e-6
ATT_SCALE = HEAD_DIM ** -0.5
ALIBI_SLOPES = tuple(2.0 ** (-8.0 * (a + 1) / N_ATT_HEADS) for a in range(N_ATT_HEADS))

COL_Q_L = 0
COL_F_L = LIN_WIDTH // HEAD_DIM
COL_I_L = 2 * LIN_WIDTH // HEAD_DIM
COL_G_L = 3 * LIN_WIDTH // HEAD_DIM
OFF_Q_A = 4 * LIN_WIDTH
OFF_K_A = OFF_Q_A + ATT_WIDTH
OFF_V_A = OFF_K_A + KV_WIDTH

MOD_ROWS = 40
HGRN_CHUNK = 128
HGRN_SUB = 8
LOG2E = math.log2(math.e)
BF16_ROWS = 16

VMEM_LIMIT = 56 * 1024 * 1024


def _cparams(sem):
    return pltpu.CompilerParams(dimension_semantics=sem, vmem_limit_bytes=VMEM_LIMIT)


def _silu(x):
    return x / (1.0 + jnp.exp(-x))


def _rms_rows(x, g):
    ms = jnp.mean(x * x, axis=-1, keepdims=True)
    return x * lax.rsqrt(ms + EPS) * g


def _forget_gate(f_raw, lb):
    t = jnp.exp(-jnp.abs(f_raw))
    inv = 1.0 / (1.0 + t)
    sig = jnp.where(f_raw >= 0.0, inv, t * inv)
    return t, (1.0 - lb) * sig


def _log_forget(f_raw, lb):
    t, fp = _forget_gate(f_raw, lb)
    log_sig = jnp.minimum(f_raw, 0.0) - jnp.log(1.0 + t)
    return jnp.where(lb > 0.0, jnp.log(lb + fp), log_sig), fp


def _lb_kernel(lb_ref, o_ref):
    x = lb_ref[...]
    m = jnp.max(x, axis=0, keepdims=True)
    e = jnp.exp(x - m)
    p = e / jnp.sum(e, axis=0, keepdims=True)
    acc = jnp.zeros_like(p[0:1])
    for l in range(DEPTH):
        if l > 0:
            acc = acc + p[l:l + 1]
        o_ref[l:l + 1, :] = acc


def _lower_bounds(lower_bounds):
    return pl.pallas_call(
        _lb_kernel,
        out_shape=jax.ShapeDtypeStruct((DEPTH, LIN_WIDTH), F32),
        name="lower_bounds",
    )(lower_bounds.astype(F32))


ADA_TN = 1024


def _ada_kernel(c_ref, w_ref, b_ref, o_ref):
    s = _silu(c_ref[...]).astype(BF16)
    acc = jnp.dot(s, w_ref[...].astype(BF16), preferred_element_type=F32)
    o_ref[...] = acc + b_ref[...]


def _modulation(c_all, w_ada, b_ada):
    n = 6 * D_MODEL
    return pl.pallas_call(
        _ada_kernel,
        grid=(DEPTH, n // ADA_TN),
        in_specs=[
            pl.BlockSpec((MOD_ROWS, D_MODEL), lambda l, j: (0, 0)),
            pl.BlockSpec((None, D_MODEL, ADA_TN), lambda l, j: (l, 0, j)),
            pl.BlockSpec((None, 1, ADA_TN), lambda l, j: (l, 0, j)),
        ],
        out_specs=pl.BlockSpec((None, MOD_ROWS, ADA_TN), lambda l, j: (l, 0, j)),
        out_shape=jax.ShapeDtypeStruct((DEPTH, MOD_ROWS, n), F32),
        compiler_params=_cparams(("parallel", "parallel")),
        name="modulation",
    )(c_all, w_ada, b_ada.reshape(DEPTH, 1, n))


NORM_ROWS = 256


def _modulated_norm_into(h_scr, x_ref, sh_ref, sc_ref, g_ref):
    rows = x_ref.shape[0]
    ch = min(rows, NORM_ROWS)
    per_row = sc_ref.shape[0] != 1
    gain = g_ref[...]

    def body(c, carry):
        r0 = pl.multiple_of(c * ch, ch)
        rs = pl.ds(r0, ch)
        sc = sc_ref[rs, :] if per_row else sc_ref[...]
        sh = sh_ref[rs, :] if per_row else sh_ref[...]
        y = _rms_rows(x_ref[rs, :], gain)
        h_scr[rs, :] = (y * (1.0 + sc) + sh).astype(BF16)
        return carry

    lax.fori_loop(0, rows // ch, body, 0)


def _norm_mm_kernel(x_ref, sh_ref, sc_ref, g_ref, w_ref, o_ref, h_scr):
    @pl.when(pl.program_id(2) == 0)
    def _():
        _modulated_norm_into(h_scr, x_ref, sh_ref, sc_ref, g_ref)

    o_ref[...] = jnp.dot(h_scr[...], w_ref[...], preferred_element_type=F32).astype(o_ref.dtype)


def _norm_matmul(x, mod, mod_col, gain, w, layer, tm, tn):
    G, R, _ = x.shape
    r = mod.shape[1]
    rb = 1 if r == 1 else tm
    n = w.shape[2]
    row_idx = (lambda b, i: 0) if r == 1 else (lambda b, i: i)
    return pl.pallas_call(
        _norm_mm_kernel,
        grid=(G, R // tm, n // tn),
        in_specs=[
            pl.BlockSpec((None, tm, D_MODEL), lambda b, i, j: (b, i, 0)),
            pl.BlockSpec((None, rb, D_MODEL), lambda b, i, j: (b, row_idx(b, i), mod_col)),
            pl.BlockSpec((None, rb, D_MODEL), lambda b, i, j: (b, row_idx(b, i), mod_col + 1)),
            pl.BlockSpec((1, D_MODEL), lambda b, i, j: (0, 0)),
            pl.BlockSpec((None, D_MODEL, tn), lambda b, i, j: (layer, 0, j)),
        ],
        out_specs=pl.BlockSpec((None, tm, tn), lambda b, i, j: (b, i, j)),
        out_shape=jax.ShapeDtypeStruct((G, R, n), F32),
        scratch_shapes=[pltpu.VMEM((tm, D_MODEL), BF16)],
        compiler_params=_cparams(("parallel", "parallel", "arbitrary")),
        name="norm_in_proj",
    )(x, mod, mod, gain, w)


def _norm_gu_kernel(x_ref, sh_ref, sc_ref, g_ref, wg_ref, wu_ref, o_ref, h_scr):
    @pl.when(pl.program_id(2) == 0)
    def _():
        _modulated_norm_into(h_scr, x_ref, sh_ref, sc_ref, g_ref)

    h = h_scr[...]
    gate = jnp.dot(h, wg_ref[...], preferred_element_type=F32)
    up = jnp.dot(h, wu_ref[...], preferred_element_type=F32)
    o_ref[...] = (_silu(gate) * up).astype(o_ref.dtype)


def _norm_gate_up(x, mod, mod_col, gain, wg, wu, layer, tm, tn):
    G, R, _ = x.shape
    r = mod.shape[1]
    rb = 1 if r == 1 else tm
    row_idx = (lambda b, i: 0) if r == 1 else (lambda b, i: i)
    return pl.pallas_call(
        _norm_gu_kernel,
        grid=(G, R // tm, D_FF // tn),
        in_specs=[
            pl.BlockSpec((None, tm, D_MODEL), lambda b, i, j: (b, i, 0)),
            pl.BlockSpec((None, rb, D_MODEL), lambda b, i, j: (b, row_idx(b, i), mod_col)),
            pl.BlockSpec((None, rb, D_MODEL), lambda b, i, j: (b, row_idx(b, i), mod_col + 1)),
            pl.BlockSpec((1, D_MODEL), lambda b, i, j: (0, 0)),
            pl.BlockSpec((None, D_MODEL, tn), lambda b, i, j: (layer, 0, j)),
            pl.BlockSpec((None, D_MODEL, tn), lambda b, i, j: (layer, 0, j)),
        ],
        out_specs=pl.BlockSpec((None, tm, tn), lambda b, i, j: (b, i, j)),
        out_shape=jax.ShapeDtypeStruct((G, R, D_FF), BF16),
        scratch_shapes=[pltpu.VMEM((tm, D_MODEL), BF16)],
        compiler_params=_cparams(("parallel", "parallel", "arbitrary")),
        name="norm_gate_up",
    )(x, mod, mod, gain, wg, wu)


def _mm_res_kernel(a_ref, w_ref, x_ref, gt_ref, o_ref):
    acc = jnp.dot(a_ref[...].astype(BF16), w_ref[...], preferred_element_type=F32)
    o_ref[...] = x_ref[...] + gt_ref[...] * acc


def _matmul_residual(a, w, layer, x, mod, mod_off, tm, tn):
    G, R, K = a.shape
    r = mod.shape[1]
    rb = 1 if r == 1 else tm
    row_idx = (lambda b, i: 0) if r == 1 else (lambda b, i: i)
    gcol = mod_off // tn
    return pl.pallas_call(
        _mm_res_kernel,
        grid=(G, R // tm, D_MODEL // tn),
        in_specs=[
            pl.BlockSpec((None, tm, K), lambda b, i, j: (b, i, 0)),
            pl.BlockSpec((None, K, tn), lambda b, i, j: (layer, 0, j)),
            pl.BlockSpec((None, tm, tn), lambda b, i, j: (b, i, j)),
            pl.BlockSpec((None, rb, tn), lambda b, i, j: (b, row_idx(b, i), gcol + j)),
        ],
        out_specs=pl.BlockSpec((None, tm, tn), lambda b, i, j: (b, i, j)),
        out_shape=jax.ShapeDtypeStruct((G, R, D_MODEL), F32),
        compiler_params=_cparams(("parallel", "parallel", "arbitrary")),
        name="matmul_residual",
    )(a, w, x, mod)


def _split3_bf16(x):
    hi = x.astype(BF16)
    r1 = x - hi.astype(F32)
    mid = r1.astype(BF16)
    lo = (r1 - mid.astype(F32)).astype(BF16)
    return hi, mid, lo


HGRN_HP = 4
HGRN_TB = 1024
HGRN_SEP_MIN = 32


def _hgrn_prompt_kernel(q_ref, f_ref, i_ref, g_ref, lb_ref, gn_ref, o_ref, s_ref, st_scr):
    C = HGRN_CHUNK
    P = HGRN_SUB
    D = HEAD_DIM
    t_idx = pl.program_id(2)
    n_chunks = q_ref.shape[0] // C
    row = lax.broadcasted_iota(jnp.int32, (C, C), 0)
    col = lax.broadcasted_iota(jnp.int32, (C, C), 1)
    tri = (row >= col).astype(BF16)
    sel_r = lax.broadcasted_iota(jnp.int32, (P * D, C), 0)
    sel_c = lax.broadcasted_iota(jnp.int32, (P * D, C), 1)
    psh = P.bit_length() - 1
    sel = ((sel_c & (P - 1)) == (sel_r >> (D.bit_length() - 1))).astype(BF16)
    keep_diag = ((col >> psh) == (row >> psh)) & ((col & (P - 1)) <= (row & (P - 1)))
    g_norm = gn_ref[...]

    levels = []
    L = C // 2
    while L >= P:
        levels.append(L)
        L //= 2

    @pl.when(t_idx == 0)
    def _():
        st_scr[...] = jnp.zeros_like(st_scr)

    def chunk_body(c, carry):
        rs = pl.ds(pl.multiple_of(c * C, C), C)
        qs, ks, vs, logfs, parts = [], [], [], [], []
        for hh in range(HGRN_HP):
            cs = slice(hh * D, (hh + 1) * D)
            lb = lb_ref[:, cs]
            logf, fp = _log_forget(f_ref[rs, cs], lb)
            qs.append(_silu(q_ref[rs, cs]))
            ks.append((1.0 - lb) - fp)
            vs.append(i_ref[rs, cs].astype(BF16))
            parts.extend(_split3_bf16(logf * LOG2E))
        cum = jnp.dot(tri, jnp.concatenate(parts, axis=1), preferred_element_type=F32)

        for hh in range(HGRN_HP):
            cs = slice(hh * D, (hh + 1) * D)
            q, k, v16 = qs[hh], ks[hh], vs[hh]
            b = (cum[:, (3 * hh) * D:(3 * hh + 1) * D] + cum[:, (3 * hh + 1) * D:(3 * hh + 2) * D]
                 + cum[:, (3 * hh + 2) * D:(3 * hh + 3) * D])
            b_end = b[C - 1:C, :]
            st = st_scr[hh]
            o = lax.dot_general((q * jnp.exp2(b)).astype(BF16), st.astype(BF16),
                                (((1,), (1,)), ((), ())), preferred_element_type=F32)

            q_sep, k_sep = [], []
            a_masked = []
            for L in levels:
                q_rows, k_rows = [], []
                for p in range(C // (2 * L)):
                    lo, mid, hi = 2 * L * p, 2 * L * p + L, 2 * L * (p + 1)
                    ref = b[mid - 1:mid, :]
                    zf = jnp.zeros((L, D), F32)
                    kk = jnp.concatenate([k[lo:mid] * jnp.exp2(ref - b[lo:mid]), zf], axis=0)
                    qq = jnp.concatenate([zf, q[mid:hi] * jnp.exp2(b[mid:hi] - ref)], axis=0)
                    q_rows.append(qq.astype(BF16))
                    k_rows.append(kk.astype(BF16))
                if L >= HGRN_SEP_MIN:
                    for p in range(len(q_rows)):
                        pad = lambda x: jnp.concatenate(
                            [y for y in (jnp.zeros((2 * L * p, D), BF16), x,
                                         jnp.zeros((C - 2 * L * (p + 1), D), BF16)) if y.shape[0] > 0],
                            axis=0)
                        q_sep.append(pad(q_rows[p]))
                        k_sep.append(pad(k_rows[p]))
                else:
                    sh = L.bit_length() - 1
                    a_l = lax.dot_general(jnp.concatenate(q_rows, axis=0),
                                          jnp.concatenate(k_rows, axis=0),
                                          (((1,), (1,)), ((), ())), preferred_element_type=F32)
                    keep = (((row >> sh) & 1) == 1) & ((col >> sh) == (row >> sh) - 1)
                    a_masked.append((keep, a_l))
            a = lax.dot_general(jnp.concatenate(q_sep, axis=1), jnp.concatenate(k_sep, axis=1),
                                (((1,), (1,)), ((), ())), preferred_element_type=F32)
            for keep, a_l in a_masked:
                a = jnp.where(keep, a_l, a)

            b3 = b.reshape(C // P, P, D)
            k3 = k.reshape(C // P, P, D)
            zs = []
            for s in range(P):
                bs = jnp.broadcast_to(b3[:, s:s + 1, :], (C // P, P, D)).reshape(C, D)
                ksb = jnp.broadcast_to(k3[:, s:s + 1, :], (C // P, P, D)).reshape(C, D)
                e = jnp.exp2(jnp.minimum(b - bs, 0.0))
                zs.append((q * ksb * e).astype(BF16))
            a_d = jnp.dot(jnp.concatenate(zs, axis=1), sel, preferred_element_type=F32)
            a = jnp.where(keep_diag, a_d, a)

            o = o + jnp.dot(a.astype(BF16), v16, preferred_element_type=F32)

            k_end = (k * jnp.exp2(b_end - b)).astype(BF16)
            upd = lax.dot_general(v16, k_end, (((0,), (0,)), ((), ())), preferred_element_type=F32)
            st_scr[hh] = st * jnp.exp2(b_end) + upd

            o = _rms_rows(o, g_norm) * _silu(g_ref[rs, cs])
            o_ref[rs, cs] = o.astype(o_ref.dtype)
        return carry

    lax.fori_loop(0, n_chunks, chunk_body, 0)

    @pl.when(t_idx == pl.num_programs(2) - 1)
    def _():
        for hh in range(HGRN_HP):
            s_ref[hh] = st_scr[hh].T


def _hgrn_prompt(proj, lb, g_norm):
    B, T, _ = proj.shape
    W = HGRN_HP * HEAD_DIM
    seq = lambda off: pl.BlockSpec((None, HGRN_TB, W), lambda b, h, t: (b, t, off // HGRN_HP + h))
    return pl.pallas_call(
        _hgrn_prompt_kernel,
        grid=(B, N_LIN_HEADS // HGRN_HP, T // HGRN_TB),
        in_specs=[
            seq(COL_Q_L), seq(COL_F_L), seq(COL_I_L), seq(COL_G_L),
            pl.BlockSpec((1, W), lambda b, h, t: (0, h)),
            pl.BlockSpec((1, HEAD_DIM), lambda b, h, t: (0, 0)),
        ],
        out_specs=(
            pl.BlockSpec((None, HGRN_TB, W), lambda b, h, t: (b, t, h)),
            pl.BlockSpec((None, HGRN_HP, HEAD_DIM, HEAD_DIM), lambda b, h, t: (b, h, 0, 0)),
        ),
        out_shape=(
            jax.ShapeDtypeStruct((B, T, LIN_WIDTH), BF16),
            jax.ShapeDtypeStruct((B, N_LIN_HEADS, HEAD_DIM, HEAD_DIM), F32),
        ),
        scratch_shapes=[pltpu.VMEM((HGRN_HP, HEAD_DIM, HEAD_DIM), F32)],
        compiler_params=_cparams(("parallel", "parallel", "arbitrary")),
        name="hgrn_prompt",
    )(proj, proj, proj, proj, lb, g_norm)


def _attn_prompt_kernel(sink_ref, q_ref, kc_ref, vc_ref, kp_ref, vp_ref, qn_ref, kn_ref,
                        o_ref, nk_ref, nv_ref):
    n = pl.program_id(1)
    W = WINDOW
    qi = lax.broadcasted_iota(jnp.int32, (W, 2 * W), 0)
    kj = lax.broadcasted_iota(jnp.int32, (W, 2 * W), 1)
    dist = W + qi - kj
    valid = (dist >= 0) & (dist <= W) & ((n > 0) | (kj >= W))
    dist_f = dist.astype(F32)
    q_norm = qn_ref[...]
    k_norm = kn_ref[...]

    for h in range(N_KV_HEADS):
        hs = slice(h * HEAD_DIM, (h + 1) * HEAD_DIM)
        kc = _rms_rows(kc_ref[:, hs], k_norm)
        kp = _rms_rows(kp_ref[:, hs], k_norm)
        k2 = jnp.concatenate([kp, kc], axis=0).astype(BF16)
        v2 = jnp.concatenate([vp_ref[:, hs], vc_ref[:, hs]], axis=0).astype(BF16)

        @pl.when(n == pl.num_programs(1) - 1)
        def _():
            nk_ref[:, hs] = kc
            nv_ref[:, hs] = vc_ref[:, hs]

        for g in range(GROUP):
            a = h * GROUP + g
            cs = slice(a * HEAD_DIM, (a + 1) * HEAD_DIM)
            qh = _rms_rows(q_ref[:, cs], q_norm).astype(BF16)
            s = lax.dot_general(qh, k2, (((1,), (1,)), ((), ())), preferred_element_type=F32)
            s = s * ATT_SCALE - ALIBI_SLOPES[a] * dist_f
            s = jnp.where(valid, s, -jnp.inf)
            sink = sink_ref[a]
            m = jnp.maximum(jnp.max(s, axis=-1, keepdims=True), sink)
            p = jnp.exp(s - m)
            den = jnp.sum(p, axis=-1, keepdims=True) + jnp.exp(sink - m)
            o = jnp.dot(p.astype(BF16), v2, preferred_element_type=F32) / den
            o_ref[:, cs] = o.astype(o_ref.dtype)


def _attn_prompt(proj, sinks, q_norm, k_norm):
    B, T, _ = proj.shape
    nb = T // WINDOW
    kcol = OFF_K_A // KV_WIDTH
    vcol = OFF_V_A // KV_WIDTH
    prev = lambda n: jnp.maximum(n - 1, 0)
    return pl.pallas_call(
        _attn_prompt_kernel,
        grid=(B, nb),
        in_specs=[
            pl.BlockSpec(memory_space=pltpu.SMEM),
            pl.BlockSpec((None, WINDOW, ATT_WIDTH), lambda b, n: (b, n, OFF_Q_A // ATT_WIDTH)),
            pl.BlockSpec((None, WINDOW, KV_WIDTH), lambda b, n: (b, n, kcol)),
            pl.BlockSpec((None, WINDOW, KV_WIDTH), lambda b, n: (b, n, vcol)),
            pl.BlockSpec((None, WINDOW, KV_WIDTH), lambda b, n: (b, prev(n), kcol)),
            pl.BlockSpec((None, WINDOW, KV_WIDTH), lambda b, n: (b, prev(n), vcol)),
            pl.BlockSpec((1, HEAD_DIM), lambda b, n: (0, 0)),
            pl.BlockSpec((1, HEAD_DIM), lambda b, n: (0, 0)),
        ],
        out_specs=(
            pl.BlockSpec((None, WINDOW, ATT_WIDTH), lambda b, n: (b, n, 0)),
            pl.BlockSpec((None, WINDOW, KV_WIDTH), lambda b, n: (b, 0, 0)),
            pl.BlockSpec((None, WINDOW, KV_WIDTH), lambda b, n: (b, 0, 0)),
        ),
        out_shape=(
            jax.ShapeDtypeStruct((B, T, ATT_WIDTH), BF16),
            jax.ShapeDtypeStruct((B, WINDOW, KV_WIDTH), F32),
            jax.ShapeDtypeStruct((B, WINDOW, KV_WIDTH), F32),
        ),
        compiler_params=_cparams(("parallel", "arbitrary")),
        name="attn_prompt",
    )(sinks, proj, proj, proj, proj, proj, q_norm, k_norm)


def _row_to_col(x_row, eye):
    return jnp.sum(jnp.where(eye, x_row, 0.0), axis=1, keepdims=True)


def _mix_sample_kernel(sink_ref, p_ref, s_ref, ck_ref, cv_ref, lb_ref,
                       gn_ref, qn_ref, kn_ref, o_ref, ns_ref, nk_ref, nv_ref):
    W = WINDOW
    er = lax.broadcasted_iota(jnp.int32, (HEAD_DIM, HEAD_DIM), 0)
    ec = lax.broadcasted_iota(jnp.int32, (HEAD_DIM, HEAD_DIM), 1)
    eye = er == ec
    g_norm = gn_ref[...]
    q_norm = qn_ref[...]
    k_norm = kn_ref[...]

    def cols(off, width=HEAD_DIM):
        return p_ref[:, off:off + width]

    for h in range(N_LIN_HEADS):
        c0 = h * HEAD_DIM
        q = _silu(cols(c0))
        lb = lb_ref[:, c0:c0 + HEAD_DIM]
        _, fp = _forget_gate(cols(LIN_WIDTH + c0), lb)
        f_col = _row_to_col(lb + fp, eye)
        k_col = 1.0 - f_col
        v = cols(2 * LIN_WIDTH + c0)
        s_new = s_ref[h] * f_col + k_col * v
        ns_ref[h] = s_new
        q8 = jnp.broadcast_to(q, (8, HEAD_DIM)).astype(BF16)
        o = jnp.dot(q8, s_new.astype(BF16), preferred_element_type=F32)[0:1, :]
        o = _rms_rows(o, g_norm) * _silu(cols(3 * LIN_WIDTH + c0))
        o_ref[:, c0:c0 + HEAD_DIM] = o

    row8 = lax.broadcasted_iota(jnp.int32, (8, 1), 0)
    row8_k = lax.broadcasted_iota(jnp.int32, (8, HEAD_DIM), 0)
    lane = lax.broadcasted_iota(jnp.int32, (8, W), 1)
    dist_c = (W - lane).astype(F32)
    rows_w = lax.broadcasted_iota(jnp.int32, (W, HEAD_DIM), 0)
    for h in range(N_KV_HEADS):
        hs = slice(h * HEAD_DIM, (h + 1) * HEAD_DIM)
        k_new = _rms_rows(cols(OFF_K_A + h * HEAD_DIM), k_norm)
        v_new = cols(OFF_V_A + h * HEAD_DIM)
        kc = ck_ref[:, hs]
        vc = cv_ref[:, hs]
        nk_ref[:, hs] = jnp.where(rows_w == W - 1, k_new, pltpu.roll(kc, W - 1, 0))
        nv_ref[:, hs] = jnp.where(rows_w == W - 1, v_new, pltpu.roll(vc, W - 1, 0))

        q4 = jnp.zeros((8, HEAD_DIM), F32)
        slope = jnp.zeros((8, 1), F32)
        sink = jnp.zeros((8, 1), F32)
        for g in range(GROUP):
            a = h * GROUP + g
            qg = _rms_rows(cols(OFF_Q_A + a * HEAD_DIM), q_norm)
            q4 = jnp.where(row8_k == g, qg, q4)
            slope = jnp.where(row8 == g, ALIBI_SLOPES[a], slope)
            sink = jnp.where(row8 == g, sink_ref[a], sink)
        q4b = q4.astype(BF16)
        s_c = lax.dot_general(q4b, kc.astype(BF16), (((1,), (1,)), ((), ())),
                              preferred_element_type=F32)
        s_c = s_c * ATT_SCALE - slope * dist_c
        s_n = jnp.sum(q4 * k_new, axis=-1, keepdims=True) * ATT_SCALE
        m = jnp.maximum(jnp.maximum(jnp.max(s_c, axis=-1, keepdims=True), s_n), sink)
        p_c = jnp.exp(s_c - m)
        p_n = jnp.exp(s_n - m)
        den = jnp.sum(p_c, axis=-1, keepdims=True) + p_n + jnp.exp(sink - m)
        o = jnp.dot(p_c.astype(BF16), vc.astype(BF16), preferred_element_type=F32)
        o = (o + p_n * v_new) / den
        for g in range(GROUP):
            a = h * GROUP + g
            o_ref[:, LIN_WIDTH + a * HEAD_DIM:LIN_WIDTH + (a + 1) * HEAD_DIM] = o[g:g + 1, :]


def _mix_sample(proj, state, cache_k, cache_v, layer, sinks, lb, g_norm, q_norm, k_norm):
    B = proj.shape[0]
    vec = pl.BlockSpec((1, HEAD_DIM), lambda b: (0, 0))
    lbs = pl.BlockSpec((1, LIN_WIDTH), lambda b: (0, 0))
    cache_in = pl.BlockSpec((None, None, WINDOW, KV_WIDTH), lambda b: (layer, b, 0, 0))
    st_in = pl.BlockSpec((None, None, N_LIN_HEADS, HEAD_DIM, HEAD_DIM),
                         lambda b: (layer, b, 0, 0, 0))
    cache = pl.BlockSpec((None, WINDOW, KV_WIDTH), lambda b: (b, 0, 0))
    st = pl.BlockSpec((None, N_LIN_HEADS, HEAD_DIM, HEAD_DIM), lambda b: (b, 0, 0, 0))
    return pl.pallas_call(
        _mix_sample_kernel,
        grid=(B,),
        in_specs=[
            pl.BlockSpec(memory_space=pltpu.SMEM),
            pl.BlockSpec((None, 1, IN_WIDTH), lambda b: (b, 0, 0)),
            st_in, cache_in, cache_in, lbs, vec, vec, vec,
        ],
        out_specs=(
            pl.BlockSpec((None, 1, MIX_WIDTH), lambda b: (b, 0, 0)),
            st, cache, cache,
        ),
        out_shape=(
            jax.ShapeDtypeStruct((B, 1, MIX_WIDTH), F32),
            jax.ShapeDtypeStruct(state.shape[1:], F32),
            jax.ShapeDtypeStruct((B, WINDOW, KV_WIDTH), F32),
            jax.ShapeDtypeStruct((B, WINDOW, KV_WIDTH), F32),
        ),
        compiler_params=_cparams(("parallel",)),
        name="mix_sample",
    )(sinks, proj, state, cache_k, cache_v, lb, g_norm, q_norm, k_norm)


def _out_proj_kernel(ol_ref, oa_ref, w1_ref, w2_ref, x_ref, gt_ref, o_ref):
    acc = jnp.dot(ol_ref[...], w1_ref[...], preferred_element_type=F32)
    acc = acc + jnp.dot(oa_ref[...], w2_ref[...], preferred_element_type=F32)
    o_ref[...] = x_ref[...] + gt_ref[...] * acc


def _out_proj_prompt(o_l, o_a, w, layer, x, mod, mod_off, tm, tn):
    G, R, _ = x.shape
    gcol = mod_off // tn
    return pl.pallas_call(
        _out_proj_kernel,
        grid=(G, R // tm, D_MODEL // tn),
        in_specs=[
            pl.BlockSpec((None, tm, LIN_WIDTH), lambda b, i, j: (b, i, 0)),
            pl.BlockSpec((None, tm, ATT_WIDTH), lambda b, i, j: (b, i, 0)),
            pl.BlockSpec((None, LIN_WIDTH, tn), lambda b, i, j: (layer, 0, j)),
            pl.BlockSpec((None, ATT_WIDTH, tn), lambda b, i, j: (layer, 1, j)),
            pl.BlockSpec((None, tm, tn), lambda b, i, j: (b, i, j)),
            pl.BlockSpec((None, 1, tn), lambda b, i, j: (b, 0, gcol + j)),
        ],
        out_specs=pl.BlockSpec((None, tm, tn), lambda b, i, j: (b, i, j)),
        out_shape=jax.ShapeDtypeStruct((G, R, D_MODEL), F32),
        compiler_params=_cparams(("parallel", "parallel", "arbitrary")),
        name="out_proj",
    )(o_l, o_a, w, w, x, mod)


P_TM = 1024
P_TN_IN = 512
P_TN_OUT = 512
P_TN_FF = 512
P_TN_DOWN = 256
S_TN = 512


def kernel(x_prompt, x_sample, cache_k, cache_v, state_hgrn, c_prompt, c_sample, lower_bounds,
           w_ada, b_ada, norm_mix, w_in, q_norm, k_norm, attn_sinks, g_norm, w_out, norm_ffn,
           w_gate, w_up, w_down):
    B, T, _ = x_prompt.shape
    BS = x_sample.shape[0]

    lb_all = _lower_bounds(lower_bounds)

    c_all = jnp.concatenate(
        [c_prompt, c_sample, jnp.zeros((MOD_ROWS - B - BS, D_MODEL), F32)], axis=0)
    mod = _modulation(c_all, w_ada, b_ada)

    w_in_b = w_in.astype(BF16)
    w_out_b = w_out.astype(BF16)
    w_gate_b = w_gate.astype(BF16)
    w_up_b = w_up.astype(BF16)
    w_down_b = w_down.astype(BF16)

    cache_k2 = cache_k.reshape(DEPTH, BS, WINDOW, KV_WIDTH)
    cache_v2 = cache_v.reshape(DEPTH, BS, WINDOW, KV_WIDTH)

    yp = x_prompt
    ys = x_sample.reshape(1, BS, D_MODEL)
    pk, pv, ps, sk, sv, ss = [], [], [], [], [], []
    for l in range(DEPTH):
        mod_p = mod[l, :B].reshape(B, 1, 6 * D_MODEL)
        mod_s = mod[l, B:B + BS].reshape(1, BS, 6 * D_MODEL)
        gain_mix = norm_mix[l].reshape(1, D_MODEL)
        gain_ffn = norm_ffn[l].reshape(1, D_MODEL)
        lb = lb_all[l].reshape(1, LIN_WIDTH)
        gn = g_norm[l].reshape(1, HEAD_DIM)
        qn = q_norm[l].reshape(1, HEAD_DIM)
        kn = k_norm[l].reshape(1, HEAD_DIM)
        sinks = attn_sinks[l].astype(F32)

        proj = _norm_matmul(yp, mod_p, 0, gain_mix, w_in_b, l, P_TM, P_TN_IN)
        o_l, s_p = _hgrn_prompt(proj, lb, gn)
        o_a, k_p, v_p = _attn_prompt(proj, sinks, qn, kn)
        yp = _out_proj_prompt(o_l, o_a, w_out_b, l, yp, mod_p, 2 * D_MODEL, P_TM, P_TN_OUT)
        ff = _norm_gate_up(yp, mod_p, 3, gain_ffn, w_gate_b, w_up_b, l, P_TM, P_TN_FF)
        yp = _matmul_residual(ff, w_down_b, l, yp, mod_p, 5 * D_MODEL, P_TM, P_TN_DOWN)
        pk.append(k_p.reshape(B, WINDOW, N_KV_HEADS, HEAD_DIM))
        pv.append(v_p.reshape(B, WINDOW, N_KV_HEADS, HEAD_DIM))
        ps.append(s_p)

        proj_s = _norm_matmul(ys, mod_s, 0, gain_mix, w_in_b, l, BS, S_TN)
        mixed_s, s_s, k_s, v_s = _mix_sample(
            proj_s.reshape(BS, 1, IN_WIDTH), state_hgrn, cache_k2, cache_v2, l,
            sinks, lb, gn, qn, kn)
        ys = _matmul_residual(mixed_s.reshape(1, BS, MIX_WIDTH), w_out_b, l, ys, mod_s,
                              2 * D_MODEL, BS, S_TN)
        ff_s = _norm_gate_up(ys, mod_s, 3, gain_ffn, w_gate_b, w_up_b, l, BS, S_TN)
        ys = _matmul_residual(ff_s, w_down_b, l, ys, mod_s, 5 * D_MODEL, BS, S_TN)
        sk.append(k_s.reshape(BS, WINDOW, N_KV_HEADS, HEAD_DIM))
        sv.append(v_s.reshape(BS, WINDOW, N_KV_HEADS, HEAD_DIM))
        ss.append(s_s)

    return (yp, ys.reshape(BS, 1, D_MODEL), jnp.stack(pk), jnp.stack(pv), jnp.stack(ps),
            jnp.stack(sk), jnp.stack(sv), jnp.stack(ss))
```

```python
import functools
import math

import jax
import jax.numpy as jnp
from jax import lax
from jax.experimental import pallas as pl
from jax.experimental.pallas import tpu as pltpu

F32 = jnp.float32
BF16 = jnp.bfloat16

D_MODEL = 2048
DEPTH = 4
HEAD_DIM = 128
N_LIN_HEADS = 8
LIN_WIDTH = N_LIN_HEADS * HEAD_DIM
N_ATT_HEADS = 8
N_KV_HEADS = 2
GROUP = N_ATT_HEADS // N_KV_HEADS
ATT_WIDTH = N_ATT_HEADS * HEAD_DIM
KV_WIDTH = N_KV_HEADS * HEAD_DIM
MIX_WIDTH = LIN_WIDTH + ATT_WIDTH
WINDOW = 128
D_FF = 5632
IN_WIDTH = 4 * LIN_WIDTH + ATT_WIDTH + 2 * KV_WIDTH
EPS = 1e-6
ATT_SCALE = HEAD_DIM ** -0.5
ALIBI_SLOPES = tuple(2.0 ** (-8.0 * (a + 1) / N_ATT_HEADS) for a in range(N_ATT_HEADS))

COL_Q_L = 0
COL_F_L = LIN_WIDTH // HEAD_DIM
COL_I_L = 2 * LIN_WIDTH // HEAD_DIM
COL_G_L = 3 * LIN_WIDTH // HEAD_DIM
OFF_Q_A = 4 * LIN_WIDTH
OFF_K_A = OFF_Q_A + ATT_WIDTH
OFF_V_A = OFF_K_A + KV_WIDTH

MOD_ROWS = 40
HGRN_CHUNK = 128
HGRN_SUB = 8
LOG2E = math.log2(math.e)
BF16_ROWS = 16

VMEM_LIMIT = 56 * 1024 * 1024


def _cparams(sem):
    return pltpu.CompilerParams(dimension_semantics=sem, vmem_limit_bytes=VMEM_LIMIT)


def _silu(x):
    return x / (1.0 + jnp.exp(-x))


def _rms_rows(x, g):
    ms = jnp.mean(x * x, axis=-1, keepdims=True)
    return x * lax.rsqrt(ms + EPS) * g


def _forget_gate(f_raw, lb):
    t = jnp.exp(-jnp.abs(f_raw))
    inv = 1.0 / (1.0 + t)
    sig = jnp.where(f_raw >= 0.0, inv, t * inv)
    return t, (1.0 - lb) * sig


def _log_forget(f_raw, lb):
    t, fp = _forget_gate(f_raw, lb)
    log_sig = jnp.minimum(f_raw, 0.0) - jnp.log(1.0 + t)
    return jnp.where(lb > 0.0, jnp.log(lb + fp), log_sig), fp


def _lb_kernel(lb_ref, o_ref):
    x = lb_ref[...]
    m = jnp.max(x, axis=0, keepdims=True)
    e = jnp.exp(x - m)
    p = e / jnp.sum(e, axis=0, keepdims=True)
    acc = jnp.zeros_like(p[0:1])
    for l in range(DEPTH):
        if l > 0:
            acc = acc + p[l:l + 1]
        o_ref[l:l + 1, :] = acc


def _lower_bounds(lower_bounds):
    return pl.pallas_call(
        _lb_kernel,
        out_shape=jax.ShapeDtypeStruct((DEPTH, LIN_WIDTH), F32),
        name="lower_bounds",
    )(lower_bounds.astype(F32))


ADA_TN = 1024


def _ada_kernel(c_ref, w_ref, b_ref, o_ref):
    s = _silu(c_ref[...]).astype(BF16)
    acc = jnp.dot(s, w_ref[...].astype(BF16), preferred_element_type=F32)
    o_ref[...] = acc + b_ref[...]


def _modulation(c_all, w_ada, b_ada):
    n = 6 * D_MODEL
    return pl.pallas_call(
        _ada_kernel,
        grid=(DEPTH, n // ADA_TN),
        in_specs=[
            pl.BlockSpec((MOD_ROWS, D_MODEL), lambda l, j: (0, 0)),
            pl.BlockSpec((None, D_MODEL, ADA_TN), lambda l, j: (l, 0, j)),
            pl.BlockSpec((None, 1, ADA_TN), lambda l, j: (l, 0, j)),
        ],
        out_specs=pl.BlockSpec((None, MOD_ROWS, ADA_TN), lambda l, j: (l, 0, j)),
        out_shape=jax.ShapeDtypeStruct((DEPTH, MOD_ROWS, n), F32),
        compiler_params=_cparams(("parallel", "parallel")),
        name="modulation",
    )(c_all, w_ada, b_ada.reshape(DEPTH, 1, n))


NORM_ROWS = 16
NORM_UNROLL = 4


def _modulated_norm_into(h_scr, x_ref, sh_ref, sc_ref, g_ref):
    rows = x_ref.shape[0]
    ch = min(rows, NORM_ROWS)
    per_row = sc_ref.shape[0] != 1
    gain = g_ref[...]
    if not per_row:
        gain = gain * (1.0 + sc_ref[...])
        shift = sh_ref[...]

    def body(c, carry):
        rs = pl.ds(pl.multiple_of(c * ch, ch), ch)
        x = x_ref[rs, :]
        inv = lax.rsqrt(jnp.mean(x * x, axis=-1, keepdims=True) + EPS)
        if per_row:
            h = (x * inv * gain) * (1.0 + sc_ref[rs, :]) + sh_ref[rs, :]
        else:
            h = (x * inv) * gain + shift
        h_scr[rs, :] = h.astype(BF16)
        return carry

    lax.fori_loop(0, rows // ch, body, 0, unroll=min(NORM_UNROLL, rows // ch))


def _norm_mm_kernel(x_ref, sh_ref, sc_ref, g_ref, w_ref, o_ref, h_scr):
    @pl.when(pl.program_id(2) == 0)
    def _():
        _modulated_norm_into(h_scr, x_ref, sh_ref, sc_ref, g_ref)

    o_ref[...] = jnp.dot(h_scr[...], w_ref[...].astype(BF16),
                         preferred_element_type=F32).astype(o_ref.dtype)


def _norm_matmul(x, mod, mod_col, gain, w, layer, tm, tn):
    G, R, _ = x.shape
    r = mod.shape[1]
    rb = 1 if r == 1 else tm
    n = w.shape[2]
    row_idx = (lambda b, i: 0) if r == 1 else (lambda b, i: i)
    return pl.pallas_call(
        _norm_mm_kernel,
        grid=(G, R // tm, n // tn),
        in_specs=[
            pl.BlockSpec((None, tm, D_MODEL), lambda b, i, j: (b, i, 0)),
            pl.BlockSpec((None, rb, D_MODEL), lambda b, i, j: (b, row_idx(b, i), mod_col)),
            pl.BlockSpec((None, rb, D_MODEL), lambda b, i, j: (b, row_idx(b, i), mod_col + 1)),
            pl.BlockSpec((1, D_MODEL), lambda b, i, j: (0, 0)),
            pl.BlockSpec((None, D_MODEL, tn), lambda b, i, j: (layer, 0, j)),
        ],
        out_specs=pl.BlockSpec((None, tm, tn), lambda b, i, j: (b, i, j)),
        out_shape=jax.ShapeDtypeStruct((G, R, n), F32),
        scratch_shapes=[pltpu.VMEM((tm, D_MODEL), BF16)],
        compiler_params=_cparams(("parallel", "parallel", "arbitrary")),
        name="norm_in_proj",
    )(x, mod, mod, gain, w)


def _norm_gu_kernel(x_ref, sh_ref, sc_ref, g_ref, wg_ref, wu_ref, o_ref, h_scr):
    @pl.when(pl.program_id(2) == 0)
    def _():
        _modulated_norm_into(h_scr, x_ref, sh_ref, sc_ref, g_ref)

    h = h_scr[...]
    gate = jnp.dot(h, wg_ref[...].astype(BF16), preferred_element_type=F32)
    up = jnp.dot(h, wu_ref[...].astype(BF16), preferred_element_type=F32)
    o_ref[...] = (_silu(gate) * up).astype(o_ref.dtype)


def _norm_gate_up(x, mod, mod_col, gain, wg, wu, layer, tm, tn):
    G, R, _ = x.shape
    r = mod.shape[1]
    rb = 1 if r == 1 else tm
    row_idx = (lambda b, i: 0) if r == 1 else (lambda b, i: i)
    return pl.pallas_call(
        _norm_gu_kernel,
        grid=(G, R // tm, D_FF // tn),
        in_specs=[
            pl.BlockSpec((None, tm, D_MODEL), lambda b, i, j: (b, i, 0)),
            pl.BlockSpec((None, rb, D_MODEL), lambda b, i, j: (b, row_idx(b, i), mod_col)),
            pl.BlockSpec((None, rb, D_MODEL), lambda b, i, j: (b, row_idx(b, i), mod_col + 1)),
            pl.BlockSpec((1, D_MODEL), lambda b, i, j: (0, 0)),
            pl.BlockSpec((None, D_MODEL, tn), lambda b, i, j: (layer, 0, j)),
            pl.BlockSpec((None, D_MODEL, tn), lambda b, i, j: (layer, 0, j)),
        ],
        out_specs=pl.BlockSpec((None, tm, tn), lambda b, i, j: (b, i, j)),
        out_shape=jax.ShapeDtypeStruct((G, R, D_FF), BF16),
        scratch_shapes=[pltpu.VMEM((tm, D_MODEL), BF16)],
        compiler_params=_cparams(("parallel", "parallel", "arbitrary")),
        name="norm_gate_up",
    )(x, mod, mod, gain, wg, wu)


def _mm_res_kernel(a_ref, w_ref, x_ref, gt_ref, o_ref):
    acc = jnp.dot(a_ref[...].astype(BF16), w_ref[...].astype(BF16), preferred_element_type=F32)
    o_ref[...] = x_ref[...] + gt_ref[...] * acc


def _matmul_residual(a, w, layer, x, mod, mod_off, tm, tn):
    G, R, K = a.shape
    r = mod.shape[1]
    rb = 1 if r == 1 else tm
    row_idx = (lambda b, i: 0) if r == 1 else (lambda b, i: i)
    gcol = mod_off // tn
    return pl.pallas_call(
        _mm_res_kernel,
        grid=(G, R // tm, D_MODEL // tn),
        in_specs=[
            pl.BlockSpec((None, tm, K), lambda b, i, j: (b, i, 0)),
            pl.BlockSpec((None, K, tn), lambda b, i, j: (layer, 0, j)),
            pl.BlockSpec((None, tm, tn), lambda b, i, j: (b, i, j)),
            pl.BlockSpec((None, rb, tn), lambda b, i, j: (b, row_idx(b, i), gcol + j)),
        ],
        out_specs=pl.BlockSpec((None, tm, tn), lambda b, i, j: (b, i, j)),
        out_shape=jax.ShapeDtypeStruct((G, R, D_MODEL), F32),
        compiler_params=_cparams(("parallel", "parallel", "arbitrary")),
        name="matmul_residual",
    )(a, w, x, mod)


def _split3_bf16(x):
    hi = x.astype(BF16)
    r1 = x - hi.astype(F32)
    mid = r1.astype(BF16)
    lo = (r1 - mid.astype(F32)).astype(BF16)
    return hi, mid, lo


HGRN_HP = 4
HGRN_TB = 1024
HGRN_SEP_MIN = 32


def _hgrn_prompt_kernel(q_ref, f_ref, i_ref, g_ref, lb_ref, gn_ref, o_ref, s_ref, st_scr):
    C = HGRN_CHUNK
    P = HGRN_SUB
    D = HEAD_DIM
    t_idx = pl.program_id(2)
    n_chunks = q_ref.shape[0] // C
    row = lax.broadcasted_iota(jnp.int32, (C, C), 0)
    col = lax.broadcasted_iota(jnp.int32, (C, C), 1)
    tri = (row >= col).astype(BF16)
    sel_r = lax.broadcasted_iota(jnp.int32, (P * D, C), 0)
    sel_c = lax.broadcasted_iota(jnp.int32, (P * D, C), 1)
    psh = P.bit_length() - 1
    sel = ((sel_c & (P - 1)) == (sel_r >> (D.bit_length() - 1))).astype(BF16)
    keep_diag = ((col >> psh) == (row >> psh)) & ((col & (P - 1)) <= (row & (P - 1)))
    g_norm = gn_ref[...]

    levels = []
    L = C // 2
    while L >= P:
        levels.append(L)
        L //= 2

    @pl.when(t_idx == 0)
    def _():
        st_scr[...] = jnp.zeros_like(st_scr)

    def chunk_body(c, carry):
        rs = pl.ds(pl.multiple_of(c * C, C), C)
        qs, ks, vs, logfs, parts = [], [], [], [], []
        for hh in range(HGRN_HP):
            cs = slice(hh * D, (hh + 1) * D)
            lb = lb_ref[:, cs]
            logf, fp = _log_forget(f_ref[rs, cs], lb)
            qs.append(_silu(q_ref[rs, cs]))
            ks.append((1.0 - lb) - fp)
            vs.append(i_ref[rs, cs].astype(BF16))
            parts.extend(_split3_bf16(logf * LOG2E))
        cum = jnp.dot(tri, jnp.concatenate(parts, axis=1), preferred_element_type=F32)

        for hh in range(HGRN_HP):
            cs = slice(hh * D, (hh + 1) * D)
            q, k, v16 = qs[hh], ks[hh], vs[hh]
            b = (cum[:, (3 * hh) * D:(3 * hh + 1) * D] + cum[:, (3 * hh + 1) * D:(3 * hh + 2) * D]
                 + cum[:, (3 * hh + 2) * D:(3 * hh + 3) * D])
            b_end = b[C - 1:C, :]
            st = st_scr[hh]
            o = lax.dot_general((q * jnp.exp2(b)).astype(BF16), st.astype(BF16),
                                (((1,), (1,)), ((), ())), preferred_element_type=F32)

            q_sep, k_sep = [], []
            a_masked = []
            for L in levels:
                q_rows, k_rows = [], []
                for p in range(C // (2 * L)):
                    lo, mid, hi = 2 * L * p, 2 * L * p + L, 2 * L * (p + 1)
                    ref = b[mid - 1:mid, :]
                    zf = jnp.zeros((L, D), F32)
                    kk = jnp.concatenate([k[lo:mid] * jnp.exp2(ref - b[lo:mid]), zf], axis=0)
                    qq = jnp.concatenate([zf, q[mid:hi] * jnp.exp2(b[mid:hi] - ref)], axis=0)
                    q_rows.append(qq.astype(BF16))
                    k_rows.append(kk.astype(BF16))
                if L >= HGRN_SEP_MIN:
                    for p in range(len(q_rows)):
                        pad = lambda x: jnp.concatenate(
                            [y for y in (jnp.zeros((2 * L * p, D), BF16), x,
                                         jnp.zeros((C - 2 * L * (p + 1), D), BF16)) if y.shape[0] > 0],
                            axis=0)
                        q_sep.append(pad(q_rows[p]))
                        k_sep.append(pad(k_rows[p]))
                else:
                    sh = L.bit_length() - 1
                    a_l = lax.dot_general(jnp.concatenate(q_rows, axis=0),
                                          jnp.concatenate(k_rows, axis=0),
                                          (((1,), (1,)), ((), ())), preferred_element_type=F32)
                    keep = (((row >> sh) & 1) == 1) & ((col >> sh) == (row >> sh) - 1)
                    a_masked.append((keep, a_l))
            a = lax.dot_general(jnp.concatenate(q_sep, axis=1), jnp.concatenate(k_sep, axis=1),
                                (((1,), (1,)), ((), ())), preferred_element_type=F32)
            for keep, a_l in a_masked:
                a = jnp.where(keep, a_l, a)

            b3 = b.reshape(C // P, P, D)
            k3 = k.reshape(C // P, P, D)
            zs = []
            for s in range(P):
                bs = jnp.broadcast_to(b3[:, s:s + 1, :], (C // P, P, D)).reshape(C, D)
                ksb = jnp.broadcast_to(k3[:, s:s + 1, :], (C // P, P, D)).reshape(C, D)
                e = jnp.exp2(jnp.minimum(b - bs, 0.0))
                zs.append((q * ksb * e).astype(BF16))
            a_d = jnp.dot(jnp.concatenate(zs, axis=1), sel, preferred_element_type=F32)
            a = jnp.where(keep_diag, a_d, a)

            o = o + jnp.dot(a.astype(BF16), v16, preferred_element_type=F32)

            k_end = (k * jnp.exp2(b_end - b)).astype(BF16)
            upd = lax.dot_general(v16, k_end, (((0,), (0,)), ((), ())), preferred_element_type=F32)
            st_scr[hh] = st * jnp.exp2(b_end) + upd

            o = _rms_rows(o, g_norm) * _silu(g_ref[rs, cs])
            o_ref[rs, cs] = o.astype(o_ref.dtype)
        return carry

    lax.fori_loop(0, n_chunks, chunk_body, 0)

    @pl.when(t_idx == pl.num_programs(2) - 1)
    def _():
        for hh in range(HGRN_HP):
            s_ref[hh] = st_scr[hh].T


def _hgrn_prompt(proj, lb, g_norm):
    B, T, _ = proj.shape
    W = HGRN_HP * HEAD_DIM
    seq = lambda off: pl.BlockSpec((None, HGRN_TB, W), lambda b, h, t: (b, t, off // HGRN_HP + h))
    return pl.pallas_call(
        _hgrn_prompt_kernel,
        grid=(B, N_LIN_HEADS // HGRN_HP, T // HGRN_TB),
        in_specs=[
            seq(COL_Q_L), seq(COL_F_L), seq(COL_I_L), seq(COL_G_L),
            pl.BlockSpec((1, W), lambda b, h, t: (0, h)),
            pl.BlockSpec((1, HEAD_DIM), lambda b, h, t: (0, 0)),
        ],
        out_specs=(
            pl.BlockSpec((None, HGRN_TB, W), lambda b, h, t: (b, t, h)),
            pl.BlockSpec((None, HGRN_HP, HEAD_DIM, HEAD_DIM), lambda b, h, t: (b, h, 0, 0)),
        ),
        out_shape=(
            jax.ShapeDtypeStruct((B, T, LIN_WIDTH), BF16),
            jax.ShapeDtypeStruct((B, N_LIN_HEADS, HEAD_DIM, HEAD_DIM), F32),
        ),
        scratch_shapes=[pltpu.VMEM((HGRN_HP, HEAD_DIM, HEAD_DIM), F32)],
        compiler_params=_cparams(("parallel", "parallel", "arbitrary")),
        name="hgrn_prompt",
    )(proj, proj, proj, proj, lb, g_norm)


def _attn_prompt_kernel(sink_ref, q_ref, kc_ref, vc_ref, kp_ref, vp_ref, qn_ref, kn_ref,
                        o_ref, nk_ref, nv_ref):
    n = pl.program_id(1)
    W = WINDOW
    qi = lax.broadcasted_iota(jnp.int32, (W, 2 * W), 0)
    kj = lax.broadcasted_iota(jnp.int32, (W, 2 * W), 1)
    dist = W + qi - kj
    valid = (dist >= 0) & (dist <= W) & ((n > 0) | (kj >= W))
    dist_m = jnp.where(valid, dist.astype(F32), jnp.inf)
    q_gain = qn_ref[...] * (ATT_SCALE * LOG2E)
    k_norm = kn_ref[...]

    new_k = []
    for h in range(N_KV_HEADS):
        hs = slice(h * HEAD_DIM, (h + 1) * HEAD_DIM)
        kc = _rms_rows(kc_ref[:, hs], k_norm)
        kp = _rms_rows(kp_ref[:, hs], k_norm)
        new_k.append(kc)
        k2 = jnp.concatenate([kp, kc], axis=0).astype(BF16)
        v2 = jnp.concatenate([vp_ref[:, hs], vc_ref[:, hs]], axis=0).astype(BF16)

        for g in range(GROUP):
            a = h * GROUP + g
            cs = slice(a * HEAD_DIM, (a + 1) * HEAD_DIM)
            qh = _rms_rows(q_ref[:, cs], q_gain).astype(BF16)
            s = lax.dot_general(qh, k2, (((1,), (1,)), ((), ())), preferred_element_type=F32)
            s = s + (-ALIBI_SLOPES[a] * LOG2E) * dist_m
            sink = sink_ref[a] * LOG2E
            m = jnp.maximum(jnp.max(s, axis=-1, keepdims=True), sink)
            p = jnp.exp2(s - m)
            den = jnp.sum(p, axis=-1, keepdims=True) + jnp.exp2(sink - m)
            o = jnp.dot(p.astype(BF16), v2, preferred_element_type=F32) * (1.0 / den)
            o_ref[:, cs] = o.astype(o_ref.dtype)

    @pl.when(n == pl.num_programs(1) - 1)
    def _():
        for h in range(N_KV_HEADS):
            hs = slice(h * HEAD_DIM, (h + 1) * HEAD_DIM)
            nk_ref[:, hs] = new_k[h]
            nv_ref[:, hs] = vc_ref[:, hs]


def _attn_prompt(proj, sinks, q_norm, k_norm):
    B, T, _ = proj.shape
    nb = T // WINDOW
    kcol = OFF_K_A // KV_WIDTH
    vcol = OFF_V_A // KV_WIDTH
    prev = lambda n: jnp.maximum(n - 1, 0)
    return pl.pallas_call(
        _attn_prompt_kernel,
        grid=(B, nb),
        in_specs=[
            pl.BlockSpec(memory_space=pltpu.SMEM),
            pl.BlockSpec((None, WINDOW, ATT_WIDTH), lambda b, n: (b, n, OFF_Q_A // ATT_WIDTH)),
            pl.BlockSpec((None, WINDOW, KV_WIDTH), lambda b, n: (b, n, kcol)),
            pl.BlockSpec((None, WINDOW, KV_WIDTH), lambda b, n: (b, n, vcol)),
            pl.BlockSpec((None, WINDOW, KV_WIDTH), lambda b, n: (b, prev(n), kcol)),
            pl.BlockSpec((None, WINDOW, KV_WIDTH), lambda b, n: (b, prev(n), vcol)),
            pl.BlockSpec((1, HEAD_DIM), lambda b, n: (0, 0)),
            pl.BlockSpec((1, HEAD_DIM), lambda b, n: (0, 0)),
        ],
        out_specs=(
            pl.BlockSpec((None, WINDOW, ATT_WIDTH), lambda b, n: (b, n, 0)),
            pl.BlockSpec((None, WINDOW, KV_WIDTH), lambda b, n: (b, 0, 0)),
            pl.BlockSpec((None, WINDOW, KV_WIDTH), lambda b, n: (b, 0, 0)),
        ),
        out_shape=(
            jax.ShapeDtypeStruct((B, T, ATT_WIDTH), BF16),
            jax.ShapeDtypeStruct((B, WINDOW, KV_WIDTH), F32),
            jax.ShapeDtypeStruct((B, WINDOW, KV_WIDTH), F32),
        ),
        compiler_params=_cparams(("parallel", "arbitrary")),
        name="attn_prompt",
    )(sinks, proj, proj, proj, proj, proj, q_norm, k_norm)


def _row_to_col(x_row, eye):
    return jnp.sum(jnp.where(eye, x_row, 0.0), axis=1, keepdims=True)


def _mix_sample_kernel(sink_ref, p_ref, s_ref, ck_ref, cv_ref, lb_ref,
                       gn_ref, qn_ref, kn_ref, o_ref, ns_ref, nk_ref, nv_ref):
    W = WINDOW
    er = lax.broadcasted_iota(jnp.int32, (HEAD_DIM, HEAD_DIM), 0)
    ec = lax.broadcasted_iota(jnp.int32, (HEAD_DIM, HEAD_DIM), 1)
    eye = er == ec
    g_norm = gn_ref[...]
    q_norm = qn_ref[...]
    k_norm = kn_ref[...]

    def cols(off, width=HEAD_DIM):
        return p_ref[:, off:off + width]

    for h in range(N_LIN_HEADS):
        c0 = h * HEAD_DIM
        q = _silu(cols(c0))
        lb = lb_ref[:, c0:c0 + HEAD_DIM]
        _, fp = _forget_gate(cols(LIN_WIDTH + c0), lb)
        f_col = _row_to_col(lb + fp, eye)
        k_col = 1.0 - f_col
        v = cols(2 * LIN_WIDTH + c0)
        s_new = s_ref[h] * f_col + k_col * v
        ns_ref[h] = s_new
        q8 = jnp.broadcast_to(q, (8, HEAD_DIM)).astype(BF16)
        o = jnp.dot(q8, s_new.astype(BF16), preferred_element_type=F32)[0:1, :]
        o = _rms_rows(o, g_norm) * _silu(cols(3 * LIN_WIDTH + c0))
        o_ref[:, c0:c0 + HEAD_DIM] = o

    row8 = lax.broadcasted_iota(jnp.int32, (8, 1), 0)
    row8_k = lax.broadcasted_iota(jnp.int32, (8, HEAD_DIM), 0)
    lane = lax.broadcasted_iota(jnp.int32, (8, W), 1)
    dist_c = (W - lane).astype(F32)
    rows_w = lax.broadcasted_iota(jnp.int32, (W, HEAD_DIM), 0)
    for h in range(N_KV_HEADS):
        hs = slice(h * HEAD_DIM, (h + 1) * HEAD_DIM)
        k_new = _rms_rows(cols(OFF_K_A + h * HEAD_DIM), k_norm)
        v_new = cols(OFF_V_A + h * HEAD_DIM)
        kc = ck_ref[:, hs]
        vc = cv_ref[:, hs]
        nk_ref[:, hs] = jnp.where(rows_w == W - 1, k_new, pltpu.roll(kc, W - 1, 0))
        nv_ref[:, hs] = jnp.where(rows_w == W - 1, v_new, pltpu.roll(vc, W - 1, 0))

        q4 = jnp.zeros((8, HEAD_DIM), F32)
        slope = jnp.zeros((8, 1), F32)
        sink = jnp.zeros((8, 1), F32)
        for g in range(GROUP):
            a = h * GROUP + g
            qg = _rms_rows(cols(OFF_Q_A + a * HEAD_DIM), q_norm)
            q4 = jnp.where(row8_k == g, qg, q4)
            slope = jnp.where(row8 == g, ALIBI_SLOPES[a], slope)
            sink = jnp.where(row8 == g, sink_ref[a], sink)
        q4b = q4.astype(BF16)
        s_c = lax.dot_general(q4b, kc.astype(BF16), (((1,), (1,)), ((), ())),
                              preferred_element_type=F32)
        s_c = s_c * ATT_SCALE - slope * dist_c
        s_n = jnp.sum(q4 * k_new, axis=-1, keepdims=True) * ATT_SCALE
        m = jnp.maximum(jnp.maximum(jnp.max(s_c, axis=-1, keepdims=True), s_n), sink)
        p_c = jnp.exp(s_c - m)
        p_n = jnp.exp(s_n - m)
        den = jnp.sum(p_c, axis=-1, keepdims=True) + p_n + jnp.exp(sink - m)
        o = jnp.dot(p_c.astype(BF16), vc.astype(BF16), preferred_element_type=F32)
        o = (o + p_n * v_new) / den
        for g in range(GROUP):
            a = h * GROUP + g
            o_ref[:, LIN_WIDTH + a * HEAD_DIM:LIN_WIDTH + (a + 1) * HEAD_DIM] = o[g:g + 1, :]


def _mix_sample(proj, state, cache_k, cache_v, layer, sinks, lb, g_norm, q_norm, k_norm):
    B = proj.shape[0]
    vec = pl.BlockSpec((1, HEAD_DIM), lambda b: (0, 0))
    lbs = pl.BlockSpec((1, LIN_WIDTH), lambda b: (0, 0))
    cache_in = pl.BlockSpec((None, None, WINDOW, KV_WIDTH), lambda b: (layer, b, 0, 0))
    st_in = pl.BlockSpec((None, None, N_LIN_HEADS, HEAD_DIM, HEAD_DIM),
                         lambda b: (layer, b, 0, 0, 0))
    cache = pl.BlockSpec((None, WINDOW, KV_WIDTH), lambda b: (b, 0, 0))
    st = pl.BlockSpec((None, N_LIN_HEADS, HEAD_DIM, HEAD_DIM), lambda b: (b, 0, 0, 0))
    return pl.pallas_call(
        _mix_sample_kernel,
        grid=(B,),
        in_specs=[
            pl.BlockSpec(memory_space=pltpu.SMEM),
            pl.BlockSpec((None, 1, IN_WIDTH), lambda b: (b, 0, 0)),
            st_in, cache_in, cache_in, lbs, vec, vec, vec,
        ],
        out_specs=(
            pl.BlockSpec((None, 1, MIX_WIDTH), lambda b: (b, 0, 0)),
            st, cache, cache,
        ),
        out_shape=(
            jax.ShapeDtypeStruct((B, 1, MIX_WIDTH), F32),
            jax.ShapeDtypeStruct(state.shape[1:], F32),
            jax.ShapeDtypeStruct((B, WINDOW, KV_WIDTH), F32),
            jax.ShapeDtypeStruct((B, WINDOW, KV_WIDTH), F32),
        ),
        compiler_params=_cparams(("parallel",)),
        name="mix_sample",
    )(sinks, proj, state, cache_k, cache_v, lb, g_norm, q_norm, k_norm)


def _out_proj_kernel(ol_ref, oa_ref, w1_ref, w2_ref, x_ref, gt_ref, o_ref):
    acc = jnp.dot(ol_ref[...], w1_ref[...].astype(BF16), preferred_element_type=F32)
    acc = acc + jnp.dot(oa_ref[...], w2_ref[...].astype(BF16), preferred_element_type=F32)
    o_ref[...] = x_ref[...] + gt_ref[...] * acc


def _out_proj_prompt(o_l, o_a, w, layer, x, mod, mod_off, tm, tn):
    G, R, _ = x.shape
    gcol = mod_off // tn
    return pl.pallas_call(
        _out_proj_kernel,
        grid=(G, R // tm, D_MODEL // tn),
        in_specs=[
            pl.BlockSpec((None, tm, LIN_WIDTH), lambda b, i, j: (b, i, 0)),
            pl.BlockSpec((None, tm, ATT_WIDTH), lambda b, i, j: (b, i, 0)),
            pl.BlockSpec((None, LIN_WIDTH, tn), lambda b, i, j: (layer, 0, j)),
            pl.BlockSpec((None, ATT_WIDTH, tn), lambda b, i, j: (layer, 1, j)),
            pl.BlockSpec((None, tm, tn), lambda b, i, j: (b, i, j)),
            pl.BlockSpec((None, 1, tn), lambda b, i, j: (b, 0, gcol + j)),
        ],
        out_specs=pl.BlockSpec((None, tm, tn), lambda b, i, j: (b, i, j)),
        out_shape=jax.ShapeDtypeStruct((G, R, D_MODEL), F32),
        compiler_params=_cparams(("parallel", "parallel", "arbitrary")),
        name="out_proj",
    )(o_l, o_a, w, w, x, mod)


P_TM = 1024
P_TN_IN = 512
P_TN_OUT = 512
P_TN_FF = 512
P_TN_DOWN = 256
S_TN = 512


def kernel(x_prompt, x_sample, cache_k, cache_v, state_hgrn, c_prompt, c_sample, lower_bounds,
           w_ada, b_ada, norm_mix, w_in, q_norm, k_norm, attn_sinks, g_norm, w_out, norm_ffn,
           w_gate, w_up, w_down):
    B, T, _ = x_prompt.shape
    BS = x_sample.shape[0]

    lb_all = _lower_bounds(lower_bounds)

    c_all = jnp.concatenate(
        [c_prompt, c_sample, jnp.zeros((MOD_ROWS - B - BS, D_MODEL), F32)], axis=0)
    mod = _modulation(c_all, w_ada, b_ada)

    w_in_b, w_out_b, w_gate_b, w_up_b, w_down_b = w_in, w_out, w_gate, w_up, w_down

    cache_k2 = cache_k.reshape(DEPTH, BS, WINDOW, KV_WIDTH)
    cache_v2 = cache_v.reshape(DEPTH, BS, WINDOW, KV_WIDTH)

    yp = x_prompt
    ys = x_sample.reshape(1, BS, D_MODEL)
    pk, pv, ps, sk, sv, ss = [], [], [], [], [], []
    for l in range(DEPTH):
        mod_p = mod[l, :B].reshape(B, 1, 6 * D_MODEL)
        mod_s = mod[l, B:B + BS].reshape(1, BS, 6 * D_MODEL)
        gain_mix = norm_mix[l].reshape(1, D_MODEL)
        gain_ffn = norm_ffn[l].reshape(1, D_MODEL)
        lb = lb_all[l].reshape(1, LIN_WIDTH)
        gn = g_norm[l].reshape(1, HEAD_DIM)
        qn = q_norm[l].reshape(1, HEAD_DIM)
        kn = k_norm[l].reshape(1, HEAD_DIM)
        sinks = attn_sinks[l].astype(F32)

        proj = _norm_matmul(yp, mod_p, 0, gain_mix, w_in_b, l, P_TM, P_TN_IN)
        o_l, s_p = _hgrn_prompt(proj, lb, gn)
        o_a, k_p, v_p = _attn_prompt(proj, sinks, qn, kn)
        yp = _out_proj_prompt(o_l, o_a, w_out_b, l, yp, mod_p, 2 * D_MODEL, P_TM, P_TN_OUT)
        ff = _norm_gate_up(yp, mod_p, 3, gain_ffn, w_gate_b, w_up_b, l, P_TM, P_TN_FF)
        yp = _matmul_residual(ff, w_down_b, l, yp, mod_p, 5 * D_MODEL, P_TM, P_TN_DOWN)
        pk.append(k_p.reshape(B, WINDOW, N_KV_HEADS, HEAD_DIM))
        pv.append(v_p.reshape(B, WINDOW, N_KV_HEADS, HEAD_DIM))
        ps.append(s_p)

        proj_s = _norm_matmul(ys, mod_s, 0, gain_mix, w_in_b, l, BS, S_TN)
        mixed_s, s_s, k_s, v_s = _mix_sample(
            proj_s.reshape(BS, 1, IN_WIDTH), state_hgrn, cache_k2, cache_v2, l,
            sinks, lb, gn, qn, kn)
        ys = _matmul_residual(mixed_s.reshape(1, BS, MIX_WIDTH), w_out_b, l, ys, mod_s,
                              2 * D_MODEL, BS, S_TN)
        ff_s = _norm_gate_up(ys, mod_s, 3, gain_ffn, w_gate_b, w_up_b, l, BS, S_TN)
        ys = _matmul_residual(ff_s, w_down_b, l, ys, mod_s, 5 * D_MODEL, BS, S_TN)
        sk.append(k_s.reshape(BS, WINDOW, N_KV_HEADS, HEAD_DIM))
        sv.append(v_s.reshape(BS, WINDOW, N_KV_HEADS, HEAD_DIM))
        ss.append(s_s)

    return (yp, ys.reshape(BS, 1, D_MODEL), jnp.stack(pk), jnp.stack(pv), jnp.stack(ps),
            jnp.stack(sk), jnp.stack(sv), jnp.stack(ss))
```

```python
import functools
import math

import jax
import jax.numpy as jnp
from jax import lax
from jax.experimental import pallas as pl
from jax.experimental.pallas import tpu as pltpu

F32 = jnp.float32
BF16 = jnp.bfloat16

D_MODEL = 2048
DEPTH = 4
HEAD_DIM = 128
N_LIN_HEADS = 8
LIN_WIDTH = N_LIN_HEADS * HEAD_DIM
N_ATT_HEADS = 8
N_KV_HEADS = 2
GROUP = N_ATT_HEADS // N_KV_HEADS
ATT_WIDTH = N_ATT_HEADS * HEAD_DIM
KV_WIDTH = N_KV_HEADS * HEAD_DIM
MIX_WIDTH = LIN_WIDTH + ATT_WIDTH
WINDOW = 128
D_FF = 5632
IN_WIDTH = 4 * LIN_WIDTH + ATT_WIDTH + 2 * KV_WIDTH
EPS = 1e-6
ATT_SCALE = HEAD_DIM ** -0.5
ALIBI_SLOPES = tuple(2.0 ** (-8.0 * (a + 1) / N_ATT_HEADS) for a in range(N_ATT_HEADS))

COL_Q_L = 0
COL_F_L = LIN_WIDTH // HEAD_DIM
COL_I_L = 2 * LIN_WIDTH // HEAD_DIM
COL_G_L = 3 * LIN_WIDTH // HEAD_DIM
OFF_Q_A = 4 * LIN_WIDTH
OFF_K_A = OFF_Q_A + ATT_WIDTH
OFF_V_A = OFF_K_A + KV_WIDTH

MOD_ROWS = 40
HGRN_CHUNK = 128
HGRN_SUB = 8
LOG2E = math.log2(math.e)
BF16_ROWS = 16

VMEM_LIMIT = 56 * 1024 * 1024


def _cparams(sem):
    return pltpu.CompilerParams(dimension_semantics=sem, vmem_limit_bytes=VMEM_LIMIT)


def _silu(x):
    return x / (1.0 + jnp.exp(-x))


def _rms_rows(x, g):
    ms = jnp.mean(x * x, axis=-1, keepdims=True)
    return x * lax.rsqrt(ms + EPS) * g


def _forget_gate(f_raw, lb):
    t = jnp.exp(-jnp.abs(f_raw))
    inv = 1.0 / (1.0 + t)
    sig = jnp.where(f_raw >= 0.0, inv, t * inv)
    return t, (1.0 - lb) * sig


def _log_forget(f_raw, lb):
    t, fp = _forget_gate(f_raw, lb)
    log_sig = jnp.minimum(f_raw, 0.0) - jnp.log(1.0 + t)
    return jnp.where(lb > 0.0, jnp.log(lb + fp), log_sig), fp


def _lb_kernel(lb_ref, o_ref):
    x = lb_ref[...]
    m = jnp.max(x, axis=0, keepdims=True)
    e = jnp.exp(x - m)
    p = e / jnp.sum(e, axis=0, keepdims=True)
    acc = jnp.zeros_like(p[0:1])
    for l in range(DEPTH):
        if l > 0:
            acc = acc + p[l:l + 1]
        o_ref[l:l + 1, :] = acc


def _lower_bounds(lower_bounds):
    return pl.pallas_call(
        _lb_kernel,
        out_shape=jax.ShapeDtypeStruct((DEPTH, LIN_WIDTH), F32),
        name="lower_bounds",
    )(lower_bounds.astype(F32))


ADA_TN = 1024


def _ada_kernel(c_ref, w_ref, b_ref, o_ref):
    s = _silu(c_ref[...]).astype(BF16)
    acc = jnp.dot(s, w_ref[...].astype(BF16), preferred_element_type=F32)
    o_ref[...] = acc + b_ref[...]


def _modulation(c_all, w_ada, b_ada):
    n = 6 * D_MODEL
    return pl.pallas_call(
        _ada_kernel,
        grid=(DEPTH, n // ADA_TN),
        in_specs=[
            pl.BlockSpec((MOD_ROWS, D_MODEL), lambda l, j: (0, 0)),
            pl.BlockSpec((None, D_MODEL, ADA_TN), lambda l, j: (l, 0, j)),
            pl.BlockSpec((None, 1, ADA_TN), lambda l, j: (l, 0, j)),
        ],
        out_specs=pl.BlockSpec((None, MOD_ROWS, ADA_TN), lambda l, j: (l, 0, j)),
        out_shape=jax.ShapeDtypeStruct((DEPTH, MOD_ROWS, n), F32),
        compiler_params=_cparams(("parallel", "parallel")),
        name="modulation",
    )(c_all, w_ada, b_ada.reshape(DEPTH, 1, n))


NORM_ROWS = 16
NORM_UNROLL = 4


def _modulated_norm_into(h_scr, x_ref, sh_ref, sc_ref, g_ref):
    rows = x_ref.shape[0]
    ch = min(rows, NORM_ROWS)
    per_row = sc_ref.shape[0] != 1
    gain = g_ref[...]
    if not per_row:
        gain = gain * (1.0 + sc_ref[...])
        shift = sh_ref[...]

    def body(c, carry):
        rs = pl.ds(pl.multiple_of(c * ch, ch), ch)
        x = x_ref[rs, :]
        inv = lax.rsqrt(jnp.mean(x * x, axis=-1, keepdims=True) + EPS)
        if per_row:
            h = (x * inv * gain) * (1.0 + sc_ref[rs, :]) + sh_ref[rs, :]
        else:
            h = (x * inv) * gain + shift
        h_scr[rs, :] = h.astype(BF16)
        return carry

    lax.fori_loop(0, rows // ch, body, 0, unroll=min(NORM_UNROLL, rows // ch))


def _w_spec(w, layer, tn):
    if layer is None:
        assert w.shape[2] == tn
        return pl.BlockSpec((None, w.shape[1], tn), lambda b, i, j: (j, 0, 0))
    return pl.BlockSpec((None, w.shape[1], tn), lambda b, i, j: (layer, 0, j))


def _wb_out(w, tn):
    rows, n = w.shape[-2:]
    return (pl.BlockSpec((None, rows, tn), lambda b, i, j: (j, 0, 0)),
            jax.ShapeDtypeStruct((n // tn, rows, tn), BF16))


def _norm_mm_kernel(x_ref, sh_ref, sc_ref, g_ref, w_ref, o_ref, *rest, emit):
    h_scr = rest[-1]

    @pl.when(pl.program_id(2) == 0)
    def _():
        _modulated_norm_into(h_scr, x_ref, sh_ref, sc_ref, g_ref)

    w = w_ref[...].astype(BF16)
    if emit:
        rest[0][...] = w
    o_ref[...] = jnp.dot(h_scr[...], w, preferred_element_type=F32).astype(o_ref.dtype)


def _mod_specs(mod, mod_col, tm):
    if mod.shape[1] == 1:
        return [pl.BlockSpec((None, 1, D_MODEL), lambda b, i, j, c=mod_col + k: (b, 0, c))
                for k in range(2)]
    return [pl.BlockSpec((None, tm, D_MODEL), lambda b, i, j, c=mod_col + k: (b, i, c))
            for k in range(2)]


def _norm_matmul(x, mod, mod_col, gain, w, layer, tm, tn):
    G, R, _ = x.shape
    emit = layer is not None
    nt = w.shape[-1] // tn if emit else w.shape[0]
    if emit:
        assert G * (R // tm) == 1
        wb_spec, wb_shape = _wb_out(w, tn)
        out_specs = [pl.BlockSpec((None, tm, tn), lambda b, i, j: (b, i, j)), wb_spec]
        out_shape = [jax.ShapeDtypeStruct((G, R, nt * tn), F32), wb_shape]
    else:
        out_specs = pl.BlockSpec((None, None, tm, tn), lambda b, i, j: (j, b, i, 0))
        out_shape = jax.ShapeDtypeStruct((nt, G, R, tn), F32)
    return pl.pallas_call(
        functools.partial(_norm_mm_kernel, emit=emit),
        grid=(G, R // tm, nt),
        in_specs=[pl.BlockSpec((None, tm, D_MODEL), lambda b, i, j: (b, i, 0))]
        + _mod_specs(mod, mod_col, tm)
        + [pl.BlockSpec((1, D_MODEL), lambda b, i, j: (0, 0)), _w_spec(w, layer, tn)],
        out_specs=out_specs,
        out_shape=out_shape,
        scratch_shapes=[pltpu.VMEM((tm, D_MODEL), BF16)],
        compiler_params=_cparams(("parallel", "parallel", "arbitrary")),
        name="norm_in_proj",
    )(x, mod, mod, gain, w)


def _norm_gu_kernel(x_ref, sh_ref, sc_ref, g_ref, wg_ref, wu_ref, o_ref, *rest, emit):
    h_scr = rest[-1]

    @pl.when(pl.program_id(2) == 0)
    def _():
        _modulated_norm_into(h_scr, x_ref, sh_ref, sc_ref, g_ref)

    h = h_scr[...]
    wg = wg_ref[...].astype(BF16)
    wu = wu_ref[...].astype(BF16)
    if emit:
        rest[0][...] = wg
        rest[1][...] = wu
    gate = jnp.dot(h, wg, preferred_element_type=F32)
    up = jnp.dot(h, wu, preferred_element_type=F32)
    o_ref[...] = (_silu(gate) * up).astype(o_ref.dtype)


def _norm_gate_up(x, mod, mod_col, gain, wg, wu, layer, tm, tn):
    G, R, _ = x.shape
    emit = layer is not None
    nt = wg.shape[-1] // tn if emit else wg.shape[0]
    if emit:
        assert G * (R // tm) == 1
        wb_spec, wb_shape = _wb_out(wg, tn)
        out_specs = [pl.BlockSpec((None, tm, tn), lambda b, i, j: (b, i, j)), wb_spec, wb_spec]
        out_shape = [jax.ShapeDtypeStruct((G, R, nt * tn), BF16), wb_shape, wb_shape]
    else:
        out_specs = pl.BlockSpec((None, None, tm, tn), lambda b, i, j: (j, b, i, 0))
        out_shape = jax.ShapeDtypeStruct((nt, G, R, tn), BF16)
    return pl.pallas_call(
        functools.partial(_norm_gu_kernel, emit=emit),
        grid=(G, R // tm, nt),
        in_specs=[pl.BlockSpec((None, tm, D_MODEL), lambda b, i, j: (b, i, 0))]
        + _mod_specs(mod, mod_col, tm)
        + [pl.BlockSpec((1, D_MODEL), lambda b, i, j: (0, 0)),
           _w_spec(wg, layer, tn), _w_spec(wu, layer, tn)],
        out_specs=out_specs,
        out_shape=out_shape,
        scratch_shapes=[pltpu.VMEM((tm, D_MODEL), BF16)],
        compiler_params=_cparams(("parallel", "parallel", "arbitrary")),
        name="norm_gate_up",
    )(x, mod, mod, gain, wg, wu)


def _res_matmul_kernel(*refs, n_a):
    a_refs, w_refs = refs[:n_a], refs[n_a:2 * n_a]
    x_ref, gt_ref, o_ref = refs[2 * n_a:]
    tn = w_refs[0].shape[-1]
    acc = None
    for a_ref, w_ref in zip(a_refs, w_refs):
        slabs = [a_ref[s] for s in range(a_ref.shape[0])] if len(a_ref.shape) == 3 else [a_ref[...]]
        ts = slabs[0].shape[1]
        for s, a in enumerate(slabs):
            part = jnp.dot(a, w_ref[s * ts:(s + 1) * ts, :], preferred_element_type=F32)
            acc = part if acc is None else acc + part
    cs = pl.ds(pl.multiple_of(pl.program_id(2) * tn, tn), tn)
    o_ref[:, cs] = x_ref[:, cs] + gt_ref[:, cs] * acc


def _res_matmul_prompt(a_list, w_list, x, mod, gate_col, tm, name):
    G, R, _ = x.shape
    a_specs, w_specs, ws = [], [], []
    for a, w in zip(a_list, w_list):
        w, row_blk = w if isinstance(w, tuple) else (w, 0)
        if a.ndim == 4:
            S, _, _, ts = a.shape
            a_specs.append(pl.BlockSpec((S, None, tm, ts), lambda b, i, j: (0, b, i, 0)))
            k = S * ts
        else:
            k = a.shape[2]
            a_specs.append(pl.BlockSpec((None, tm, k), lambda b, i, j: (b, i, 0)))
        nt, _, tn = w.shape
        w_specs.append(pl.BlockSpec((None, k, tn), lambda b, i, j, r=row_blk: (j, r, 0)))
        ws.append(w)
    full = pl.BlockSpec((None, tm, D_MODEL), lambda b, i, j: (b, i, 0))
    return pl.pallas_call(
        functools.partial(_res_matmul_kernel, n_a=len(a_list)),
        grid=(G, R // tm, nt),
        in_specs=a_specs + w_specs
        + [full, pl.BlockSpec((None, 1, D_MODEL), lambda b, i, j: (b, 0, gate_col))],
        out_specs=full,
        out_shape=jax.ShapeDtypeStruct((G, R, D_MODEL), F32),
        compiler_params=_cparams(("parallel", "parallel", "arbitrary")),
        name=name,
    )(*a_list, *ws, x, mod)


def _mm_res_kernel(a_ref, w_ref, x_ref, gt_ref, o_ref, wb_ref):
    w = w_ref[...].astype(BF16)
    wb_ref[...] = w
    acc = jnp.dot(a_ref[...].astype(BF16), w, preferred_element_type=F32)
    o_ref[...] = x_ref[...] + gt_ref[...] * acc


def _matmul_residual(a, w, layer, x, mod, mod_off, tm, tn):
    G, R, K = a.shape
    assert G * (R // tm) == 1 and mod.shape[1] == R
    gcol = mod_off // tn
    wb_spec, wb_shape = _wb_out(w, tn)
    return pl.pallas_call(
        _mm_res_kernel,
        grid=(G, R // tm, D_MODEL // tn),
        in_specs=[
            pl.BlockSpec((None, tm, K), lambda b, i, j: (b, i, 0)),
            _w_spec(w, layer, tn),
            pl.BlockSpec((None, tm, tn), lambda b, i, j: (b, i, j)),
            pl.BlockSpec((None, tm, tn), lambda b, i, j: (b, i, gcol + j)),
        ],
        out_specs=[pl.BlockSpec((None, tm, tn), lambda b, i, j: (b, i, j)), wb_spec],
        out_shape=[jax.ShapeDtypeStruct((G, R, D_MODEL), F32), wb_shape],
        compiler_params=_cparams(("parallel", "parallel", "arbitrary")),
        name="matmul_residual",
    )(a, w, x, mod)


def _split3_bf16(x):
    hi = x.astype(BF16)
    r1 = x - hi.astype(F32)
    mid = r1.astype(BF16)
    lo = (r1 - mid.astype(F32)).astype(BF16)
    return hi, mid, lo


HGRN_HP = 4
HGRN_TB = 1024
HGRN_SEP_MIN = 32


def _hgrn_prompt_kernel(q_ref, f_ref, i_ref, g_ref, lb_ref, gn_ref, o_ref, s_ref, st_scr):
    C = HGRN_CHUNK
    P = HGRN_SUB
    D = HEAD_DIM
    t_idx = pl.program_id(2)
    n_chunks = q_ref.shape[0] // C
    row = lax.broadcasted_iota(jnp.int32, (C, C), 0)
    col = lax.broadcasted_iota(jnp.int32, (C, C), 1)
    tri = (row >= col).astype(BF16)
    sel_r = lax.broadcasted_iota(jnp.int32, (P * D, C), 0)
    sel_c = lax.broadcasted_iota(jnp.int32, (P * D, C), 1)
    psh = P.bit_length() - 1
    sel = ((sel_c & (P - 1)) == (sel_r >> (D.bit_length() - 1))).astype(BF16)
    keep_diag = ((col >> psh) == (row >> psh)) & ((col & (P - 1)) <= (row & (P - 1)))
    g_norm = gn_ref[...]

    levels = []
    L = C // 2
    while L >= P:
        levels.append(L)
        L //= 2

    @pl.when(t_idx == 0)
    def _():
        st_scr[...] = jnp.zeros_like(st_scr)

    def chunk_body(c, carry):
        rs = pl.ds(pl.multiple_of(c * C, C), C)
        qs, ks, vs, logfs, parts = [], [], [], [], []
        for hh in range(HGRN_HP):
            cs = slice(hh * D, (hh + 1) * D)
            lb = lb_ref[:, cs]
            logf, fp = _log_forget(f_ref[rs, cs], lb)
            qs.append(_silu(q_ref[rs, cs]))
            ks.append((1.0 - lb) - fp)
            vs.append(i_ref[rs, cs].astype(BF16))
            parts.extend(_split3_bf16(logf * LOG2E))
        cum = jnp.dot(tri, jnp.concatenate(parts, axis=1), preferred_element_type=F32)

        for hh in range(HGRN_HP):
            cs = slice(hh * D, (hh + 1) * D)
            q, k, v16 = qs[hh], ks[hh], vs[hh]
            b = (cum[:, (3 * hh) * D:(3 * hh + 1) * D] + cum[:, (3 * hh + 1) * D:(3 * hh + 2) * D]
                 + cum[:, (3 * hh + 2) * D:(3 * hh + 3) * D])
            b_end = b[C - 1:C, :]
            st = st_scr[hh]
            o = lax.dot_general((q * jnp.exp2(b)).astype(BF16), st.astype(BF16),
                                (((1,), (1,)), ((), ())), preferred_element_type=F32)

            q_sep, k_sep = [], []
            a_masked = []
            for L in levels:
                q_rows, k_rows = [], []
                for p in range(C // (2 * L)):
                    lo, mid, hi = 2 * L * p, 2 * L * p + L, 2 * L * (p + 1)
                    ref = b[mid - 1:mid, :]
                    zf = jnp.zeros((L, D), F32)
                    kk = jnp.concatenate([k[lo:mid] * jnp.exp2(ref - b[lo:mid]), zf], axis=0)
                    qq = jnp.concatenate([zf, q[mid:hi] * jnp.exp2(b[mid:hi] - ref)], axis=0)
                    q_rows.append(qq.astype(BF16))
                    k_rows.append(kk.astype(BF16))
                if L >= HGRN_SEP_MIN:
                    for p in range(len(q_rows)):
                        pad = lambda x: jnp.concatenate(
                            [y for y in (jnp.zeros((2 * L * p, D), BF16), x,
                                         jnp.zeros((C - 2 * L * (p + 1), D), BF16)) if y.shape[0] > 0],
                            axis=0)
                        q_sep.append(pad(q_rows[p]))
                        k_sep.append(pad(k_rows[p]))
                else:
                    sh = L.bit_length() - 1
                    a_l = lax.dot_general(jnp.concatenate(q_rows, axis=0),
                                          jnp.concatenate(k_rows, axis=0),
                                          (((1,), (1,)), ((), ())), preferred_element_type=F32)
                    keep = (((row >> sh) & 1) == 1) & ((col >> sh) == (row >> sh) - 1)
                    a_masked.append((keep, a_l))
            a = lax.dot_general(jnp.concatenate(q_sep, axis=1), jnp.concatenate(k_sep, axis=1),
                                (((1,), (1,)), ((), ())), preferred_element_type=F32)
            for keep, a_l in a_masked:
                a = jnp.where(keep, a_l, a)

            b3 = b.reshape(C // P, P, D)
            k3 = k.reshape(C // P, P, D)
            zs = []
            for s in range(P):
                bs = jnp.broadcast_to(b3[:, s:s + 1, :], (C // P, P, D)).reshape(C, D)
                ksb = jnp.broadcast_to(k3[:, s:s + 1, :], (C // P, P, D)).reshape(C, D)
                e = jnp.exp2(jnp.minimum(b - bs, 0.0))
                zs.append((q * ksb * e).astype(BF16))
            a_d = jnp.dot(jnp.concatenate(zs, axis=1), sel, preferred_element_type=F32)
            a = jnp.where(keep_diag, a_d, a)

            o = o + jnp.dot(a.astype(BF16), v16, preferred_element_type=F32)

            k_end = (k * jnp.exp2(b_end - b)).astype(BF16)
            upd = lax.dot_general(v16, k_end, (((0,), (0,)), ((), ())), preferred_element_type=F32)
            st_scr[hh] = st * jnp.exp2(b_end) + upd

            o = _rms_rows(o, g_norm) * _silu(g_ref[rs, cs])
            o_ref[rs, cs] = o.astype(o_ref.dtype)
        return carry

    lax.fori_loop(0, n_chunks, chunk_body, 0)

    @pl.when(t_idx == pl.num_programs(2) - 1)
    def _():
        for hh in range(HGRN_HP):
            s_ref[hh] = st_scr[hh].T


def _hgrn_prompt(proj, lb, g_norm):
    _, B, T, W = proj.shape
    assert W == HGRN_HP * HEAD_DIM
    seq = lambda off: pl.BlockSpec((None, None, HGRN_TB, W),
                                   lambda b, h, t: (off // HGRN_HP + h, b, t, 0))
    return pl.pallas_call(
        _hgrn_prompt_kernel,
        grid=(B, N_LIN_HEADS // HGRN_HP, T // HGRN_TB),
        in_specs=[
            seq(COL_Q_L), seq(COL_F_L), seq(COL_I_L), seq(COL_G_L),
            pl.BlockSpec((1, W), lambda b, h, t: (0, h)),
            pl.BlockSpec((1, HEAD_DIM), lambda b, h, t: (0, 0)),
        ],
        out_specs=(
            pl.BlockSpec((None, HGRN_TB, W), lambda b, h, t: (b, t, h)),
            pl.BlockSpec((None, HGRN_HP, HEAD_DIM, HEAD_DIM), lambda b, h, t: (b, h, 0, 0)),
        ),
        out_shape=(
            jax.ShapeDtypeStruct((B, T, LIN_WIDTH), BF16),
            jax.ShapeDtypeStruct((B, N_LIN_HEADS, HEAD_DIM, HEAD_DIM), F32),
        ),
        scratch_shapes=[pltpu.VMEM((HGRN_HP, HEAD_DIM, HEAD_DIM), F32)],
        compiler_params=_cparams(("parallel", "parallel", "arbitrary")),
        name="hgrn_prompt",
    )(proj, proj, proj, proj, lb, g_norm)


def _attn_prompt_kernel(sink_ref, q_ref, kvc_ref, kvp_ref, qn_ref, kn_ref,
                        o_ref, nk_ref, nv_ref):
    n = pl.program_id(1)
    W = WINDOW
    qi = lax.broadcasted_iota(jnp.int32, (W, 2 * W), 0)
    kj = lax.broadcasted_iota(jnp.int32, (W, 2 * W), 1)
    dist = W + qi - kj
    valid = (dist >= 0) & (dist <= W) & ((n > 0) | (kj >= W))
    dist_m = jnp.where(valid, dist.astype(F32), jnp.inf)
    q_gain = qn_ref[...] * (ATT_SCALE * LOG2E)
    k_norm = kn_ref[...]

    new_k = []
    for h in range(N_KV_HEADS):
        hs = slice(h * HEAD_DIM, (h + 1) * HEAD_DIM)
        vs = slice(KV_WIDTH + h * HEAD_DIM, KV_WIDTH + (h + 1) * HEAD_DIM)
        kc = _rms_rows(kvc_ref[:, hs], k_norm)
        kp = _rms_rows(kvp_ref[:, hs], k_norm)
        new_k.append(kc)
        k2 = jnp.concatenate([kp, kc], axis=0).astype(BF16)
        v2 = jnp.concatenate([kvp_ref[:, vs], kvc_ref[:, vs]], axis=0).astype(BF16)

        for g in range(GROUP):
            a = h * GROUP + g
            cs = slice(a * HEAD_DIM, (a + 1) * HEAD_DIM)
            qh = _rms_rows(q_ref[h, :, g * HEAD_DIM:(g + 1) * HEAD_DIM], q_gain).astype(BF16)
            s = lax.dot_general(qh, k2, (((1,), (1,)), ((), ())), preferred_element_type=F32)
            s = s + (-ALIBI_SLOPES[a] * LOG2E) * dist_m
            sink = sink_ref[a] * LOG2E
            m = jnp.maximum(jnp.max(s, axis=-1, keepdims=True), sink)
            p = jnp.exp2(s - m)
            den = jnp.sum(p, axis=-1, keepdims=True) + jnp.exp2(sink - m)
            o = jnp.dot(p.astype(BF16), v2, preferred_element_type=F32) * (1.0 / den)
            o_ref[:, cs] = o.astype(o_ref.dtype)

    @pl.when(n == pl.num_programs(1) - 1)
    def _():
        for h in range(N_KV_HEADS):
            hs = slice(h * HEAD_DIM, (h + 1) * HEAD_DIM)
            nk_ref[:, hs] = new_k[h]
            nv_ref[:, hs] = kvc_ref[:, KV_WIDTH + h * HEAD_DIM:KV_WIDTH + (h + 1) * HEAD_DIM]


def _attn_prompt(proj, sinks, q_norm, k_norm):
    _, B, T, W = proj.shape
    assert W == GROUP * HEAD_DIM == 2 * KV_WIDTH
    nb = T // WINDOW
    q_slab = OFF_Q_A // W // N_KV_HEADS
    kv_slab = OFF_K_A // W
    prev = lambda n: jnp.maximum(n - 1, 0)
    return pl.pallas_call(
        _attn_prompt_kernel,
        grid=(B, nb),
        in_specs=[
            pl.BlockSpec(memory_space=pltpu.SMEM),
            pl.BlockSpec((N_KV_HEADS, None, WINDOW, W), lambda b, n: (q_slab, b, n, 0)),
            pl.BlockSpec((None, None, WINDOW, W), lambda b, n: (kv_slab, b, n, 0)),
            pl.BlockSpec((None, None, WINDOW, W), lambda b, n: (kv_slab, b, prev(n), 0)),
            pl.BlockSpec((1, HEAD_DIM), lambda b, n: (0, 0)),
            pl.BlockSpec((1, HEAD_DIM), lambda b, n: (0, 0)),
        ],
        out_specs=(
            pl.BlockSpec((None, WINDOW, ATT_WIDTH), lambda b, n: (b, n, 0)),
            pl.BlockSpec((None, WINDOW, KV_WIDTH), lambda b, n: (b, 0, 0)),
            pl.BlockSpec((None, WINDOW, KV_WIDTH), lambda b, n: (b, 0, 0)),
        ),
        out_shape=(
            jax.ShapeDtypeStruct((B, T, ATT_WIDTH), BF16),
            jax.ShapeDtypeStruct((B, WINDOW, KV_WIDTH), F32),
            jax.ShapeDtypeStruct((B, WINDOW, KV_WIDTH), F32),
        ),
        compiler_params=_cparams(("parallel", "arbitrary")),
        name="attn_prompt",
    )(sinks, proj, proj, proj, q_norm, k_norm)


def _row_to_col(x_row, eye):
    return jnp.sum(jnp.where(eye, x_row, 0.0), axis=1, keepdims=True)


def _mix_sample_kernel(sink_ref, p_ref, s_ref, ck_ref, cv_ref, lb_ref,
                       gn_ref, qn_ref, kn_ref, o_ref, ns_ref, nk_ref, nv_ref):
    W = WINDOW
    er = lax.broadcasted_iota(jnp.int32, (HEAD_DIM, HEAD_DIM), 0)
    ec = lax.broadcasted_iota(jnp.int32, (HEAD_DIM, HEAD_DIM), 1)
    eye = er == ec
    g_norm = gn_ref[...]
    q_norm = qn_ref[...]
    k_norm = kn_ref[...]

    def cols(off, width=HEAD_DIM):
        return p_ref[:, off:off + width]

    for h in range(N_LIN_HEADS):
        c0 = h * HEAD_DIM
        q = _silu(cols(c0))
        lb = lb_ref[:, c0:c0 + HEAD_DIM]
        _, fp = _forget_gate(cols(LIN_WIDTH + c0), lb)
        f_col = _row_to_col(lb + fp, eye)
        k_col = 1.0 - f_col
        v = cols(2 * LIN_WIDTH + c0)
        s_new = s_ref[h] * f_col + k_col * v
        ns_ref[h] = s_new
        q8 = jnp.broadcast_to(q, (8, HEAD_DIM)).astype(BF16)
        o = jnp.dot(q8, s_new.astype(BF16), preferred_element_type=F32)[0:1, :]
        o = _rms_rows(o, g_norm) * _silu(cols(3 * LIN_WIDTH + c0))
        o_ref[:, c0:c0 + HEAD_DIM] = o

    row8 = lax.broadcasted_iota(jnp.int32, (8, 1), 0)
    row8_k = lax.broadcasted_iota(jnp.int32, (8, HEAD_DIM), 0)
    lane = lax.broadcasted_iota(jnp.int32, (8, W), 1)
    dist_c = (W - lane).astype(F32)
    rows_w = lax.broadcasted_iota(jnp.int32, (W, HEAD_DIM), 0)
    for h in range(N_KV_HEADS):
        hs = slice(h * HEAD_DIM, (h + 1) * HEAD_DIM)
        k_new = _rms_rows(cols(OFF_K_A + h * HEAD_DIM), k_norm)
        v_new = cols(OFF_V_A + h * HEAD_DIM)
        kc = ck_ref[:, hs]
        vc = cv_ref[:, hs]
        nk_ref[:, hs] = jnp.where(rows_w == W - 1, k_new, pltpu.roll(kc, W - 1, 0))
        nv_ref[:, hs] = jnp.where(rows_w == W - 1, v_new, pltpu.roll(vc, W - 1, 0))

        q4 = jnp.zeros((8, HEAD_DIM), F32)
        slope = jnp.zeros((8, 1), F32)
        sink = jnp.zeros((8, 1), F32)
        for g in range(GROUP):
            a = h * GROUP + g
            qg = _rms_rows(cols(OFF_Q_A + a * HEAD_DIM), q_norm)
            q4 = jnp.where(row8_k == g, qg, q4)
            slope = jnp.where(row8 == g, ALIBI_SLOPES[a], slope)
            sink = jnp.where(row8 == g, sink_ref[a], sink)
        q4b = q4.astype(BF16)
        s_c = lax.dot_general(q4b, kc.astype(BF16), (((1,), (1,)), ((), ())),
                              preferred_element_type=F32)
        s_c = s_c * ATT_SCALE - slope * dist_c
        s_n = jnp.sum(q4 * k_new, axis=-1, keepdims=True) * ATT_SCALE
        m = jnp.maximum(jnp.maximum(jnp.max(s_c, axis=-1, keepdims=True), s_n), sink)
        p_c = jnp.exp(s_c - m)
        p_n = jnp.exp(s_n - m)
        den = jnp.sum(p_c, axis=-1, keepdims=True) + p_n + jnp.exp(sink - m)
        o = jnp.dot(p_c.astype(BF16), vc.astype(BF16), preferred_element_type=F32)
        o = (o + p_n * v_new) / den
        for g in range(GROUP):
            a = h * GROUP + g
            o_ref[:, LIN_WIDTH + a * HEAD_DIM:LIN_WIDTH + (a + 1) * HEAD_DIM] = o[g:g + 1, :]


def _mix_sample(proj, state, cache_k, cache_v, layer, sinks, lb, g_norm, q_norm, k_norm):
    B = proj.shape[0]
    vec = pl.BlockSpec((1, HEAD_DIM), lambda b: (0, 0))
    lbs = pl.BlockSpec((1, LIN_WIDTH), lambda b: (0, 0))
    cache_in = pl.BlockSpec((None, None, WINDOW, KV_WIDTH), lambda b: (layer, b, 0, 0))
    st_in = pl.BlockSpec((None, None, N_LIN_HEADS, HEAD_DIM, HEAD_DIM),
                         lambda b: (layer, b, 0, 0, 0))
    cache = pl.BlockSpec((None, WINDOW, KV_WIDTH), lambda b: (b, 0, 0))
    st = pl.BlockSpec((None, N_LIN_HEADS, HEAD_DIM, HEAD_DIM), lambda b: (b, 0, 0, 0))
    return pl.pallas_call(
        _mix_sample_kernel,
        grid=(B,),
        in_specs=[
            pl.BlockSpec(memory_space=pltpu.SMEM),
            pl.BlockSpec((None, 1, IN_WIDTH), lambda b: (b, 0, 0)),
            st_in, cache_in, cache_in, lbs, vec, vec, vec,
        ],
        out_specs=(
            pl.BlockSpec((None, 1, MIX_WIDTH), lambda b: (b, 0, 0)),
            st, cache, cache,
        ),
        out_shape=(
            jax.ShapeDtypeStruct((B, 1, MIX_WIDTH), F32),
            jax.ShapeDtypeStruct(state.shape[1:], F32),
            jax.ShapeDtypeStruct((B, WINDOW, KV_WIDTH), F32),
            jax.ShapeDtypeStruct((B, WINDOW, KV_WIDTH), F32),
        ),
        compiler_params=_cparams(("parallel",)),
        name="mix_sample",
    )(sinks, proj, state, cache_k, cache_v, lb, g_norm, q_norm, k_norm)


TILE_N = 512
P_TM = 1024
P_TM_DOWN = 512


def kernel(x_prompt, x_sample, cache_k, cache_v, state_hgrn, c_prompt, c_sample, lower_bounds,
           w_ada, b_ada, norm_mix, w_in, q_norm, k_norm, attn_sinks, g_norm, w_out, norm_ffn,
           w_gate, w_up, w_down):
    B, T, _ = x_prompt.shape
    BS = x_sample.shape[0]

    lb_all = _lower_bounds(lower_bounds)

    c_all = jnp.concatenate(
        [c_prompt, c_sample, jnp.zeros((MOD_ROWS - B - BS, D_MODEL), F32)], axis=0)
    mod = _modulation(c_all, w_ada, b_ada)


    cache_k2 = cache_k.reshape(DEPTH, BS, WINDOW, KV_WIDTH)
    cache_v2 = cache_v.reshape(DEPTH, BS, WINDOW, KV_WIDTH)

    yp = x_prompt
    ys = x_sample.reshape(1, BS, D_MODEL)
    pk, pv, ps, sk, sv, ss = [], [], [], [], [], []
    for l in range(DEPTH):
        mod_p = mod[l, :B].reshape(B, 1, 6 * D_MODEL)
        mod_s = mod[l, B:B + BS].reshape(1, BS, 6 * D_MODEL)
        gain_mix = norm_mix[l].reshape(1, D_MODEL)
        gain_ffn = norm_ffn[l].reshape(1, D_MODEL)
        lb = lb_all[l].reshape(1, LIN_WIDTH)
        gn = g_norm[l].reshape(1, HEAD_DIM)
        qn = q_norm[l].reshape(1, HEAD_DIM)
        kn = k_norm[l].reshape(1, HEAD_DIM)
        sinks = attn_sinks[l].astype(F32)

        proj_s, w_in_b = _norm_matmul(ys, mod_s, 0, gain_mix, w_in, l, BS, TILE_N)
        mixed_s, s_s, k_s, v_s = _mix_sample(
            proj_s.reshape(BS, 1, IN_WIDTH), state_hgrn, cache_k2, cache_v2, l,
            sinks, lb, gn, qn, kn)
        ys, w_out_b = _matmul_residual(mixed_s.reshape(1, BS, MIX_WIDTH), w_out, l, ys, mod_s,
                                       2 * D_MODEL, BS, TILE_N)
        ff_s, w_gate_b, w_up_b = _norm_gate_up(ys, mod_s, 3, gain_ffn, w_gate, w_up, l, BS, TILE_N)
        ys, w_down_b = _matmul_residual(ff_s, w_down, l, ys, mod_s, 5 * D_MODEL, BS, TILE_N)
        sk.append(k_s.reshape(BS, WINDOW, N_KV_HEADS, HEAD_DIM))
        sv.append(v_s.reshape(BS, WINDOW, N_KV_HEADS, HEAD_DIM))
        ss.append(s_s)

        proj = _norm_matmul(yp, mod_p, 0, gain_mix, w_in_b, None, P_TM, TILE_N)
        o_l, s_p = _hgrn_prompt(proj, lb, gn)
        o_a, k_p, v_p = _attn_prompt(proj, sinks, qn, kn)
        yp = _res_matmul_prompt([o_l, o_a], [(w_out_b, 0), (w_out_b, 1)], yp, mod_p, 2, P_TM,
                                "out_proj")
        ff = _norm_gate_up(yp, mod_p, 3, gain_ffn, w_gate_b, w_up_b, None, P_TM, TILE_N)
        yp = _res_matmul_prompt([ff], [w_down_b], yp, mod_p, 5, P_TM_DOWN, "down_proj")
        pk.append(k_p.reshape(B, WINDOW, N_KV_HEADS, HEAD_DIM))
        pv.append(v_p.reshape(B, WINDOW, N_KV_HEADS, HEAD_DIM))
        ps.append(s_p)

    return (yp, ys.reshape(BS, 1, D_MODEL), jnp.stack(pk), jnp.stack(pv), jnp.stack(ps),
            jnp.stack(sk), jnp.stack(sv), jnp.stack(ss))
```

```python
import functools
import math

import jax
import jax.numpy as jnp
from jax import lax
from jax.experimental import pallas as pl
from jax.experimental.pallas import tpu as pltpu

F32 = jnp.float32
BF16 = jnp.bfloat16

D_MODEL = 2048
DEPTH = 4
HEAD_DIM = 128
N_LIN_HEADS = 8
LIN_WIDTH = N_LIN_HEADS * HEAD_DIM
N_ATT_HEADS = 8
N_KV_HEADS = 2
GROUP = N_ATT_HEADS // N_KV_HEADS
ATT_WIDTH = N_ATT_HEADS * HEAD_DIM
KV_WIDTH = N_KV_HEADS * HEAD_DIM
MIX_WIDTH = LIN_WIDTH + ATT_WIDTH
WINDOW = 128
D_FF = 5632
IN_WIDTH = 4 * LIN_WIDTH + ATT_WIDTH + 2 * KV_WIDTH
EPS = 1e-6
ATT_SCALE = HEAD_DIM ** -0.5
ALIBI_SLOPES = tuple(2.0 ** (-8.0 * (a + 1) / N_ATT_HEADS) for a in range(N_ATT_HEADS))

COL_Q_L = 0
COL_F_L = LIN_WIDTH // HEAD_DIM
COL_I_L = 2 * LIN_WIDTH // HEAD_DIM
COL_G_L = 3 * LIN_WIDTH // HEAD_DIM
OFF_Q_A = 4 * LIN_WIDTH
OFF_K_A = OFF_Q_A + ATT_WIDTH
OFF_V_A = OFF_K_A + KV_WIDTH

MOD_ROWS = 40
HGRN_CHUNK = 128
HGRN_SUB = 8
LOG2E = math.log2(math.e)
BF16_ROWS = 16

VMEM_LIMIT = 56 * 1024 * 1024


def _cparams(sem):
    return pltpu.CompilerParams(dimension_semantics=sem, vmem_limit_bytes=VMEM_LIMIT)


def _silu(x):
    return x / (1.0 + jnp.exp(-x))


def _rms_rows(x, g):
    ms = jnp.mean(x * x, axis=-1, keepdims=True)
    return x * lax.rsqrt(ms + EPS) * g


def _forget_gate(f_raw, lb):
    t = jnp.exp(-jnp.abs(f_raw))
    inv = 1.0 / (1.0 + t)
    sig = jnp.where(f_raw >= 0.0, inv, t * inv)
    return t, (1.0 - lb) * sig


def _log_forget(f_raw, lb):
    t, fp = _forget_gate(f_raw, lb)
    log_sig = jnp.minimum(f_raw, 0.0) - jnp.log(1.0 + t)
    return jnp.where(lb > 0.0, jnp.log(lb + fp), log_sig), fp


def _lb_kernel(lb_ref, o_ref):
    x = lb_ref[...]
    m = jnp.max(x, axis=0, keepdims=True)
    e = jnp.exp(x - m)
    p = e / jnp.sum(e, axis=0, keepdims=True)
    acc = jnp.zeros_like(p[0:1])
    for l in range(DEPTH):
        if l > 0:
            acc = acc + p[l:l + 1]
        o_ref[l:l + 1, :] = acc


def _lower_bounds(lower_bounds):
    return pl.pallas_call(
        _lb_kernel,
        out_shape=jax.ShapeDtypeStruct((DEPTH, LIN_WIDTH), F32),
        name="lower_bounds",
    )(lower_bounds.astype(F32))


ADA_TN = 1024


def _ada_kernel(c_ref, w_ref, b_ref, o_ref):
    s = _silu(c_ref[...]).astype(BF16)
    acc = jnp.dot(s, w_ref[...].astype(BF16), preferred_element_type=F32)
    o_ref[...] = acc + b_ref[...]


def _modulation(c_all, w_ada, b_ada):
    n = 6 * D_MODEL
    return pl.pallas_call(
        _ada_kernel,
        grid=(DEPTH, n // ADA_TN),
        in_specs=[
            pl.BlockSpec((MOD_ROWS, D_MODEL), lambda l, j: (0, 0)),
            pl.BlockSpec((None, D_MODEL, ADA_TN), lambda l, j: (l, 0, j)),
            pl.BlockSpec((None, 1, ADA_TN), lambda l, j: (l, 0, j)),
        ],
        out_specs=pl.BlockSpec((None, MOD_ROWS, ADA_TN), lambda l, j: (l, 0, j)),
        out_shape=jax.ShapeDtypeStruct((DEPTH, MOD_ROWS, n), F32),
        compiler_params=_cparams(("parallel", "parallel")),
        name="modulation",
    )(c_all, w_ada, b_ada.reshape(DEPTH, 1, n))


NORM_ROWS = 16
NORM_UNROLL = 4


def _modulated_norm_into(h_scr, x_ref, sh_ref, sc_ref, g_ref):
    rows = x_ref.shape[0]
    ch = min(rows, NORM_ROWS)
    per_row = sc_ref.shape[0] != 1
    gain = g_ref[...]
    if not per_row:
        gain = gain * (1.0 + sc_ref[...])
        shift = sh_ref[...]

    def body(c, carry):
        rs = pl.ds(pl.multiple_of(c * ch, ch), ch)
        x = x_ref[rs, :]
        inv = lax.rsqrt(jnp.mean(x * x, axis=-1, keepdims=True) + EPS)
        if per_row:
            h = (x * inv * gain) * (1.0 + sc_ref[rs, :]) + sh_ref[rs, :]
        else:
            h = (x * inv) * gain + shift
        h_scr[rs, :] = h.astype(BF16)
        return carry

    lax.fori_loop(0, rows // ch, body, 0, unroll=min(NORM_UNROLL, rows // ch))


def _w_spec(w, layer, tn):
    if layer is None:
        assert w.shape[2] == tn
        return pl.BlockSpec((None, w.shape[1], tn), lambda b, i, j: (j, 0, 0))
    return pl.BlockSpec((None, w.shape[1], tn), lambda b, i, j: (layer, 0, j))


def _wb_out(w, tn):
    rows, n = w.shape[-2:]
    return (pl.BlockSpec((None, rows, tn), lambda b, i, j: (j, 0, 0)),
            jax.ShapeDtypeStruct((n // tn, rows, tn), BF16))


def _norm_mm_kernel(x_ref, sh_ref, sc_ref, g_ref, w_ref, o_ref, *rest, emit):
    h_scr = rest[-1]

    @pl.when(pl.program_id(2) == 0)
    def _():
        _modulated_norm_into(h_scr, x_ref, sh_ref, sc_ref, g_ref)

    w = w_ref[...].astype(BF16)
    if emit:
        rest[0][...] = w
    o_ref[...] = jnp.dot(h_scr[...], w, preferred_element_type=F32).astype(o_ref.dtype)


def _mod_specs(mod, mod_col, tm):
    if mod.shape[1] == 1:
        return [pl.BlockSpec((None, 1, D_MODEL), lambda b, i, j, c=mod_col + k: (b, 0, c))
                for k in range(2)]
    return [pl.BlockSpec((None, tm, D_MODEL), lambda b, i, j, c=mod_col + k: (b, i, c))
            for k in range(2)]


def _in_proj_kernel(x_ref, sh_ref, sc_ref, g_ref, w_ref, o32_ref, o16_ref, h_scr):
    @pl.when(pl.program_id(2) == 0)
    def _():
        _modulated_norm_into(h_scr, x_ref, sh_ref, sc_ref, g_ref)

    acc = jnp.dot(h_scr[...], w_ref[...], preferred_element_type=F32)
    o32_ref[...] = acc
    o16_ref[...] = acc.astype(BF16)


def _lookup(j, table):
    out = table[-1]
    for k in range(len(table) - 2, -1, -1):
        out = jnp.where(j == k, table[k], out)
    return out


def _in_proj_prompt(x, mod, gain, wt, f32_slabs, tm):
    G, R, _ = x.shape
    nt, _, tn = wt.shape
    n32 = len(f32_slabs)
    n16 = nt - n32
    order = tuple(f32_slabs) + tuple(s for s in range(nt) if s not in f32_slabs)
    return pl.pallas_call(
        _in_proj_kernel,
        grid=(G, R // tm, nt),
        in_specs=[pl.BlockSpec((None, tm, D_MODEL), lambda b, i, j: (b, i, 0))]
        + _mod_specs(mod, 0, tm)
        + [pl.BlockSpec((1, D_MODEL), lambda b, i, j: (0, 0)),
           pl.BlockSpec((None, D_MODEL, tn), lambda b, i, j: (_lookup(j, order), 0, 0))],
        out_specs=[
            pl.BlockSpec((None, None, tm, tn), lambda b, i, j: (jnp.minimum(j, n32), b, i, 0)),
            pl.BlockSpec((None, None, tm, tn),
                         lambda b, i, j: (jnp.where(j < n32, n16, j - n32), b, i, 0)),
        ],
        out_shape=[jax.ShapeDtypeStruct((n32 + 1, G, R, tn), F32),
                   jax.ShapeDtypeStruct((n16 + 1, G, R, tn), BF16)],
        scratch_shapes=[pltpu.VMEM((tm, D_MODEL), BF16)],
        compiler_params=_cparams(("parallel", "parallel", "arbitrary")),
        name="in_proj_prompt",
    )(x, mod, mod, gain, wt)


def _norm_matmul(x, mod, mod_col, gain, w, layer, tm, tn):
    G, R, _ = x.shape
    emit = True
    nt = w.shape[-1] // tn
    assert G * (R // tm) == 1
    wb_spec, wb_shape = _wb_out(w, tn)
    out_specs = [pl.BlockSpec((None, tm, tn), lambda b, i, j: (b, i, j)), wb_spec]
    out_shape = [jax.ShapeDtypeStruct((G, R, nt * tn), F32), wb_shape]
    return pl.pallas_call(
        functools.partial(_norm_mm_kernel, emit=emit),
        grid=(G, R // tm, nt),
        in_specs=[pl.BlockSpec((None, tm, D_MODEL), lambda b, i, j: (b, i, 0))]
        + _mod_specs(mod, mod_col, tm)
        + [pl.BlockSpec((1, D_MODEL), lambda b, i, j: (0, 0)), _w_spec(w, layer, tn)],
        out_specs=out_specs,
        out_shape=out_shape,
        scratch_shapes=[pltpu.VMEM((tm, D_MODEL), BF16)],
        compiler_params=_cparams(("parallel", "parallel", "arbitrary")),
        name="norm_in_proj",
    )(x, mod, mod, gain, w)


def _norm_gu_kernel(x_ref, sh_ref, sc_ref, g_ref, wg_ref, wu_ref, o_ref, *rest, emit):
    h_scr = rest[-1]

    @pl.when(pl.program_id(2) == 0)
    def _():
        _modulated_norm_into(h_scr, x_ref, sh_ref, sc_ref, g_ref)

    h = h_scr[...]
    wg = wg_ref[...].astype(BF16)
    wu = wu_ref[...].astype(BF16)
    if emit:
        rest[0][...] = wg
        rest[1][...] = wu
    gate = jnp.dot(h, wg, preferred_element_type=F32)
    up = jnp.dot(h, wu, preferred_element_type=F32)
    o_ref[...] = (_silu(gate) * up).astype(o_ref.dtype)


def _norm_gate_up(x, mod, mod_col, gain, wg, wu, layer, tm, tn):
    G, R, _ = x.shape
    emit = layer is not None
    nt = wg.shape[-1] // tn if emit else wg.shape[0]
    if emit:
        assert G * (R // tm) == 1
        wb_spec, wb_shape = _wb_out(wg, tn)
        out_specs = [pl.BlockSpec((None, tm, tn), lambda b, i, j: (b, i, j)), wb_spec, wb_spec]
        out_shape = [jax.ShapeDtypeStruct((G, R, nt * tn), BF16), wb_shape, wb_shape]
    else:
        out_specs = pl.BlockSpec((None, None, tm, tn), lambda b, i, j: (j, b, i, 0))
        out_shape = jax.ShapeDtypeStruct((nt, G, R, tn), BF16)
    return pl.pallas_call(
        functools.partial(_norm_gu_kernel, emit=emit),
        grid=(G, R // tm, nt),
        in_specs=[pl.BlockSpec((None, tm, D_MODEL), lambda b, i, j: (b, i, 0))]
        + _mod_specs(mod, mod_col, tm)
        + [pl.BlockSpec((1, D_MODEL), lambda b, i, j: (0, 0)),
           _w_spec(wg, layer, tn), _w_spec(wu, layer, tn)],
        out_specs=out_specs,
        out_shape=out_shape,
        scratch_shapes=[pltpu.VMEM((tm, D_MODEL), BF16)],
        compiler_params=_cparams(("parallel", "parallel", "arbitrary")),
        name="norm_gate_up",
    )(x, mod, mod, gain, wg, wu)


def _res_matmul_kernel(*refs, n_a):
    a_refs, w_refs = refs[:n_a], refs[n_a:2 * n_a]
    x_ref, gt_ref, o_ref = refs[2 * n_a:]
    tn = w_refs[0].shape[-1]
    acc = None
    for a_ref, w_ref in zip(a_refs, w_refs):
        slabs = [a_ref[s] for s in range(a_ref.shape[0])] if len(a_ref.shape) == 3 else [a_ref[...]]
        ts = slabs[0].shape[1]
        for s, a in enumerate(slabs):
            part = jnp.dot(a, w_ref[s * ts:(s + 1) * ts, :], preferred_element_type=F32)
            acc = part if acc is None else acc + part
    o_ref[...] = x_ref[...] + gt_ref[...] * acc


def _res_matmul_prompt(a_list, w_list, x, mod, gate_col, tm, name):
    G, R, _ = x.shape
    a_specs, w_specs, ws = [], [], []
    for a, w in zip(a_list, w_list):
        w, row_blk = w if isinstance(w, tuple) else (w, 0)
        if a.ndim == 4:
            S, _, _, ts = a.shape
            a_specs.append(pl.BlockSpec((S, None, tm, ts), lambda b, i, j: (0, b, i, 0)))
            k = S * ts
        else:
            k = a.shape[2]
            a_specs.append(pl.BlockSpec((None, tm, k), lambda b, i, j: (b, i, 0)))
        nt, _, tn = w.shape
        w_specs.append(pl.BlockSpec((None, k, tn), lambda b, i, j, r=row_blk: (j, r, 0)))
        ws.append(w)
    tile = pl.BlockSpec((None, tm, tn), lambda b, i, j: (b, i, j))
    return pl.pallas_call(
        functools.partial(_res_matmul_kernel, n_a=len(a_list)),
        grid=(G, R // tm, nt),
        in_specs=a_specs + w_specs
        + [tile, pl.BlockSpec((None, 1, tn), lambda b, i, j: (b, 0, gate_col * nt + j))],
        out_specs=tile,
        out_shape=jax.ShapeDtypeStruct((G, R, D_MODEL), F32),
        compiler_params=_cparams(("parallel", "parallel", "arbitrary")),
        name=name,
    )(*a_list, *ws, x, mod)


def _mm_res_kernel(a_ref, w_ref, x_ref, gt_ref, o_ref, wb_ref):
    w = w_ref[...].astype(BF16)
    wb_ref[...] = w
    acc = jnp.dot(a_ref[...].astype(BF16), w, preferred_element_type=F32)
    o_ref[...] = x_ref[...] + gt_ref[...] * acc


def _matmul_residual(a, w, layer, x, mod, mod_off, tm, tn):
    G, R, K = a.shape
    assert G * (R // tm) == 1 and mod.shape[1] == R
    gcol = mod_off // tn
    wb_spec, wb_shape = _wb_out(w, tn)
    return pl.pallas_call(
        _mm_res_kernel,
        grid=(G, R // tm, D_MODEL // tn),
        in_specs=[
            pl.BlockSpec((None, tm, K), lambda b, i, j: (b, i, 0)),
            _w_spec(w, layer, tn),
            pl.BlockSpec((None, tm, tn), lambda b, i, j: (b, i, j)),
            pl.BlockSpec((None, tm, tn), lambda b, i, j: (b, i, gcol + j)),
        ],
        out_specs=[pl.BlockSpec((None, tm, tn), lambda b, i, j: (b, i, j)), wb_spec],
        out_shape=[jax.ShapeDtypeStruct((G, R, D_MODEL), F32), wb_shape],
        compiler_params=_cparams(("parallel", "parallel", "arbitrary")),
        name="matmul_residual",
    )(a, w, x, mod)


def _split3_bf16(x):
    hi = x.astype(BF16)
    r1 = x - hi.astype(F32)
    mid = r1.astype(BF16)
    lo = (r1 - mid.astype(F32)).astype(BF16)
    return hi, mid, lo


HGRN_HP = 4
HGRN_TB = 1024
HGRN_SEP_MIN = 32


def _hgrn_prompt_kernel(q_ref, f_ref, i_ref, g_ref, lb_ref, gn_ref, o_ref, s_ref, st_scr):
    C = HGRN_CHUNK
    P = HGRN_SUB
    D = HEAD_DIM
    t_idx = pl.program_id(2)
    n_chunks = q_ref.shape[0] // C
    row = lax.broadcasted_iota(jnp.int32, (C, C), 0)
    col = lax.broadcasted_iota(jnp.int32, (C, C), 1)
    tri = (row >= col).astype(BF16)
    sel_r = lax.broadcasted_iota(jnp.int32, (P * D, C), 0)
    sel_c = lax.broadcasted_iota(jnp.int32, (P * D, C), 1)
    psh = P.bit_length() - 1
    sel = ((sel_c & (P - 1)) == (sel_r >> (D.bit_length() - 1))).astype(BF16)
    keep_diag = ((col >> psh) == (row >> psh)) & ((col & (P - 1)) <= (row & (P - 1)))
    g_norm = gn_ref[...]

    levels = []
    L = C // 2
    while L >= P:
        levels.append(L)
        L //= 2

    @pl.when(t_idx == 0)
    def _():
        st_scr[...] = jnp.zeros_like(st_scr)

    def chunk_body(c, carry):
        rs = pl.ds(pl.multiple_of(c * C, C), C)
        qs, ks, vs, logfs, parts = [], [], [], [], []
        for hh in range(HGRN_HP):
            cs = slice(hh * D, (hh + 1) * D)
            lb = lb_ref[:, cs]
            logf, fp = _log_forget(f_ref[rs, cs], lb)
            qs.append(_silu(q_ref[rs, cs].astype(F32)))
            ks.append((1.0 - lb) - fp)
            vs.append(i_ref[rs, cs].astype(BF16))
            parts.extend(_split3_bf16(logf * LOG2E))
        cum = jnp.dot(tri, jnp.concatenate(parts, axis=1), preferred_element_type=F32)

        for hh in range(HGRN_HP):
            cs = slice(hh * D, (hh + 1) * D)
            q, k, v16 = qs[hh], ks[hh], vs[hh]
            b = (cum[:, (3 * hh) * D:(3 * hh + 1) * D] + cum[:, (3 * hh + 1) * D:(3 * hh + 2) * D]
                 + cum[:, (3 * hh + 2) * D:(3 * hh + 3) * D])
            b_end = b[C - 1:C, :]
            st = st_scr[hh]
            o = lax.dot_general((q * jnp.exp2(b)).astype(BF16), st.astype(BF16),
                                (((1,), (1,)), ((), ())), preferred_element_type=F32)

            q_sep, k_sep = [], []
            a_masked = []
            for L in levels:
                q_rows, k_rows = [], []
                for p in range(C // (2 * L)):
                    lo, mid, hi = 2 * L * p, 2 * L * p + L, 2 * L * (p + 1)
                    ref = b[mid - 1:mid, :]
                    zf = jnp.zeros((L, D), F32)
                    kk = jnp.concatenate([k[lo:mid] * jnp.exp2(ref - b[lo:mid]), zf], axis=0)
                    qq = jnp.concatenate([zf, q[mid:hi] * jnp.exp2(b[mid:hi] - ref)], axis=0)
                    q_rows.append(qq.astype(BF16))
                    k_rows.append(kk.astype(BF16))
                if L >= HGRN_SEP_MIN:
                    for p in range(len(q_rows)):
                        pad = lambda x: jnp.concatenate(
                            [y for y in (jnp.zeros((2 * L * p, D), BF16), x,
                                         jnp.zeros((C - 2 * L * (p + 1), D), BF16)) if y.shape[0] > 0],
                            axis=0)
                        q_sep.append(pad(q_rows[p]))
                        k_sep.append(pad(k_rows[p]))
                else:
                    sh = L.bit_length() - 1
                    a_l = lax.dot_general(jnp.concatenate(q_rows, axis=0),
                                          jnp.concatenate(k_rows, axis=0),
                                          (((1,), (1,)), ((), ())), preferred_element_type=F32)
                    keep = (((row >> sh) & 1) == 1) & ((col >> sh) == (row >> sh) - 1)
                    a_masked.append((keep, a_l))
            a = lax.dot_general(jnp.concatenate(q_sep, axis=1), jnp.concatenate(k_sep, axis=1),
                                (((1,), (1,)), ((), ())), preferred_element_type=F32)
            for keep, a_l in a_masked:
                a = jnp.where(keep, a_l, a)

            b3 = b.reshape(C // P, P, D)
            k3 = k.reshape(C // P, P, D)
            zs = []
            for s in range(P):
                bs = jnp.broadcast_to(b3[:, s:s + 1, :], (C // P, P, D)).reshape(C, D)
                ksb = jnp.broadcast_to(k3[:, s:s + 1, :], (C // P, P, D)).reshape(C, D)
                e = jnp.exp2(jnp.minimum(b - bs, 0.0))
                zs.append((q * ksb * e).astype(BF16))
            a_d = jnp.dot(jnp.concatenate(zs, axis=1), sel, preferred_element_type=F32)
            a = jnp.where(keep_diag, a_d, a)

            o = o + jnp.dot(a.astype(BF16), v16, preferred_element_type=F32)

            k_end = (k * jnp.exp2(b_end - b)).astype(BF16)
            upd = lax.dot_general(v16, k_end, (((0,), (0,)), ((), ())), preferred_element_type=F32)
            st_scr[hh] = st * jnp.exp2(b_end) + upd

            o = _rms_rows(o, g_norm) * _silu(g_ref[rs, cs].astype(F32))
            o_ref[rs, cs] = o.astype(o_ref.dtype)
        return carry

    lax.fori_loop(0, n_chunks, chunk_body, 0)

    @pl.when(t_idx == pl.num_programs(2) - 1)
    def _():
        for hh in range(HGRN_HP):
            s_ref[hh] = st_scr[hh].T


def _hgrn_prompt(proj32, proj16, lb, g_norm):
    _, B, T, W = proj32.shape
    assert W == HGRN_HP * HEAD_DIM
    seq = lambda slab: pl.BlockSpec((None, None, HGRN_TB, W),
                                    lambda b, h, t: (slab + h, b, t, 0))
    return pl.pallas_call(
        _hgrn_prompt_kernel,
        grid=(B, N_LIN_HEADS // HGRN_HP, T // HGRN_TB),
        in_specs=[
            seq(0), seq(0), seq(2), seq(4),
            pl.BlockSpec((1, W), lambda b, h, t: (0, h)),
            pl.BlockSpec((1, HEAD_DIM), lambda b, h, t: (0, 0)),
        ],
        out_specs=(
            pl.BlockSpec((None, HGRN_TB, W), lambda b, h, t: (b, t, h)),
            pl.BlockSpec((None, HGRN_HP, HEAD_DIM, HEAD_DIM), lambda b, h, t: (b, h, 0, 0)),
        ),
        out_shape=(
            jax.ShapeDtypeStruct((B, T, LIN_WIDTH), BF16),
            jax.ShapeDtypeStruct((B, N_LIN_HEADS, HEAD_DIM, HEAD_DIM), F32),
        ),
        scratch_shapes=[pltpu.VMEM((HGRN_HP, HEAD_DIM, HEAD_DIM), F32)],
        compiler_params=_cparams(("parallel", "parallel", "arbitrary")),
        name="hgrn_prompt",
    )(proj16, proj32, proj16, proj16, lb, g_norm)


def _attn_prompt_kernel(sink_ref, q_ref, kvc_ref, kvp_ref, qn_ref, kn_ref,
                        o_ref, nk_ref, nv_ref):
    n = pl.program_id(1)
    W = WINDOW
    qi = lax.broadcasted_iota(jnp.int32, (W, 2 * W), 0)
    kj = lax.broadcasted_iota(jnp.int32, (W, 2 * W), 1)
    dist = W + qi - kj
    valid = (dist >= 0) & (dist <= W) & ((n > 0) | (kj >= W))
    dist_m = jnp.where(valid, dist.astype(F32), jnp.inf)
    q_gain = qn_ref[...] * (ATT_SCALE * LOG2E)
    k_norm = kn_ref[...]

    new_k = []
    for h in range(N_KV_HEADS):
        hs = slice(h * HEAD_DIM, (h + 1) * HEAD_DIM)
        vs = slice(KV_WIDTH + h * HEAD_DIM, KV_WIDTH + (h + 1) * HEAD_DIM)
        kc = _rms_rows(kvc_ref[:, hs], k_norm)
        kp = _rms_rows(kvp_ref[:, hs], k_norm)
        new_k.append(kc)
        k2 = jnp.concatenate([kp, kc], axis=0).astype(BF16)
        v2 = jnp.concatenate([kvp_ref[:, vs], kvc_ref[:, vs]], axis=0).astype(BF16)

        for g in range(GROUP):
            a = h * GROUP + g
            cs = slice(a * HEAD_DIM, (a + 1) * HEAD_DIM)
            qh = _rms_rows(q_ref[h, :, g * HEAD_DIM:(g + 1) * HEAD_DIM].astype(F32),
                           q_gain).astype(BF16)
            s = lax.dot_general(qh, k2, (((1,), (1,)), ((), ())), preferred_element_type=F32)
            s = s + (-ALIBI_SLOPES[a] * LOG2E) * dist_m
            sink = sink_ref[a] * LOG2E
            m = jnp.maximum(jnp.max(s, axis=-1, keepdims=True), sink)
            p = jnp.exp2(s - m)
            den = jnp.sum(p, axis=-1, keepdims=True) + jnp.exp2(sink - m)
            o = jnp.dot(p.astype(BF16), v2, preferred_element_type=F32) * (1.0 / den)
            o_ref[:, cs] = o.astype(o_ref.dtype)

    @pl.when(n == pl.num_programs(1) - 1)
    def _():
        for h in range(N_KV_HEADS):
            hs = slice(h * HEAD_DIM, (h + 1) * HEAD_DIM)
            nk_ref[:, hs] = new_k[h]
            nv_ref[:, hs] = kvc_ref[:, KV_WIDTH + h * HEAD_DIM:KV_WIDTH + (h + 1) * HEAD_DIM]


def _attn_prompt(proj32, proj16, sinks, q_norm, k_norm):
    _, B, T, W = proj32.shape
    assert W == GROUP * HEAD_DIM == 2 * KV_WIDTH
    nb = T // WINDOW
    q_slab = 6 // N_KV_HEADS
    kv_slab = 2
    prev = lambda n: jnp.maximum(n - 1, 0)
    return pl.pallas_call(
        _attn_prompt_kernel,
        grid=(B, nb),
        in_specs=[
            pl.BlockSpec(memory_space=pltpu.SMEM),
            pl.BlockSpec((N_KV_HEADS, None, WINDOW, W), lambda b, n: (q_slab, b, n, 0)),
            pl.BlockSpec((None, None, WINDOW, W), lambda b, n: (kv_slab, b, n, 0)),
            pl.BlockSpec((None, None, WINDOW, W), lambda b, n: (kv_slab, b, prev(n), 0)),
            pl.BlockSpec((1, HEAD_DIM), lambda b, n: (0, 0)),
            pl.BlockSpec((1, HEAD_DIM), lambda b, n: (0, 0)),
        ],
        out_specs=(
            pl.BlockSpec((None, WINDOW, ATT_WIDTH), lambda b, n: (b, n, 0)),
            pl.BlockSpec((None, WINDOW, KV_WIDTH), lambda b, n: (b, 0, 0)),
            pl.BlockSpec((None, WINDOW, KV_WIDTH), lambda b, n: (b, 0, 0)),
        ),
        out_shape=(
            jax.ShapeDtypeStruct((B, T, ATT_WIDTH), BF16),
            jax.ShapeDtypeStruct((B, WINDOW, KV_WIDTH), F32),
            jax.ShapeDtypeStruct((B, WINDOW, KV_WIDTH), F32),
        ),
        compiler_params=_cparams(("parallel", "arbitrary")),
        name="attn_prompt",
    )(sinks, proj16, proj32, proj32, q_norm, k_norm)


def _row_to_col(x_row, eye):
    return jnp.sum(jnp.where(eye, x_row, 0.0), axis=1, keepdims=True)


def _mix_sample_kernel(sink_ref, p_ref, s_ref, ck_ref, cv_ref, lb_ref,
                       gn_ref, qn_ref, kn_ref, o_ref, ns_ref, nk_ref, nv_ref):
    W = WINDOW
    er = lax.broadcasted_iota(jnp.int32, (HEAD_DIM, HEAD_DIM), 0)
    ec = lax.broadcasted_iota(jnp.int32, (HEAD_DIM, HEAD_DIM), 1)
    eye = er == ec
    g_norm = gn_ref[...]
    q_norm = qn_ref[...]
    k_norm = kn_ref[...]

    def cols(off, width=HEAD_DIM):
        return p_ref[:, off:off + width]

    for h in range(N_LIN_HEADS):
        c0 = h * HEAD_DIM
        q = _silu(cols(c0))
        lb = lb_ref[:, c0:c0 + HEAD_DIM]
        _, fp = _forget_gate(cols(LIN_WIDTH + c0), lb)
        f_col = _row_to_col(lb + fp, eye)
        k_col = 1.0 - f_col
        v = cols(2 * LIN_WIDTH + c0)
        s_new = s_ref[h] * f_col + k_col * v
        ns_ref[h] = s_new
        q8 = jnp.broadcast_to(q, (8, HEAD_DIM)).astype(BF16)
        o = jnp.dot(q8, s_new.astype(BF16), preferred_element_type=F32)[0:1, :]
        o = _rms_rows(o, g_norm) * _silu(cols(3 * LIN_WIDTH + c0))
        o_ref[:, c0:c0 + HEAD_DIM] = o

    row8 = lax.broadcasted_iota(jnp.int32, (8, 1), 0)
    row8_k = lax.broadcasted_iota(jnp.int32, (8, HEAD_DIM), 0)
    lane = lax.broadcasted_iota(jnp.int32, (8, W), 1)
    dist_c = (W - lane).astype(F32)
    rows_w = lax.broadcasted_iota(jnp.int32, (W, HEAD_DIM), 0)
    for h in range(N_KV_HEADS):
        hs = slice(h * HEAD_DIM, (h + 1) * HEAD_DIM)
        k_new = _rms_rows(cols(OFF_K_A + h * HEAD_DIM), k_norm)
        v_new = cols(OFF_V_A + h * HEAD_DIM)
        kc = ck_ref[:, hs]
        vc = cv_ref[:, hs]
        nk_ref[:, hs] = jnp.where(rows_w == W - 1, k_new, pltpu.roll(kc, W - 1, 0))
        nv_ref[:, hs] = jnp.where(rows_w == W - 1, v_new, pltpu.roll(vc, W - 1, 0))

        q4 = jnp.zeros((8, HEAD_DIM), F32)
        slope = jnp.zeros((8, 1), F32)
        sink = jnp.zeros((8, 1), F32)
        for g in range(GROUP):
            a = h * GROUP + g
            qg = _rms_rows(cols(OFF_Q_A + a * HEAD_DIM), q_norm)
            q4 = jnp.where(row8_k == g, qg, q4)
            slope = jnp.where(row8 == g, ALIBI_SLOPES[a], slope)
            sink = jnp.where(row8 == g, sink_ref[a], sink)
        q4b = q4.astype(BF16)
        s_c = lax.dot_general(q4b, kc.astype(BF16), (((1,), (1,)), ((), ())),
                              preferred_element_type=F32)
        s_c = s_c * ATT_SCALE - slope * dist_c
        s_n = jnp.sum(q4 * k_new, axis=-1, keepdims=True) * ATT_SCALE
        m = jnp.maximum(jnp.maximum(jnp.max(s_c, axis=-1, keepdims=True), s_n), sink)
        p_c = jnp.exp(s_c - m)
        p_n = jnp.exp(s_n - m)
        den = jnp.sum(p_c, axis=-1, keepdims=True) + p_n + jnp.exp(sink - m)
        o = jnp.dot(p_c.astype(BF16), vc.astype(BF16), preferred_element_type=F32)
        o = (o + p_n * v_new) / den
        for g in range(GROUP):
            a = h * GROUP + g
            o_ref[:, LIN_WIDTH + a * HEAD_DIM:LIN_WIDTH + (a + 1) * HEAD_DIM] = o[g:g + 1, :]


def _mix_sample(proj, state, cache_k, cache_v, layer, sinks, lb, g_norm, q_norm, k_norm):
    B = proj.shape[0]
    vec = pl.BlockSpec((1, HEAD_DIM), lambda b: (0, 0))
    lbs = pl.BlockSpec((1, LIN_WIDTH), lambda b: (0, 0))
    cache_in = pl.BlockSpec((None, None, WINDOW, KV_WIDTH), lambda b: (layer, b, 0, 0))
    st_in = pl.BlockSpec((None, None, N_LIN_HEADS, HEAD_DIM, HEAD_DIM),
                         lambda b: (layer, b, 0, 0, 0))
    cache = pl.BlockSpec((None, WINDOW, KV_WIDTH), lambda b: (b, 0, 0))
    st = pl.BlockSpec((None, N_LIN_HEADS, HEAD_DIM, HEAD_DIM), lambda b: (b, 0, 0, 0))
    return pl.pallas_call(
        _mix_sample_kernel,
        grid=(B,),
        in_specs=[
            pl.BlockSpec(memory_space=pltpu.SMEM),
            pl.BlockSpec((None, 1, IN_WIDTH), lambda b: (b, 0, 0)),
            st_in, cache_in, cache_in, lbs, vec, vec, vec,
        ],
        out_specs=(
            pl.BlockSpec((None, 1, MIX_WIDTH), lambda b: (b, 0, 0)),
            st, cache, cache,
        ),
        out_shape=(
            jax.ShapeDtypeStruct((B, 1, MIX_WIDTH), F32),
            jax.ShapeDtypeStruct(state.shape[1:], F32),
            jax.ShapeDtypeStruct((B, WINDOW, KV_WIDTH), F32),
            jax.ShapeDtypeStruct((B, WINDOW, KV_WIDTH), F32),
        ),
        compiler_params=_cparams(("parallel",)),
        name="mix_sample",
    )(sinks, proj, state, cache_k, cache_v, lb, g_norm, q_norm, k_norm)


TILE_N = 512
P_TM = 1024
F32_SLABS = (COL_F_L // HGRN_HP, COL_F_L // HGRN_HP + 1, OFF_K_A // TILE_N)


def kernel(x_prompt, x_sample, cache_k, cache_v, state_hgrn, c_prompt, c_sample, lower_bounds,
           w_ada, b_ada, norm_mix, w_in, q_norm, k_norm, attn_sinks, g_norm, w_out, norm_ffn,
           w_gate, w_up, w_down):
    B, T, _ = x_prompt.shape
    BS = x_sample.shape[0]

    lb_all = _lower_bounds(lower_bounds)

    c_all = jnp.concatenate(
        [c_prompt, c_sample, jnp.zeros((MOD_ROWS - B - BS, D_MODEL), F32)], axis=0)
    mod = _modulation(c_all, w_ada, b_ada)


    cache_k2 = cache_k.reshape(DEPTH, BS, WINDOW, KV_WIDTH)
    cache_v2 = cache_v.reshape(DEPTH, BS, WINDOW, KV_WIDTH)

    yp = x_prompt
    ys = x_sample.reshape(1, BS, D_MODEL)
    pk, pv, ps, sk, sv, ss = [], [], [], [], [], []
    for l in range(DEPTH):
        mod_p = mod[l, :B].reshape(B, 1, 6 * D_MODEL)
        mod_s = mod[l, B:B + BS].reshape(1, BS, 6 * D_MODEL)
        gain_mix = norm_mix[l].reshape(1, D_MODEL)
        gain_ffn = norm_ffn[l].reshape(1, D_MODEL)
        lb = lb_all[l].reshape(1, LIN_WIDTH)
        gn = g_norm[l].reshape(1, HEAD_DIM)
        qn = q_norm[l].reshape(1, HEAD_DIM)
        kn = k_norm[l].reshape(1, HEAD_DIM)
        sinks = attn_sinks[l].astype(F32)

        proj_s, w_in_b = _norm_matmul(ys, mod_s, 0, gain_mix, w_in, l, BS, TILE_N)
        mixed_s, s_s, k_s, v_s = _mix_sample(
            proj_s.reshape(BS, 1, IN_WIDTH), state_hgrn, cache_k2, cache_v2, l,
            sinks, lb, gn, qn, kn)
        ys, w_out_b = _matmul_residual(mixed_s.reshape(1, BS, MIX_WIDTH), w_out, l, ys, mod_s,
                                       2 * D_MODEL, BS, TILE_N)
        ff_s, w_gate_b, w_up_b = _norm_gate_up(ys, mod_s, 3, gain_ffn, w_gate, w_up, l, BS, TILE_N)
        ys, w_down_b = _matmul_residual(ff_s, w_down, l, ys, mod_s, 5 * D_MODEL, BS, TILE_N)
        sk.append(k_s.reshape(BS, WINDOW, N_KV_HEADS, HEAD_DIM))
        sv.append(v_s.reshape(BS, WINDOW, N_KV_HEADS, HEAD_DIM))
        ss.append(s_s)

        proj32, proj16 = _in_proj_prompt(yp, mod_p, gain_mix, w_in_b, F32_SLABS, P_TM)
        o_l, s_p = _hgrn_prompt(proj32, proj16, lb, gn)
        o_a, k_p, v_p = _attn_prompt(proj32, proj16, sinks, qn, kn)
        yp = _res_matmul_prompt([o_l, o_a], [(w_out_b, 0), (w_out_b, 1)], yp, mod_p, 2, P_TM,
                                "out_proj")
        ff = _norm_gate_up(yp, mod_p, 3, gain_ffn, w_gate_b, w_up_b, None, P_TM, TILE_N)
        yp = _res_matmul_prompt([ff], [w_down_b], yp, mod_p, 5, P_TM, "down_proj")
        pk.append(k_p.reshape(B, WINDOW, N_KV_HEADS, HEAD_DIM))
        pv.append(v_p.reshape(B, WINDOW, N_KV_HEADS, HEAD_DIM))
        ps.append(s_p)

    return (yp, ys.reshape(BS, 1, D_MODEL), jnp.stack(pk), jnp.stack(pv), jnp.stack(ps),
            jnp.stack(sk), jnp.stack(sv), jnp.stack(ss))
```

```python
import functools
import math

import jax
import jax.numpy as jnp
from jax import lax
from jax.experimental import pallas as pl
from jax.experimental.pallas import tpu as pltpu

F32 = jnp.float32
BF16 = jnp.bfloat16

D_MODEL = 2048
DEPTH = 4
HEAD_DIM = 128
N_LIN_HEADS = 8
LIN_WIDTH = N_LIN_HEADS * HEAD_DIM
N_ATT_HEADS = 8
N_KV_HEADS = 2
GROUP = N_ATT_HEADS // N_KV_HEADS
ATT_WIDTH = N_ATT_HEADS * HEAD_DIM
KV_WIDTH = N_KV_HEADS * HEAD_DIM
MIX_WIDTH = LIN_WIDTH + ATT_WIDTH
WINDOW = 128
D_FF = 5632
IN_WIDTH = 4 * LIN_WIDTH + ATT_WIDTH + 2 * KV_WIDTH
EPS = 1e-6
ATT_SCALE = HEAD_DIM ** -0.5
ALIBI_SLOPES = tuple(2.0 ** (-8.0 * (a + 1) / N_ATT_HEADS) for a in range(N_ATT_HEADS))

OFF_Q_A = 4 * LIN_WIDTH
OFF_K_A = OFF_Q_A + ATT_WIDTH
OFF_V_A = OFF_K_A + KV_WIDTH

MOD_ROWS = 40
HGRN_CHUNK = 128
HGRN_SUB = 8
LOG2E = math.log2(math.e)

VMEM_LIMIT = 56 * 1024 * 1024


def _cparams(sem):
    return pltpu.CompilerParams(dimension_semantics=sem, vmem_limit_bytes=VMEM_LIMIT)


def _silu(x):
    return x / (1.0 + jnp.exp(-x))


def _rms_rows(x, g):
    ms = jnp.mean(x * x, axis=-1, keepdims=True)
    return x * lax.rsqrt(ms + EPS) * g


def _forget_gate(f_raw, lb):
    t = jnp.exp(-jnp.abs(f_raw))
    inv = 1.0 / (1.0 + t)
    sig = jnp.where(f_raw >= 0.0, inv, t * inv)
    return t, (1.0 - lb) * sig


def _log_forget(f_raw, lb):
    t, fp = _forget_gate(f_raw, lb)
    log_sig = jnp.minimum(f_raw, 0.0) - jnp.log(1.0 + t)
    return jnp.where(lb > 0.0, jnp.log(lb + fp), log_sig), fp


def _lb_kernel(lb_ref, o_ref):
    x = lb_ref[...]
    m = jnp.max(x, axis=0, keepdims=True)
    e = jnp.exp(x - m)
    p = e / jnp.sum(e, axis=0, keepdims=True)
    acc = jnp.zeros_like(p[0:1])
    for l in range(DEPTH):
        if l > 0:
            acc = acc + p[l:l + 1]
        o_ref[l:l + 1, :] = acc


def _lower_bounds(lower_bounds):
    return pl.pallas_call(
        _lb_kernel,
        out_shape=jax.ShapeDtypeStruct((DEPTH, LIN_WIDTH), F32),
        name="lower_bounds",
    )(lower_bounds.astype(F32))


ADA_TN = 1024


def _ada_kernel(c_ref, w_ref, b_ref, o_ref):
    s = _silu(c_ref[...]).astype(BF16)
    acc = jnp.dot(s, w_ref[...].astype(BF16), preferred_element_type=F32)
    o_ref[...] = acc + b_ref[...]


def _modulation(c_all, w_ada, b_ada):
    n = 6 * D_MODEL
    return pl.pallas_call(
        _ada_kernel,
        grid=(DEPTH, n // ADA_TN),
        in_specs=[
            pl.BlockSpec((MOD_ROWS, D_MODEL), lambda l, j: (0, 0)),
            pl.BlockSpec((None, D_MODEL, ADA_TN), lambda l, j: (l, 0, j)),
            pl.BlockSpec((None, 1, ADA_TN), lambda l, j: (l, 0, j)),
        ],
        out_specs=pl.BlockSpec((None, MOD_ROWS, ADA_TN), lambda l, j: (l, 0, j)),
        out_shape=jax.ShapeDtypeStruct((DEPTH, MOD_ROWS, n), F32),
        compiler_params=_cparams(("parallel", "parallel")),
        name="modulation",
    )(c_all, w_ada, b_ada.reshape(DEPTH, 1, n))


NORM_ROWS = 16
NORM_UNROLL = 4


def _modulated_norm_into(h_scr, x_ref, sh_ref, sc_ref, g_ref):
    rows = x_ref.shape[0]
    ch = min(rows, NORM_ROWS)
    per_row = sc_ref.shape[0] != 1
    gain = g_ref[...]
    if not per_row:
        gain = gain * (1.0 + sc_ref[...])
        shift = sh_ref[...]

    def body(c, carry):
        rs = pl.ds(pl.multiple_of(c * ch, ch), ch)
        x = x_ref[rs, :]
        inv = lax.rsqrt(jnp.mean(x * x, axis=-1, keepdims=True) + EPS)
        if per_row:
            h = (x * inv * gain) * (1.0 + sc_ref[rs, :]) + sh_ref[rs, :]
        else:
            h = (x * inv) * gain + shift
        h_scr[rs, :] = h.astype(BF16)
        return carry

    lax.fori_loop(0, rows // ch, body, 0, unroll=min(NORM_UNROLL, rows // ch))


def _w_spec(w, layer, tn):
    if layer is None:
        assert w.shape[2] == tn
        return pl.BlockSpec((None, w.shape[1], tn), lambda b, i, j: (j, 0, 0))
    return pl.BlockSpec((None, w.shape[1], tn), lambda b, i, j: (layer, 0, j))


def _wb_out(w, tn):
    rows, n = w.shape[-2:]
    return (pl.BlockSpec((None, rows, tn), lambda b, i, j: (j, 0, 0)),
            jax.ShapeDtypeStruct((n // tn, rows, tn), BF16))


def _norm_mm_kernel(x_ref, sh_ref, sc_ref, g_ref, w_ref, o_ref, wb_ref, h_scr):
    @pl.when(pl.program_id(2) == 0)
    def _():
        _modulated_norm_into(h_scr, x_ref, sh_ref, sc_ref, g_ref)

    w = w_ref[...].astype(BF16)
    wb_ref[...] = w
    o_ref[...] = jnp.dot(h_scr[...], w, preferred_element_type=F32)


def _mod_specs(mod, mod_col, tm):
    if mod.shape[1] == 1:
        return [pl.BlockSpec((None, 1, D_MODEL), lambda b, i, j, c=mod_col + k: (b, 0, c))
                for k in range(2)]
    return [pl.BlockSpec((None, tm, D_MODEL), lambda b, i, j, c=mod_col + k: (b, i, c))
            for k in range(2)]


def _lookup(j, table):
    out = table[-1]
    for k in range(len(table) - 2, -1, -1):
        out = jnp.where(j == k, table[k], out)
    return out


def _norm_matmul(x, mod, mod_col, gain, w, layer, tm, tn):
    G, R, _ = x.shape
    nt = w.shape[-1] // tn
    assert G * (R // tm) == 1
    wb_spec, wb_shape = _wb_out(w, tn)
    out_specs = [pl.BlockSpec((None, tm, tn), lambda b, i, j: (b, i, j)), wb_spec]
    out_shape = [jax.ShapeDtypeStruct((G, R, nt * tn), F32), wb_shape]
    return pl.pallas_call(
        _norm_mm_kernel,
        grid=(G, R // tm, nt),
        in_specs=[pl.BlockSpec((None, tm, D_MODEL), lambda b, i, j: (b, i, 0))]
        + _mod_specs(mod, mod_col, tm)
        + [pl.BlockSpec((1, D_MODEL), lambda b, i, j: (0, 0)), _w_spec(w, layer, tn)],
        out_specs=out_specs,
        out_shape=out_shape,
        scratch_shapes=[pltpu.VMEM((tm, D_MODEL), BF16)],
        compiler_params=_cparams(("parallel", "parallel", "arbitrary")),
        name="norm_in_proj",
    )(x, mod, mod, gain, w)


def _norm_gu_kernel(x_ref, sh_ref, sc_ref, g_ref, wg_ref, wu_ref, o_ref, *rest, emit):
    h_scr = rest[-1]

    @pl.when(pl.program_id(2) == 0)
    def _():
        _modulated_norm_into(h_scr, x_ref, sh_ref, sc_ref, g_ref)

    h = h_scr[...]
    wg = wg_ref[...].astype(BF16)
    wu = wu_ref[...].astype(BF16)
    if emit:
        rest[0][...] = wg
        rest[1][...] = wu
    gate = jnp.dot(h, wg, preferred_element_type=F32)
    up = jnp.dot(h, wu, preferred_element_type=F32)
    o_ref[...] = (_silu(gate) * up).astype(o_ref.dtype)


def _norm_gate_up(x, mod, mod_col, gain, wg, wu, layer, tm, tn):
    G, R, _ = x.shape
    emit = layer is not None
    nt = wg.shape[-1] // tn if emit else wg.shape[0]
    if emit:
        assert G * (R // tm) == 1
        wb_spec, wb_shape = _wb_out(wg, tn)
        out_specs = [pl.BlockSpec((None, tm, tn), lambda b, i, j: (b, i, j)), wb_spec, wb_spec]
        out_shape = [jax.ShapeDtypeStruct((G, R, nt * tn), BF16), wb_shape, wb_shape]
    else:
        out_specs = pl.BlockSpec((None, None, tm, tn), lambda b, i, j: (j, b, i, 0))
        out_shape = jax.ShapeDtypeStruct((nt, G, R, tn), BF16)
    return pl.pallas_call(
        functools.partial(_norm_gu_kernel, emit=emit),
        grid=(G, R // tm, nt),
        in_specs=[pl.BlockSpec((None, tm, D_MODEL), lambda b, i, j: (b, i, 0))]
        + _mod_specs(mod, mod_col, tm)
        + [pl.BlockSpec((1, D_MODEL), lambda b, i, j: (0, 0)),
           _w_spec(wg, layer, tn), _w_spec(wu, layer, tn)],
        out_specs=out_specs,
        out_shape=out_shape,
        scratch_shapes=[pltpu.VMEM((tm, D_MODEL), BF16)],
        compiler_params=_cparams(("parallel", "parallel", "arbitrary")),
        name="norm_gate_up",
    )(x, mod, mod, gain, wg, wu)


def _res_matmul_kernel(*refs, n_a):
    a_refs, w_refs = refs[:n_a], refs[n_a:2 * n_a]
    x_ref, gt_ref, o_ref = refs[2 * n_a:]
    acc = None
    for a_ref, w_ref in zip(a_refs, w_refs):
        slabs = [a_ref[s] for s in range(a_ref.shape[0])] if len(a_ref.shape) == 3 else [a_ref[...]]
        ts = slabs[0].shape[1]
        for s, a in enumerate(slabs):
            part = jnp.dot(a, w_ref[s * ts:(s + 1) * ts, :], preferred_element_type=F32)
            acc = part if acc is None else acc + part
    o_ref[...] = x_ref[...] + gt_ref[...] * acc


def _res_matmul_prompt(a_list, w_list, x, mod, gate_col, tm, name):
    G, R, _ = x.shape
    a_specs, w_specs, ws = [], [], []
    for a, w in zip(a_list, w_list):
        w, row_blk = w if isinstance(w, tuple) else (w, 0)
        if a.ndim == 4:
            S, _, _, ts = a.shape
            a_specs.append(pl.BlockSpec((S, None, tm, ts), lambda b, i, j: (0, b, i, 0)))
            k = S * ts
        else:
            k = a.shape[2]
            a_specs.append(pl.BlockSpec((None, tm, k), lambda b, i, j: (b, i, 0)))
        nt, _, tn = w.shape
        w_specs.append(pl.BlockSpec((None, k, tn), lambda b, i, j, r=row_blk: (j, r, 0)))
        ws.append(w)
    tile = pl.BlockSpec((None, tm, tn), lambda b, i, j: (b, i, j))
    return pl.pallas_call(
        functools.partial(_res_matmul_kernel, n_a=len(a_list)),
        grid=(G, R // tm, nt),
        in_specs=a_specs + w_specs
        + [tile, pl.BlockSpec((None, 1, tn), lambda b, i, j: (b, 0, gate_col * nt + j))],
        out_specs=tile,
        out_shape=jax.ShapeDtypeStruct((G, R, D_MODEL), F32),
        compiler_params=_cparams(("parallel", "parallel", "arbitrary")),
        name=name,
    )(*a_list, *ws, x, mod)


def _mm_res_kernel(a_ref, w_ref, x_ref, gt_ref, o_ref, wb_ref):
    w = w_ref[...].astype(BF16)
    wb_ref[...] = w
    acc = jnp.dot(a_ref[...].astype(BF16), w, preferred_element_type=F32)
    o_ref[...] = x_ref[...] + gt_ref[...] * acc


def _matmul_residual(a, w, layer, x, mod, mod_off, tm, tn):
    G, R, K = a.shape
    assert G * (R // tm) == 1 and mod.shape[1] == R
    gcol = mod_off // tn
    wb_spec, wb_shape = _wb_out(w, tn)
    return pl.pallas_call(
        _mm_res_kernel,
        grid=(G, R // tm, D_MODEL // tn),
        in_specs=[
            pl.BlockSpec((None, tm, K), lambda b, i, j: (b, i, 0)),
            _w_spec(w, layer, tn),
            pl.BlockSpec((None, tm, tn), lambda b, i, j: (b, i, j)),
            pl.BlockSpec((None, tm, tn), lambda b, i, j: (b, i, gcol + j)),
        ],
        out_specs=[pl.BlockSpec((None, tm, tn), lambda b, i, j: (b, i, j)), wb_spec],
        out_shape=[jax.ShapeDtypeStruct((G, R, D_MODEL), F32), wb_shape],
        compiler_params=_cparams(("parallel", "parallel", "arbitrary")),
        name="matmul_residual",
    )(a, w, x, mod)


def _split3_bf16(x):
    hi = x.astype(BF16)
    r1 = x - hi.astype(F32)
    mid = r1.astype(BF16)
    lo = (r1 - mid.astype(F32)).astype(BF16)
    return hi, mid, lo


HGRN_HP = 4
HGRN_SEP_MIN = 32


def _hgrn_consts():
    C, P, D = HGRN_CHUNK, HGRN_SUB, HEAD_DIM
    row = lax.broadcasted_iota(jnp.int32, (C, C), 0)
    col = lax.broadcasted_iota(jnp.int32, (C, C), 1)
    tri = (row >= col).astype(BF16)
    sel_r = lax.broadcasted_iota(jnp.int32, (P * D, C), 0)
    sel_c = lax.broadcasted_iota(jnp.int32, (P * D, C), 1)
    psh = P.bit_length() - 1
    sel = ((sel_c & (P - 1)) == (sel_r >> (D.bit_length() - 1))).astype(BF16)
    keep_diag = ((col >> psh) == (row >> psh)) & ((col & (P - 1)) <= (row & (P - 1)))
    levels = []
    L = C // 2
    while L >= P:
        levels.append(L)
        L //= 2
    keep_level = {}
    for L in levels:
        if L < HGRN_SEP_MIN:
            sh = L.bit_length() - 1
            keep_level[L] = (((row >> sh) & 1) == 1) & ((col >> sh) == (row >> sh) - 1)
    return tri, sel, keep_diag, levels, keep_level


def _hgrn_heads(q_raw, f_raw, i_raw, g_raw, lbs, g_norm, states, consts):
    C, P, D = HGRN_CHUNK, HGRN_SUB, HEAD_DIM
    tri, sel, keep_diag, levels, keep_level = consts
    n = len(q_raw)
    qs, ks, vs, parts = [], [], [], []
    for hh in range(n):
        logf, fp = _log_forget(f_raw[hh], lbs[hh])
        qs.append(_silu(q_raw[hh].astype(F32)))
        ks.append((1.0 - lbs[hh]) - fp)
        vs.append(i_raw[hh].astype(BF16))
        parts.extend(_split3_bf16(logf * LOG2E))
    cum = jnp.dot(tri, jnp.concatenate(parts, axis=1), preferred_element_type=F32)

    outs, new_states = [], []
    for hh in range(n):
        q, k, v16, st = qs[hh], ks[hh], vs[hh], states[hh]
        b = (cum[:, (3 * hh) * D:(3 * hh + 1) * D] + cum[:, (3 * hh + 1) * D:(3 * hh + 2) * D]
             + cum[:, (3 * hh + 2) * D:(3 * hh + 3) * D])
        b_end = b[C - 1:C, :]
        o = lax.dot_general((q * jnp.exp2(b)).astype(BF16), st.astype(BF16),
                            (((1,), (1,)), ((), ())), preferred_element_type=F32)

        q_sep, k_sep, a_masked = [], [], []
        for L in levels:
            q_rows, k_rows = [], []
            for p in range(C // (2 * L)):
                lo, mid, hi = 2 * L * p, 2 * L * p + L, 2 * L * (p + 1)
                ref = b[mid - 1:mid, :]
                zf = jnp.zeros((L, D), F32)
                kk = jnp.concatenate([k[lo:mid] * jnp.exp2(ref - b[lo:mid]), zf], axis=0)
                qq = jnp.concatenate([zf, q[mid:hi] * jnp.exp2(b[mid:hi] - ref)], axis=0)
                q_rows.append(qq.astype(BF16))
                k_rows.append(kk.astype(BF16))
            if L >= HGRN_SEP_MIN:
                for p in range(len(q_rows)):
                    above = jnp.zeros((2 * L * p, D), BF16)
                    below = jnp.zeros((C - 2 * L * (p + 1), D), BF16)
                    for rows_p, dst in ((q_rows[p], q_sep), (k_rows[p], k_sep)):
                        dst.append(jnp.concatenate(
                            [y for y in (above, rows_p, below) if y.shape[0] > 0], axis=0))
            else:
                a_l = lax.dot_general(jnp.concatenate(q_rows, axis=0),
                                      jnp.concatenate(k_rows, axis=0),
                                      (((1,), (1,)), ((), ())), preferred_element_type=F32)
                a_masked.append((keep_level[L], a_l))
        a = lax.dot_general(jnp.concatenate(q_sep, axis=1), jnp.concatenate(k_sep, axis=1),
                            (((1,), (1,)), ((), ())), preferred_element_type=F32)
        for keep, a_l in a_masked:
            a = jnp.where(keep, a_l, a)

        b3 = b.reshape(C // P, P, D)
        k3 = k.reshape(C // P, P, D)
        zs = []
        for s in range(P):
            bs = jnp.broadcast_to(b3[:, s:s + 1, :], (C // P, P, D)).reshape(C, D)
            ksb = jnp.broadcast_to(k3[:, s:s + 1, :], (C // P, P, D)).reshape(C, D)
            e = jnp.exp2(jnp.minimum(b - bs, 0.0))
            zs.append((q * ksb * e).astype(BF16))
        a_d = jnp.dot(jnp.concatenate(zs, axis=1), sel, preferred_element_type=F32)
        a = jnp.where(keep_diag, a_d, a)

        o = o + jnp.dot(a.astype(BF16), v16, preferred_element_type=F32)

        k_end = (k * jnp.exp2(b_end - b)).astype(BF16)
        upd = lax.dot_general(v16, k_end, (((0,), (0,)), ((), ())), preferred_element_type=F32)
        new_states.append(st * jnp.exp2(b_end) + upd)
        outs.append(_rms_rows(o, g_norm) * _silu(g_raw[hh].astype(F32)))
    return outs, new_states


IPH_TM = 512
IPH_SLAB_ORDER = (2, 3, 10, 0, 1, 4, 5, 6, 7, 8, 9)
IPH_N32 = 3
IPH_N16_HGRN = 6


def _in_proj_hgrn_kernel(x_ref, sh_ref, sc_ref, g_ref, w_ref, lb_ref, gn_ref,
                         o32_ref, o16_ref, ol_ref, s_ref,
                         h_scr, ring32, ring16, st_scr, *, n_tiles, tiles_per_seq):
    C, D = HGRN_CHUNK, HEAD_DIM
    W = HGRN_HP * D
    n_hg = N_LIN_HEADS // HGRN_HP
    n_units = (IPH_TM // C) * n_hg
    t = pl.program_id(0)
    j = pl.program_id(1)
    slot = t % 2
    do_mm = t < n_tiles
    do_hg = (t > 0) & (j < n_units)
    consts = _hgrn_consts()
    g_norm = gn_ref[...]

    @pl.when((t == 0) & (j == 0))
    def _():
        st_scr[...] = jnp.zeros_like(st_scr)

    @pl.when(do_mm & (j == 0))
    def _():
        _modulated_norm_into(h_scr, x_ref, sh_ref, sc_ref, g_ref)

    def matmul():
        acc = jnp.dot(h_scr[...], w_ref[...], preferred_element_type=F32)
        a16 = acc.astype(BF16)
        o32_ref[...] = acc
        o16_ref[...] = a16
        ring32[slot, jnp.minimum(j, IPH_N32 - 1)] = acc
        in16 = (j >= IPH_N32) & (j < IPH_N32 + IPH_N16_HGRN)
        ring16[slot, jnp.where(in16, j - IPH_N32, IPH_N16_HGRN)] = a16

    def hgrn_unit():
        hg = j % n_hg
        c = j // n_hg
        pslot = 1 - slot
        rs = pl.ds(pl.multiple_of(c * C, C), C)
        first = (((t - 1) % tiles_per_seq) == 0) & (c == 0)
        heads = range(HGRN_HP)
        lane = lambda hh: slice(hh * D, (hh + 1) * D)
        q_raw = [ring16[pslot, hg, rs, lane(hh)] for hh in heads]
        i_raw = [ring16[pslot, n_hg + hg, rs, lane(hh)] for hh in heads]
        g_raw = [ring16[pslot, 2 * n_hg + hg, rs, lane(hh)] for hh in heads]
        f_raw = [ring32[pslot, hg, rs, lane(hh)] for hh in heads]
        lbs = [lb_ref[:, pl.ds(pl.multiple_of(hg * W + hh * D, D), D)] for hh in heads]
        states = [jnp.where(first, 0.0, st_scr[hg * HGRN_HP + hh]) for hh in heads]
        outs, new_states = _hgrn_heads(q_raw, f_raw, i_raw, g_raw, lbs, g_norm, states, consts)
        for hh in heads:
            st_scr[hg * HGRN_HP + hh] = new_states[hh]
        ol_ref[rs, pl.ds(pl.multiple_of(hg * W, W), W)] = jnp.concatenate(
            outs, axis=1).astype(ol_ref.dtype)

    @pl.when(do_mm & do_hg)
    def _():
        hgrn_unit()
        matmul()

    @pl.when(do_mm & jnp.logical_not(do_hg))
    def _():
        matmul()

    @pl.when(jnp.logical_not(do_mm))
    def _():
        o32_ref[...] = jnp.zeros_like(o32_ref)
        o16_ref[...] = jnp.zeros_like(o16_ref)

    @pl.when(jnp.logical_not(do_mm) & do_hg)
    def _():
        hgrn_unit()

    last_chunk = (j // n_hg) == IPH_TM // C - 1
    last_tile = ((t - 1) % tiles_per_seq) == tiles_per_seq - 1

    @pl.when(do_hg & last_chunk & last_tile)
    def _():
        hg = j % n_hg
        for hh in range(HGRN_HP):
            s_ref[hg * HGRN_HP + hh] = st_scr[hg * HGRN_HP + hh].T


def _in_proj_hgrn_prompt(x, mod, gain, wt, lb, g_norm):
    B, T, _ = x.shape
    nt, _, tn = wt.shape
    tm = IPH_TM
    tps = T // tm
    n_tiles = B * tps
    assert nt == len(IPH_SLAB_ORDER) and tn == HGRN_HP * HEAD_DIM
    assert (tm // HGRN_CHUNK) * (N_LIN_HEADS // HGRN_HP) <= nt

    cur = lambda t: jnp.minimum(t, n_tiles - 1)
    prev = lambda t: jnp.maximum(t - 1, 0)
    n_qa = nt - IPH_N32 - IPH_N16_HGRN
    first_qa = IPH_N32 + IPH_N16_HGRN

    def o32_slab(t, j):
        return jnp.where((t < n_tiles) & (j == IPH_N32 - 1), 0, 1)

    def o16_slab(t, j):
        return jnp.where((t < n_tiles) & (j >= first_qa), j - first_qa, n_qa)

    return pl.pallas_call(
        functools.partial(_in_proj_hgrn_kernel, n_tiles=n_tiles, tiles_per_seq=tps),
        grid=(n_tiles + 1, nt),
        in_specs=[
            pl.BlockSpec((None, tm, D_MODEL), lambda t, j: (cur(t) // tps, cur(t) % tps, 0)),
            pl.BlockSpec((None, 1, D_MODEL), lambda t, j: (cur(t) // tps, 0, 0)),
            pl.BlockSpec((None, 1, D_MODEL), lambda t, j: (cur(t) // tps, 0, 1)),
            pl.BlockSpec((1, D_MODEL), lambda t, j: (0, 0)),
            pl.BlockSpec((None, D_MODEL, tn), lambda t, j: (_lookup(j, IPH_SLAB_ORDER), 0, 0)),
            pl.BlockSpec((1, LIN_WIDTH), lambda t, j: (0, 0)),
            pl.BlockSpec((1, HEAD_DIM), lambda t, j: (0, 0)),
        ],
        out_specs=[
            pl.BlockSpec((None, None, tm, tn),
                         lambda t, j: (o32_slab(t, j), cur(t) // tps, cur(t) % tps, 0)),
            pl.BlockSpec((None, None, tm, tn),
                         lambda t, j: (o16_slab(t, j), cur(t) // tps, cur(t) % tps, 0)),
            pl.BlockSpec((None, tm, LIN_WIDTH), lambda t, j: (prev(t) // tps, prev(t) % tps, 0)),
            pl.BlockSpec((None, N_LIN_HEADS, HEAD_DIM, HEAD_DIM),
                         lambda t, j: (prev(t) // tps, 0, 0, 0)),
        ],
        out_shape=[
            jax.ShapeDtypeStruct((2, B, T, tn), F32),
            jax.ShapeDtypeStruct((n_qa + 1, B, T, tn), BF16),
            jax.ShapeDtypeStruct((B, T, LIN_WIDTH), BF16),
            jax.ShapeDtypeStruct((B, N_LIN_HEADS, HEAD_DIM, HEAD_DIM), F32),
        ],
        scratch_shapes=[
            pltpu.VMEM((tm, D_MODEL), BF16),
            pltpu.VMEM((2, IPH_N32, tm, tn), F32),
            pltpu.VMEM((2, IPH_N16_HGRN + 1, tm, tn), BF16),
            pltpu.VMEM((N_LIN_HEADS, HEAD_DIM, HEAD_DIM), F32),
        ],
        compiler_params=_cparams(("arbitrary", "arbitrary")),
        name="in_proj_hgrn",
    )(x, mod, mod, gain, wt, lb, g_norm)


def _attn_prompt_kernel(sink_ref, q_ref, kvc_ref, kvp_ref, qn_ref, kn_ref,
                        o_ref, nk_ref, nv_ref):
    n = pl.program_id(1)
    W = WINDOW
    qi = lax.broadcasted_iota(jnp.int32, (W, 2 * W), 0)
    kj = lax.broadcasted_iota(jnp.int32, (W, 2 * W), 1)
    dist = W + qi - kj
    valid = (dist >= 0) & (dist <= W) & ((n > 0) | (kj >= W))
    dist_m = jnp.where(valid, dist.astype(F32), jnp.inf)
    q_gain = qn_ref[...] * (ATT_SCALE * LOG2E)
    k_norm = kn_ref[...]

    new_k = []
    for h in range(N_KV_HEADS):
        hs = slice(h * HEAD_DIM, (h + 1) * HEAD_DIM)
        vs = slice(KV_WIDTH + h * HEAD_DIM, KV_WIDTH + (h + 1) * HEAD_DIM)
        kc = _rms_rows(kvc_ref[:, hs], k_norm)
        kp = _rms_rows(kvp_ref[:, hs], k_norm)
        new_k.append(kc)
        k2 = jnp.concatenate([kp, kc], axis=0).astype(BF16)
        v2 = jnp.concatenate([kvp_ref[:, vs], kvc_ref[:, vs]], axis=0).astype(BF16)

        for g in range(GROUP):
            a = h * GROUP + g
            cs = slice(a * HEAD_DIM, (a + 1) * HEAD_DIM)
            qh = _rms_rows(q_ref[h, :, g * HEAD_DIM:(g + 1) * HEAD_DIM].astype(F32),
                           q_gain).astype(BF16)
            s = lax.dot_general(qh, k2, (((1,), (1,)), ((), ())), preferred_element_type=F32)
            s = s + (-ALIBI_SLOPES[a] * LOG2E) * dist_m
            sink = sink_ref[a] * LOG2E
            m = jnp.maximum(jnp.max(s, axis=-1, keepdims=True), sink)
            p = jnp.exp2(s - m)
            den = jnp.sum(p, axis=-1, keepdims=True) + jnp.exp2(sink - m)
            o = jnp.dot(p.astype(BF16), v2, preferred_element_type=F32) * (1.0 / den)
            o_ref[:, cs] = o.astype(o_ref.dtype)

    @pl.when(n == pl.num_programs(1) - 1)
    def _():
        for h in range(N_KV_HEADS):
            hs = slice(h * HEAD_DIM, (h + 1) * HEAD_DIM)
            nk_ref[:, hs] = new_k[h]
            nv_ref[:, hs] = kvc_ref[:, KV_WIDTH + h * HEAD_DIM:KV_WIDTH + (h + 1) * HEAD_DIM]


def _attn_prompt(proj32, proj16, sinks, q_norm, k_norm):
    _, B, T, W = proj32.shape
    assert W == GROUP * HEAD_DIM == 2 * KV_WIDTH
    nb = T // WINDOW
    q_slab = 0
    kv_slab = 0
    prev = lambda n: jnp.maximum(n - 1, 0)
    return pl.pallas_call(
        _attn_prompt_kernel,
        grid=(B, nb),
        in_specs=[
            pl.BlockSpec(memory_space=pltpu.SMEM),
            pl.BlockSpec((N_KV_HEADS, None, WINDOW, W), lambda b, n: (q_slab, b, n, 0)),
            pl.BlockSpec((None, None, WINDOW, W), lambda b, n: (kv_slab, b, n, 0)),
            pl.BlockSpec((None, None, WINDOW, W), lambda b, n: (kv_slab, b, prev(n), 0)),
            pl.BlockSpec((1, HEAD_DIM), lambda b, n: (0, 0)),
            pl.BlockSpec((1, HEAD_DIM), lambda b, n: (0, 0)),
        ],
        out_specs=(
            pl.BlockSpec((None, WINDOW, ATT_WIDTH), lambda b, n: (b, n, 0)),
            pl.BlockSpec((None, WINDOW, KV_WIDTH), lambda b, n: (b, 0, 0)),
            pl.BlockSpec((None, WINDOW, KV_WIDTH), lambda b, n: (b, 0, 0)),
        ),
        out_shape=(
            jax.ShapeDtypeStruct((B, T, ATT_WIDTH), BF16),
            jax.ShapeDtypeStruct((B, WINDOW, KV_WIDTH), F32),
            jax.ShapeDtypeStruct((B, WINDOW, KV_WIDTH), F32),
        ),
        compiler_params=_cparams(("parallel", "arbitrary")),
        name="attn_prompt",
    )(sinks, proj16, proj32, proj32, q_norm, k_norm)


def _row_to_col(x_row, eye):
    return jnp.sum(jnp.where(eye, x_row, 0.0), axis=1, keepdims=True)


def _mix_sample_kernel(sink_ref, p_ref, s_ref, ck_ref, cv_ref, lb_ref,
                       gn_ref, qn_ref, kn_ref, o_ref, ns_ref, nk_ref, nv_ref):
    W = WINDOW
    er = lax.broadcasted_iota(jnp.int32, (HEAD_DIM, HEAD_DIM), 0)
    ec = lax.broadcasted_iota(jnp.int32, (HEAD_DIM, HEAD_DIM), 1)
    eye = er == ec
    g_norm = gn_ref[...]
    q_norm = qn_ref[...]
    k_norm = kn_ref[...]

    def cols(off, width=HEAD_DIM):
        return p_ref[:, off:off + width]

    for h in range(N_LIN_HEADS):
        c0 = h * HEAD_DIM
        q = _silu(cols(c0))
        lb = lb_ref[:, c0:c0 + HEAD_DIM]
        _, fp = _forget_gate(cols(LIN_WIDTH + c0), lb)
        f_col = _row_to_col(lb + fp, eye)
        k_col = 1.0 - f_col
        v = cols(2 * LIN_WIDTH + c0)
        s_new = s_ref[h] * f_col + k_col * v
        ns_ref[h] = s_new
        q8 = jnp.broadcast_to(q, (8, HEAD_DIM)).astype(BF16)
        o = jnp.dot(q8, s_new.astype(BF16), preferred_element_type=F32)[0:1, :]
        o = _rms_rows(o, g_norm) * _silu(cols(3 * LIN_WIDTH + c0))
        o_ref[:, c0:c0 + HEAD_DIM] = o

    row8 = lax.broadcasted_iota(jnp.int32, (8, 1), 0)
    row8_k = lax.broadcasted_iota(jnp.int32, (8, HEAD_DIM), 0)
    lane = lax.broadcasted_iota(jnp.int32, (8, W), 1)
    dist_c = (W - lane).astype(F32)
    rows_w = lax.broadcasted_iota(jnp.int32, (W, HEAD_DIM), 0)
    for h in range(N_KV_HEADS):
        hs = slice(h * HEAD_DIM, (h + 1) * HEAD_DIM)
        k_new = _rms_rows(cols(OFF_K_A + h * HEAD_DIM), k_norm)
        v_new = cols(OFF_V_A + h * HEAD_DIM)
        kc = ck_ref[:, hs]
        vc = cv_ref[:, hs]
        nk_ref[:, hs] = jnp.where(rows_w == W - 1, k_new, pltpu.roll(kc, W - 1, 0))
        nv_ref[:, hs] = jnp.where(rows_w == W - 1, v_new, pltpu.roll(vc, W - 1, 0))

        q4 = jnp.zeros((8, HEAD_DIM), F32)
        slope = jnp.zeros((8, 1), F32)
        sink = jnp.zeros((8, 1), F32)
        for g in range(GROUP):
            a = h * GROUP + g
            qg = _rms_rows(cols(OFF_Q_A + a * HEAD_DIM), q_norm)
            q4 = jnp.where(row8_k == g, qg, q4)
            slope = jnp.where(row8 == g, ALIBI_SLOPES[a], slope)
            sink = jnp.where(row8 == g, sink_ref[a], sink)
        q4b = q4.astype(BF16)
        s_c = lax.dot_general(q4b, kc.astype(BF16), (((1,), (1,)), ((), ())),
                              preferred_element_type=F32)
        s_c = s_c * ATT_SCALE - slope * dist_c
        s_n = jnp.sum(q4 * k_new, axis=-1, keepdims=True) * ATT_SCALE
        m = jnp.maximum(jnp.maximum(jnp.max(s_c, axis=-1, keepdims=True), s_n), sink)
        p_c = jnp.exp(s_c - m)
        p_n = jnp.exp(s_n - m)
        den = jnp.sum(p_c, axis=-1, keepdims=True) + p_n + jnp.exp(sink - m)
        o = jnp.dot(p_c.astype(BF16), vc.astype(BF16), preferred_element_type=F32)
        o = (o + p_n * v_new) / den
        for g in range(GROUP):
            a = h * GROUP + g
            o_ref[:, LIN_WIDTH + a * HEAD_DIM:LIN_WIDTH + (a + 1) * HEAD_DIM] = o[g:g + 1, :]


def _mix_sample(proj, state, cache_k, cache_v, layer, sinks, lb, g_norm, q_norm, k_norm):
    B = proj.shape[0]
    vec = pl.BlockSpec((1, HEAD_DIM), lambda b: (0, 0))
    lbs = pl.BlockSpec((1, LIN_WIDTH), lambda b: (0, 0))
    cache_in = pl.BlockSpec((None, None, WINDOW, KV_WIDTH), lambda b: (layer, b, 0, 0))
    st_in = pl.BlockSpec((None, None, N_LIN_HEADS, HEAD_DIM, HEAD_DIM),
                         lambda b: (layer, b, 0, 0, 0))
    cache = pl.BlockSpec((None, WINDOW, KV_WIDTH), lambda b: (b, 0, 0))
    st = pl.BlockSpec((None, N_LIN_HEADS, HEAD_DIM, HEAD_DIM), lambda b: (b, 0, 0, 0))
    return pl.pallas_call(
        _mix_sample_kernel,
        grid=(B,),
        in_specs=[
            pl.BlockSpec(memory_space=pltpu.SMEM),
            pl.BlockSpec((None, 1, IN_WIDTH), lambda b: (b, 0, 0)),
            st_in, cache_in, cache_in, lbs, vec, vec, vec,
        ],
        out_specs=(
            pl.BlockSpec((None, 1, MIX_WIDTH), lambda b: (b, 0, 0)),
            st, cache, cache,
        ),
        out_shape=(
            jax.ShapeDtypeStruct((B, 1, MIX_WIDTH), F32),
            jax.ShapeDtypeStruct(state.shape[1:], F32),
            jax.ShapeDtypeStruct((B, WINDOW, KV_WIDTH), F32),
            jax.ShapeDtypeStruct((B, WINDOW, KV_WIDTH), F32),
        ),
        compiler_params=_cparams(("parallel",)),
        name="mix_sample",
    )(sinks, proj, state, cache_k, cache_v, lb, g_norm, q_norm, k_norm)


TILE_N = 512
P_TM = 1024


def kernel(x_prompt, x_sample, cache_k, cache_v, state_hgrn, c_prompt, c_sample, lower_bounds,
           w_ada, b_ada, norm_mix, w_in, q_norm, k_norm, attn_sinks, g_norm, w_out, norm_ffn,
           w_gate, w_up, w_down):
    B, T, _ = x_prompt.shape
    BS = x_sample.shape[0]

    lb_all = _lower_bounds(lower_bounds)

    c_all = jnp.concatenate(
        [c_prompt, c_sample, jnp.zeros((MOD_ROWS - B - BS, D_MODEL), F32)], axis=0)
    mod = _modulation(c_all, w_ada, b_ada)

    cache_k2 = cache_k.reshape(DEPTH, BS, WINDOW, KV_WIDTH)
    cache_v2 = cache_v.reshape(DEPTH, BS, WINDOW, KV_WIDTH)

    yp = x_prompt
    ys = x_sample.reshape(1, BS, D_MODEL)
    pk, pv, ps, sk, sv, ss = [], [], [], [], [], []
    for l in range(DEPTH):
        mod_p = mod[l, :B].reshape(B, 1, 6 * D_MODEL)
        mod_s = mod[l, B:B + BS].reshape(1, BS, 6 * D_MODEL)
        gain_mix = norm_mix[l].reshape(1, D_MODEL)
        gain_ffn = norm_ffn[l].reshape(1, D_MODEL)
        lb = lb_all[l].reshape(1, LIN_WIDTH)
        gn = g_norm[l].reshape(1, HEAD_DIM)
        qn = q_norm[l].reshape(1, HEAD_DIM)
        kn = k_norm[l].reshape(1, HEAD_DIM)
        sinks = attn_sinks[l].astype(F32)

        proj_s, w_in_b = _norm_matmul(ys, mod_s, 0, gain_mix, w_in, l, BS, TILE_N)
        mixed_s, s_s, k_s, v_s = _mix_sample(
            proj_s.reshape(BS, 1, IN_WIDTH), state_hgrn, cache_k2, cache_v2, l,
            sinks, lb, gn, qn, kn)
        ys, w_out_b = _matmul_residual(mixed_s.reshape(1, BS, MIX_WIDTH), w_out, l, ys, mod_s,
                                       2 * D_MODEL, BS, TILE_N)
        ff_s, w_gate_b, w_up_b = _norm_gate_up(ys, mod_s, 3, gain_ffn, w_gate, w_up, l, BS, TILE_N)
        ys, w_down_b = _matmul_residual(ff_s, w_down, l, ys, mod_s, 5 * D_MODEL, BS, TILE_N)
        sk.append(k_s.reshape(BS, WINDOW, N_KV_HEADS, HEAD_DIM))
        sv.append(v_s.reshape(BS, WINDOW, N_KV_HEADS, HEAD_DIM))
        ss.append(s_s)

        proj32, proj16, o_l, s_p = _in_proj_hgrn_prompt(yp, mod_p, gain_mix, w_in_b, lb, gn)
        o_a, k_p, v_p = _attn_prompt(proj32, proj16, sinks, qn, kn)
        yp = _res_matmul_prompt([o_l, o_a], [(w_out_b, 0), (w_out_b, 1)], yp, mod_p, 2, P_TM,
                                "out_proj")
        ff = _norm_gate_up(yp, mod_p, 3, gain_ffn, w_gate_b, w_up_b, None, P_TM, TILE_N)
        yp = _res_matmul_prompt([ff], [w_down_b], yp, mod_p, 5, P_TM, "down_proj")
        pk.append(k_p.reshape(B, WINDOW, N_KV_HEADS, HEAD_DIM))
        pv.append(v_p.reshape(B, WINDOW, N_KV_HEADS, HEAD_DIM))
        ps.append(s_p)

    return (yp, ys.reshape(BS, 1, D_MODEL), jnp.stack(pk), jnp.stack(pv), jnp.stack(ps),
            jnp.stack(sk), jnp.stack(sv), jnp.stack(ss))
```

```python
import functools
import math

import jax
import jax.numpy as jnp
from jax import lax
from jax.experimental import pallas as pl
from jax.experimental.pallas import tpu as pltpu

F32 = jnp.float32
BF16 = jnp.bfloat16

D_MODEL = 2048
DEPTH = 4
HEAD_DIM = 128
N_LIN_HEADS = 8
LIN_WIDTH = N_LIN_HEADS * HEAD_DIM
N_ATT_HEADS = 8
N_KV_HEADS = 2
GROUP = N_ATT_HEADS // N_KV_HEADS
ATT_WIDTH = N_ATT_HEADS * HEAD_DIM
KV_WIDTH = N_KV_HEADS * HEAD_DIM
MIX_WIDTH = LIN_WIDTH + ATT_WIDTH
WINDOW = 128
D_FF = 5632
IN_WIDTH = 4 * LIN_WIDTH + ATT_WIDTH + 2 * KV_WIDTH
EPS = 1e-6
ATT_SCALE = HEAD_DIM ** -0.5
ALIBI_SLOPES = tuple(2.0 ** (-8.0 * (a + 1) / N_ATT_HEADS) for a in range(N_ATT_HEADS))

OFF_Q_A = 4 * LIN_WIDTH
OFF_K_A = OFF_Q_A + ATT_WIDTH
OFF_V_A = OFF_K_A + KV_WIDTH

MOD_ROWS = 40
HGRN_CHUNK = 128
HGRN_SUB = 8
LOG2E = math.log2(math.e)

VMEM_LIMIT = 56 * 1024 * 1024


def _cparams(sem):
    return pltpu.CompilerParams(dimension_semantics=sem, vmem_limit_bytes=VMEM_LIMIT)


def _silu(x):
    return x / (1.0 + jnp.exp(-x))


def _rms_rows(x, g):
    ms = jnp.mean(x * x, axis=-1, keepdims=True)
    return x * lax.rsqrt(ms + EPS) * g


def _forget_gate(f_raw, lb):
    t = jnp.exp(-jnp.abs(f_raw))
    inv = 1.0 / (1.0 + t)
    sig = jnp.where(f_raw >= 0.0, inv, t * inv)
    return t, (1.0 - lb) * sig


def _log_forget(f_raw, lb):
    t, fp = _forget_gate(f_raw, lb)
    log_sig = jnp.minimum(f_raw, 0.0) - jnp.log(1.0 + t)
    return jnp.where(lb > 0.0, jnp.log(lb + fp), log_sig), fp


def _lb_kernel(lb_ref, o_ref):
    x = lb_ref[...]
    m = jnp.max(x, axis=0, keepdims=True)
    e = jnp.exp(x - m)
    p = e / jnp.sum(e, axis=0, keepdims=True)
    acc = jnp.zeros_like(p[0:1])
    for l in range(DEPTH):
        if l > 0:
            acc = acc + p[l:l + 1]
        o_ref[l:l + 1, :] = acc


def _lower_bounds(lower_bounds):
    return pl.pallas_call(
        _lb_kernel,
        out_shape=jax.ShapeDtypeStruct((DEPTH, LIN_WIDTH), F32),
        name="lower_bounds",
    )(lower_bounds.astype(F32))


ADA_TN = 1024


def _ada_kernel(c_ref, w_ref, b_ref, o_ref):
    s = _silu(c_ref[...]).astype(BF16)
    acc = jnp.dot(s, w_ref[...].astype(BF16), preferred_element_type=F32)
    o_ref[...] = acc + b_ref[...]


def _modulation(c_all, w_ada, b_ada):
    n = 6 * D_MODEL
    return pl.pallas_call(
        _ada_kernel,
        grid=(DEPTH, n // ADA_TN),
        in_specs=[
            pl.BlockSpec((MOD_ROWS, D_MODEL), lambda l, j: (0, 0)),
            pl.BlockSpec((None, D_MODEL, ADA_TN), lambda l, j: (l, 0, j)),
            pl.BlockSpec((None, 1, ADA_TN), lambda l, j: (l, 0, j)),
        ],
        out_specs=pl.BlockSpec((None, MOD_ROWS, ADA_TN), lambda l, j: (l, 0, j)),
        out_shape=jax.ShapeDtypeStruct((DEPTH, MOD_ROWS, n), F32),
        compiler_params=_cparams(("parallel", "parallel")),
        name="modulation",
    )(c_all, w_ada, b_ada.reshape(DEPTH, 1, n))


NORM_ROWS = 16
NORM_UNROLL = 4


def _modulated_norm_into(h_scr, x_ref, sh_ref, sc_ref, g_ref):
    rows = x_ref.shape[0]
    ch = min(rows, NORM_ROWS)
    per_row = sc_ref.shape[0] != 1
    gain = g_ref[...]
    if not per_row:
        gain = gain * (1.0 + sc_ref[...])
        shift = sh_ref[...]

    def body(c, carry):
        rs = pl.ds(pl.multiple_of(c * ch, ch), ch)
        x = x_ref[rs, :]
        inv = lax.rsqrt(jnp.mean(x * x, axis=-1, keepdims=True) + EPS)
        if per_row:
            h = (x * inv * gain) * (1.0 + sc_ref[rs, :]) + sh_ref[rs, :]
        else:
            h = (x * inv) * gain + shift
        h_scr[rs, :] = h.astype(BF16)
        return carry

    lax.fori_loop(0, rows // ch, body, 0, unroll=min(NORM_UNROLL, rows // ch))


def _w_spec(w, layer, tn):
    return pl.BlockSpec((None, w.shape[1], tn), lambda b, i, j: (layer, 0, j))


def _wb_out(w, tn):
    rows, n = w.shape[-2:]
    return (pl.BlockSpec((None, rows, tn), lambda b, i, j: (j, 0, 0)),
            jax.ShapeDtypeStruct((n // tn, rows, tn), BF16))


def _norm_mm_kernel(x_ref, sh_ref, sc_ref, g_ref, w_ref, o_ref, wb_ref, h_scr):
    @pl.when(pl.program_id(2) == 0)
    def _():
        _modulated_norm_into(h_scr, x_ref, sh_ref, sc_ref, g_ref)

    w = w_ref[...].astype(BF16)
    wb_ref[...] = w
    o_ref[...] = jnp.dot(h_scr[...], w, preferred_element_type=F32)


def _mod_specs(mod, mod_col, tm):
    if mod.shape[1] == 1:
        return [pl.BlockSpec((None, 1, D_MODEL), lambda b, i, j, c=mod_col + k: (b, 0, c))
                for k in range(2)]
    return [pl.BlockSpec((None, tm, D_MODEL), lambda b, i, j, c=mod_col + k: (b, i, c))
            for k in range(2)]


def _lookup(j, table):
    out = table[-1]
    for k in range(len(table) - 2, -1, -1):
        out = jnp.where(j == k, table[k], out)
    return out


def _norm_matmul(x, mod, mod_col, gain, w, layer, tm, tn):
    G, R, _ = x.shape
    nt = w.shape[-1] // tn
    assert G * (R // tm) == 1
    wb_spec, wb_shape = _wb_out(w, tn)
    out_specs = [pl.BlockSpec((None, tm, tn), lambda b, i, j: (b, i, j)), wb_spec]
    out_shape = [jax.ShapeDtypeStruct((G, R, nt * tn), F32), wb_shape]
    return pl.pallas_call(
        _norm_mm_kernel,
        grid=(G, R // tm, nt),
        in_specs=[pl.BlockSpec((None, tm, D_MODEL), lambda b, i, j: (b, i, 0))]
        + _mod_specs(mod, mod_col, tm)
        + [pl.BlockSpec((1, D_MODEL), lambda b, i, j: (0, 0)), _w_spec(w, layer, tn)],
        out_specs=out_specs,
        out_shape=out_shape,
        scratch_shapes=[pltpu.VMEM((tm, D_MODEL), BF16)],
        compiler_params=_cparams(("parallel", "parallel", "arbitrary")),
        name="norm_in_proj",
    )(x, mod, mod, gain, w)


def _norm_gu_kernel(x_ref, sh_ref, sc_ref, g_ref, wg_ref, wu_ref, o_ref, wgb_ref, wub_ref, h_scr):
    @pl.when(pl.program_id(2) == 0)
    def _():
        _modulated_norm_into(h_scr, x_ref, sh_ref, sc_ref, g_ref)

    h = h_scr[...]
    wg = wg_ref[...].astype(BF16)
    wu = wu_ref[...].astype(BF16)
    wgb_ref[...] = wg
    wub_ref[...] = wu
    gate = jnp.dot(h, wg, preferred_element_type=F32)
    up = jnp.dot(h, wu, preferred_element_type=F32)
    o_ref[...] = (_silu(gate) * up).astype(o_ref.dtype)


def _norm_gate_up(x, mod, mod_col, gain, wg, wu, layer, tm, tn):
    G, R, _ = x.shape
    nt = wg.shape[-1] // tn
    assert G * (R // tm) == 1
    wb_spec, wb_shape = _wb_out(wg, tn)
    return pl.pallas_call(
        _norm_gu_kernel,
        grid=(G, R // tm, nt),
        in_specs=[pl.BlockSpec((None, tm, D_MODEL), lambda b, i, j: (b, i, 0))]
        + _mod_specs(mod, mod_col, tm)
        + [pl.BlockSpec((1, D_MODEL), lambda b, i, j: (0, 0)),
           _w_spec(wg, layer, tn), _w_spec(wu, layer, tn)],
        out_specs=[pl.BlockSpec((None, tm, tn), lambda b, i, j: (b, i, j)), wb_spec, wb_spec],
        out_shape=[jax.ShapeDtypeStruct((G, R, nt * tn), BF16), wb_shape, wb_shape],
        scratch_shapes=[pltpu.VMEM((tm, D_MODEL), BF16)],
        compiler_params=_cparams(("parallel", "parallel", "arbitrary")),
        name="norm_gate_up",
    )(x, mod, mod, gain, wg, wu)


def _res_matmul_kernel(*refs, n_a):
    a_refs, w_refs = refs[:n_a], refs[n_a:2 * n_a]
    x_ref, gt_ref, o_ref = refs[2 * n_a:]
    acc = None
    for a_ref, w_ref in zip(a_refs, w_refs):
        slabs = [a_ref[s] for s in range(a_ref.shape[0])] if len(a_ref.shape) == 3 else [a_ref[...]]
        ts = slabs[0].shape[1]
        for s, a in enumerate(slabs):
            part = jnp.dot(a, w_ref[s * ts:(s + 1) * ts, :], preferred_element_type=F32)
            acc = part if acc is None else acc + part
    o_ref[...] = x_ref[...] + gt_ref[...] * acc


def _res_matmul_prompt(a_list, w_list, x, mod, gate_col, tm, name):
    G, R, _ = x.shape
    a_specs, w_specs, ws = [], [], []
    for a, w in zip(a_list, w_list):
        w, row_blk = w if isinstance(w, tuple) else (w, 0)
        if a.ndim == 4:
            S, _, _, ts = a.shape
            a_specs.append(pl.BlockSpec((S, None, tm, ts), lambda b, i, j: (0, b, i, 0)))
            k = S * ts
        else:
            k = a.shape[2]
            a_specs.append(pl.BlockSpec((None, tm, k), lambda b, i, j: (b, i, 0)))
        nt, _, tn = w.shape
        w_specs.append(pl.BlockSpec((None, k, tn), lambda b, i, j, r=row_blk: (j, r, 0)))
        ws.append(w)
    tile = pl.BlockSpec((None, tm, tn), lambda b, i, j: (b, i, j))
    return pl.pallas_call(
        functools.partial(_res_matmul_kernel, n_a=len(a_list)),
        grid=(G, R // tm, nt),
        in_specs=a_specs + w_specs
        + [tile, pl.BlockSpec((None, 1, tn), lambda b, i, j: (b, 0, gate_col * nt + j))],
        out_specs=tile,
        out_shape=jax.ShapeDtypeStruct((G, R, D_MODEL), F32),
        compiler_params=_cparams(("parallel", "parallel", "arbitrary")),
        name=name,
    )(*a_list, *ws, x, mod)


def _mm_res_kernel(a_ref, w_ref, x_ref, gt_ref, o_ref, wb_ref):
    w = w_ref[...].astype(BF16)
    wb_ref[...] = w
    acc = jnp.dot(a_ref[...].astype(BF16), w, preferred_element_type=F32)
    o_ref[...] = x_ref[...] + gt_ref[...] * acc


def _matmul_residual(a, w, layer, x, mod, mod_off, tm, tn):
    G, R, K = a.shape
    assert G * (R // tm) == 1 and mod.shape[1] == R
    gcol = mod_off // tn
    wb_spec, wb_shape = _wb_out(w, tn)
    return pl.pallas_call(
        _mm_res_kernel,
        grid=(G, R // tm, D_MODEL // tn),
        in_specs=[
            pl.BlockSpec((None, tm, K), lambda b, i, j: (b, i, 0)),
            _w_spec(w, layer, tn),
            pl.BlockSpec((None, tm, tn), lambda b, i, j: (b, i, j)),
            pl.BlockSpec((None, tm, tn), lambda b, i, j: (b, i, gcol + j)),
        ],
        out_specs=[pl.BlockSpec((None, tm, tn), lambda b, i, j: (b, i, j)), wb_spec],
        out_shape=[jax.ShapeDtypeStruct((G, R, D_MODEL), F32), wb_shape],
        compiler_params=_cparams(("parallel", "parallel", "arbitrary")),
        name="matmul_residual",
    )(a, w, x, mod)


def _split3_bf16(x):
    hi = x.astype(BF16)
    r1 = x - hi.astype(F32)
    mid = r1.astype(BF16)
    lo = (r1 - mid.astype(F32)).astype(BF16)
    return hi, mid, lo


HGRN_HP = 4
HGRN_SEP_MIN = 32


def _hgrn_consts():
    C, P, D = HGRN_CHUNK, HGRN_SUB, HEAD_DIM
    row = lax.broadcasted_iota(jnp.int32, (C, C), 0)
    col = lax.broadcasted_iota(jnp.int32, (C, C), 1)
    tri = (row >= col).astype(BF16)
    sel_r = lax.broadcasted_iota(jnp.int32, (P * D, C), 0)
    sel_c = lax.broadcasted_iota(jnp.int32, (P * D, C), 1)
    psh = P.bit_length() - 1
    sel = ((sel_c & (P - 1)) == (sel_r >> (D.bit_length() - 1))).astype(BF16)
    keep_diag = ((col >> psh) == (row >> psh)) & ((col & (P - 1)) <= (row & (P - 1)))
    levels = []
    L = C // 2
    while L >= P:
        levels.append(L)
        L //= 2
    keep_level = {}
    for L in levels:
        if L < HGRN_SEP_MIN:
            sh = L.bit_length() - 1
            keep_level[L] = (((row >> sh) & 1) == 1) & ((col >> sh) == (row >> sh) - 1)
    return tri, sel, keep_diag, levels, keep_level


def _hgrn_heads(q_raw, f_raw, i_raw, g_raw, lbs, g_norm, states, consts):
    C, P, D = HGRN_CHUNK, HGRN_SUB, HEAD_DIM
    tri, sel, keep_diag, levels, keep_level = consts
    n = len(q_raw)
    qs, ks, vs, parts = [], [], [], []
    for hh in range(n):
        logf, fp = _log_forget(f_raw[hh], lbs[hh])
        qs.append(_silu(q_raw[hh].astype(F32)))
        ks.append((1.0 - lbs[hh]) - fp)
        vs.append(i_raw[hh].astype(BF16))
        parts.extend(_split3_bf16(logf * LOG2E))
    cum = jnp.dot(tri, jnp.concatenate(parts, axis=1), preferred_element_type=F32)

    outs, new_states = [], []
    for hh in range(n):
        q, k, v16, st = qs[hh], ks[hh], vs[hh], states[hh]
        b = (cum[:, (3 * hh) * D:(3 * hh + 1) * D] + cum[:, (3 * hh + 1) * D:(3 * hh + 2) * D]
             + cum[:, (3 * hh + 2) * D:(3 * hh + 3) * D])
        b_end = b[C - 1:C, :]
        o = lax.dot_general((q * jnp.exp2(b)).astype(BF16), st.astype(BF16),
                            (((1,), (1,)), ((), ())), preferred_element_type=F32)

        q_sep, k_sep, a_masked = [], [], []
        for L in levels:
            q_rows, k_rows = [], []
            for p in range(C // (2 * L)):
                lo, mid, hi = 2 * L * p, 2 * L * p + L, 2 * L * (p + 1)
                ref = b[mid - 1:mid, :]
                zf = jnp.zeros((L, D), F32)
                kk = jnp.concatenate([k[lo:mid] * jnp.exp2(ref - b[lo:mid]), zf], axis=0)
                qq = jnp.concatenate([zf, q[mid:hi] * jnp.exp2(b[mid:hi] - ref)], axis=0)
                q_rows.append(qq.astype(BF16))
                k_rows.append(kk.astype(BF16))
            if L >= HGRN_SEP_MIN:
                for p in range(len(q_rows)):
                    above = jnp.zeros((2 * L * p, D), BF16)
                    below = jnp.zeros((C - 2 * L * (p + 1), D), BF16)
                    for rows_p, dst in ((q_rows[p], q_sep), (k_rows[p], k_sep)):
                        dst.append(jnp.concatenate(
                            [y for y in (above, rows_p, below) if y.shape[0] > 0], axis=0))
            else:
                a_l = lax.dot_general(jnp.concatenate(q_rows, axis=0),
                                      jnp.concatenate(k_rows, axis=0),
                                      (((1,), (1,)), ((), ())), preferred_element_type=F32)
                a_masked.append((keep_level[L], a_l))
        a = lax.dot_general(jnp.concatenate(q_sep, axis=1), jnp.concatenate(k_sep, axis=1),
                            (((1,), (1,)), ((), ())), preferred_element_type=F32)
        for keep, a_l in a_masked:
            a = jnp.where(keep, a_l, a)

        b3 = b.reshape(C // P, P, D)
        k3 = k.reshape(C // P, P, D)
        zs = []
        for s in range(P):
            bs = jnp.broadcast_to(b3[:, s:s + 1, :], (C // P, P, D)).reshape(C, D)
            ksb = jnp.broadcast_to(k3[:, s:s + 1, :], (C // P, P, D)).reshape(C, D)
            e = jnp.exp2(jnp.minimum(b - bs, 0.0))
            zs.append((q * ksb * e).astype(BF16))
        a_d = jnp.dot(jnp.concatenate(zs, axis=1), sel, preferred_element_type=F32)
        a = jnp.where(keep_diag, a_d, a)

        o = o + jnp.dot(a.astype(BF16), v16, preferred_element_type=F32)

        k_end = (k * jnp.exp2(b_end - b)).astype(BF16)
        upd = lax.dot_general(v16, k_end, (((0,), (0,)), ((), ())), preferred_element_type=F32)
        new_states.append(st * jnp.exp2(b_end) + upd)
        outs.append(_rms_rows(o, g_norm) * _silu(g_raw[hh].astype(F32)))
    return outs, new_states


def _lookahead_slot(x_ref, sh0_ref, sc0_ref, g_ref, h_scr, n_i):
    tile = pl.program_id(0) * n_i + pl.program_id(1)

    @pl.when((tile == 0) & (pl.program_id(2) == 0))
    def _():
        _modulated_norm_into(h_scr.at[0], x_ref, sh0_ref, sc0_ref, g_ref)

    return tile % 2


def _lookahead_slab(x_ref, shn_ref, scn_ref, g_ref, h_scr, cur, n_steps):
    j = pl.program_id(2)
    tm = x_ref.shape[0]
    per_step = -(-tm // ((n_steps - 1) * NORM_ROWS)) * NORM_ROWS
    gain = g_ref[...] * (1.0 + scn_ref[...])
    shift = shn_ref[...]
    r0 = jnp.clip((j - 1) * per_step, 0, tm - per_step)
    nxt = h_scr.at[1 - cur]
    for k in range(per_step // NORM_ROWS):
        rs = pl.ds(pl.multiple_of(r0 + k * NORM_ROWS, NORM_ROWS), NORM_ROWS)
        x = x_ref[rs, :]
        inv = lax.rsqrt(jnp.mean(x * x, axis=-1, keepdims=True) + EPS)
        nxt[rs, :] = ((x * inv) * gain + shift).astype(BF16)


def _lookahead_specs(x, mod, mod_col, tm, n_j):
    G, R, _ = x.shape
    n_i = R // tm
    last = G * n_i - 1

    def x_tile(b, i, j):
        t = jnp.minimum(b * n_i + i + (j > 0).astype(jnp.int32), last)
        return t // n_i, t % n_i, 0

    nxt_b = lambda b, i: jnp.minimum(b * n_i + i + 1, last) // n_i
    specs = [pl.BlockSpec((None, tm, D_MODEL), x_tile)]
    specs += [pl.BlockSpec((None, 1, D_MODEL), lambda b, i, j, c=mod_col + k: (0, 0, c))
              for k in range(2)]
    specs += [pl.BlockSpec((None, 1, D_MODEL), lambda b, i, j, c=mod_col + k: (nxt_b(b, i), 0, c))
              for k in range(2)]
    return specs, pltpu.VMEM((2, tm, D_MODEL), BF16), n_i


def _in_proj_kernel(x_ref, sh0_ref, sc0_ref, shn_ref, scn_ref, g_ref, w_ref, o32_ref, o16_ref,
                    h_scr, *, n_i, n_steps):
    cur = _lookahead_slot(x_ref, sh0_ref, sc0_ref, g_ref, h_scr, n_i)
    acc = jnp.dot(h_scr[cur], w_ref[...], preferred_element_type=F32)
    o32_ref[...] = acc
    o16_ref[...] = acc.astype(BF16)
    _lookahead_slab(x_ref, shn_ref, scn_ref, g_ref, h_scr, cur, n_steps)


def _in_proj_prompt(x, mod, gain, wt, f32_slabs, tm):
    G, R, _ = x.shape
    nt, _, tn = wt.shape
    n32 = len(f32_slabs)
    n16 = nt - n32
    order = tuple(f32_slabs) + tuple(s for s in range(nt) if s not in f32_slabs)
    x_specs, h_scratch, n_i = _lookahead_specs(x, mod, 0, tm, nt)
    return pl.pallas_call(
        functools.partial(_in_proj_kernel, n_i=n_i, n_steps=nt),
        grid=(G, R // tm, nt),
        in_specs=x_specs
        + [pl.BlockSpec((1, D_MODEL), lambda b, i, j: (0, 0)),
           pl.BlockSpec((None, D_MODEL, tn), lambda b, i, j: (_lookup(j, order), 0, 0))],
        out_specs=[
            pl.BlockSpec((None, None, tm, tn), lambda b, i, j: (jnp.minimum(j, n32), b, i, 0)),
            pl.BlockSpec((None, None, tm, tn),
                         lambda b, i, j: (jnp.where(j < n32, n16, j - n32), b, i, 0)),
        ],
        out_shape=[jax.ShapeDtypeStruct((n32 + 1, G, R, tn), F32),
                   jax.ShapeDtypeStruct((n16 + 1, G, R, tn), BF16)],
        scratch_shapes=[h_scratch],
        compiler_params=_cparams(("arbitrary", "arbitrary", "arbitrary")),
        name="in_proj_prompt",
    )(x, mod, mod, mod, mod, gain, wt)


def _gate_up_kernel(x_ref, sh0_ref, sc0_ref, shn_ref, scn_ref, g_ref, wg_ref, wu_ref, o_ref,
                    h_scr, *, n_i, n_steps):
    cur = _lookahead_slot(x_ref, sh0_ref, sc0_ref, g_ref, h_scr, n_i)
    h = h_scr[cur]
    gate = jnp.dot(h, wg_ref[...], preferred_element_type=F32)
    up = jnp.dot(h, wu_ref[...], preferred_element_type=F32)
    o_ref[...] = (_silu(gate) * up).astype(o_ref.dtype)
    _lookahead_slab(x_ref, shn_ref, scn_ref, g_ref, h_scr, cur, n_steps)


def _gate_up_prompt(x, mod, mod_col, gain, wgt, wut, tm):
    G, R, _ = x.shape
    nt, _, tn = wgt.shape
    x_specs, h_scratch, n_i = _lookahead_specs(x, mod, mod_col, tm, nt)
    w_spec = pl.BlockSpec((None, D_MODEL, tn), lambda b, i, j: (j, 0, 0))
    return pl.pallas_call(
        functools.partial(_gate_up_kernel, n_i=n_i, n_steps=nt),
        grid=(G, R // tm, nt),
        in_specs=x_specs + [pl.BlockSpec((1, D_MODEL), lambda b, i, j: (0, 0)), w_spec, w_spec],
        out_specs=pl.BlockSpec((None, None, tm, tn), lambda b, i, j: (j, b, i, 0)),
        out_shape=jax.ShapeDtypeStruct((nt, G, R, tn), BF16),
        scratch_shapes=[h_scratch],
        compiler_params=_cparams(("arbitrary", "arbitrary", "arbitrary")),
        name="gate_up_prompt",
    )(x, mod, mod, mod, mod, gain, wgt, wut)


HGRN_TB = 1024


def _hgrn_prompt_kernel(q_ref, f_ref, i_ref, g_ref, lb_ref, gn_ref, o_ref, s_ref, st_scr):
    C, D = HGRN_CHUNK, HEAD_DIM
    t_idx = pl.program_id(2)
    consts = _hgrn_consts()
    g_norm = gn_ref[...]
    heads = range(HGRN_HP)
    lane = lambda hh: slice(hh * D, (hh + 1) * D)
    lbs = [lb_ref[:, lane(hh)] for hh in heads]

    @pl.when(t_idx == 0)
    def _():
        st_scr[...] = jnp.zeros_like(st_scr)

    def chunk_body(c, carry):
        rs = pl.ds(pl.multiple_of(c * C, C), C)
        outs, new_states = _hgrn_heads(
            [q_ref[rs, lane(hh)] for hh in heads], [f_ref[rs, lane(hh)] for hh in heads],
            [i_ref[rs, lane(hh)] for hh in heads], [g_ref[rs, lane(hh)] for hh in heads],
            lbs, g_norm, [st_scr[hh] for hh in heads], consts)
        for hh in heads:
            st_scr[hh] = new_states[hh]
            o_ref[rs, lane(hh)] = outs[hh].astype(o_ref.dtype)
        return carry

    lax.fori_loop(0, q_ref.shape[0] // C, chunk_body, 0)

    @pl.when(t_idx == pl.num_programs(2) - 1)
    def _():
        for hh in heads:
            s_ref[hh] = st_scr[hh].T


def _hgrn_prompt(proj32, proj16, lb, g_norm):
    _, B, T, W = proj32.shape
    assert W == HGRN_HP * HEAD_DIM
    seq = lambda slab: pl.BlockSpec((None, None, HGRN_TB, W),
                                    lambda b, h, t: (slab + h, b, t, 0))
    return pl.pallas_call(
        _hgrn_prompt_kernel,
        grid=(B, N_LIN_HEADS // HGRN_HP, T // HGRN_TB),
        in_specs=[
            seq(0), seq(0), seq(2), seq(4),
            pl.BlockSpec((1, W), lambda b, h, t: (0, h)),
            pl.BlockSpec((1, HEAD_DIM), lambda b, h, t: (0, 0)),
        ],
        out_specs=(
            pl.BlockSpec((None, HGRN_TB, W), lambda b, h, t: (b, t, h)),
            pl.BlockSpec((None, HGRN_HP, HEAD_DIM, HEAD_DIM), lambda b, h, t: (b, h, 0, 0)),
        ),
        out_shape=(
            jax.ShapeDtypeStruct((B, T, LIN_WIDTH), BF16),
            jax.ShapeDtypeStruct((B, N_LIN_HEADS, HEAD_DIM, HEAD_DIM), F32),
        ),
        scratch_shapes=[pltpu.VMEM((HGRN_HP, HEAD_DIM, HEAD_DIM), F32)],
        compiler_params=_cparams(("parallel", "parallel", "arbitrary")),
        name="hgrn_prompt",
    )(proj16, proj32, proj16, proj16, lb, g_norm)


def _attn_prompt_kernel(sink_ref, q_ref, kvc_ref, kvp_ref, qn_ref, kn_ref,
                        o_ref, nk_ref, nv_ref):
    n = pl.program_id(1)
    W = WINDOW
    qi = lax.broadcasted_iota(jnp.int32, (W, 2 * W), 0)
    kj = lax.broadcasted_iota(jnp.int32, (W, 2 * W), 1)
    dist = W + qi - kj
    valid = (dist >= 0) & (dist <= W) & ((n > 0) | (kj >= W))
    dist_m = jnp.where(valid, dist.astype(F32), jnp.inf)
    q_gain = qn_ref[...] * (ATT_SCALE * LOG2E)
    k_norm = kn_ref[...]

    new_k = []
    for h in range(N_KV_HEADS):
        hs = slice(h * HEAD_DIM, (h + 1) * HEAD_DIM)
        vs = slice(KV_WIDTH + h * HEAD_DIM, KV_WIDTH + (h + 1) * HEAD_DIM)
        kc = _rms_rows(kvc_ref[:, hs], k_norm)
        kp = _rms_rows(kvp_ref[:, hs], k_norm)
        new_k.append(kc)
        k2 = jnp.concatenate([kp, kc], axis=0).astype(BF16)
        v2 = jnp.concatenate([kvp_ref[:, vs], kvc_ref[:, vs]], axis=0).astype(BF16)

        for g in range(GROUP):
            a = h * GROUP + g
            cs = slice(a * HEAD_DIM, (a + 1) * HEAD_DIM)
            qh = _rms_rows(q_ref[h, :, g * HEAD_DIM:(g + 1) * HEAD_DIM].astype(F32),
                           q_gain).astype(BF16)
            s = lax.dot_general(qh, k2, (((1,), (1,)), ((), ())), preferred_element_type=F32)
            s = s + (-ALIBI_SLOPES[a] * LOG2E) * dist_m
            sink = sink_ref[a] * LOG2E
            m = jnp.maximum(jnp.max(s, axis=-1, keepdims=True), sink)
            p = jnp.exp2(s - m)
            den = jnp.sum(p, axis=-1, keepdims=True) + jnp.exp2(sink - m)
            o = jnp.dot(p.astype(BF16), v2, preferred_element_type=F32) * (1.0 / den)
            o_ref[:, cs] = o.astype(o_ref.dtype)

    @pl.when(n == pl.num_programs(1) - 1)
    def _():
        for h in range(N_KV_HEADS):
            hs = slice(h * HEAD_DIM, (h + 1) * HEAD_DIM)
            nk_ref[:, hs] = new_k[h]
            nv_ref[:, hs] = kvc_ref[:, KV_WIDTH + h * HEAD_DIM:KV_WIDTH + (h + 1) * HEAD_DIM]


def _attn_prompt(proj32, proj16, sinks, q_norm, k_norm):
    _, B, T, W = proj32.shape
    assert W == GROUP * HEAD_DIM == 2 * KV_WIDTH
    nb = T // WINDOW
    q_slab = 6 // N_KV_HEADS
    kv_slab = 2
    prev = lambda n: jnp.maximum(n - 1, 0)
    return pl.pallas_call(
        _attn_prompt_kernel,
        grid=(B, nb),
        in_specs=[
            pl.BlockSpec(memory_space=pltpu.SMEM),
            pl.BlockSpec((N_KV_HEADS, None, WINDOW, W), lambda b, n: (q_slab, b, n, 0)),
            pl.BlockSpec((None, None, WINDOW, W), lambda b, n: (kv_slab, b, n, 0)),
            pl.BlockSpec((None, None, WINDOW, W), lambda b, n: (kv_slab, b, prev(n), 0)),
            pl.BlockSpec((1, HEAD_DIM), lambda b, n: (0, 0)),
            pl.BlockSpec((1, HEAD_DIM), lambda b, n: (0, 0)),
        ],
        out_specs=(
            pl.BlockSpec((None, WINDOW, ATT_WIDTH), lambda b, n: (b, n, 0)),
            pl.BlockSpec((None, WINDOW, KV_WIDTH), lambda b, n: (b, 0, 0)),
            pl.BlockSpec((None, WINDOW, KV_WIDTH), lambda b, n: (b, 0, 0)),
        ),
        out_shape=(
            jax.ShapeDtypeStruct((B, T, ATT_WIDTH), BF16),
            jax.ShapeDtypeStruct((B, WINDOW, KV_WIDTH), F32),
            jax.ShapeDtypeStruct((B, WINDOW, KV_WIDTH), F32),
        ),
        compiler_params=_cparams(("parallel", "arbitrary")),
        name="attn_prompt",
    )(sinks, proj16, proj32, proj32, q_norm, k_norm)


def _row_to_col(x_row, eye):
    return jnp.sum(jnp.where(eye, x_row, 0.0), axis=1, keepdims=True)


MIX_RB = 4


def _mix_sample_kernel(sink_ref, p_ref, s_ref, ck_ref, cv_ref, lb_ref,
                       gn_ref, qn_ref, kn_ref, o_ref, ns_ref, nk_ref, nv_ref):
    for r in range(p_ref.shape[0]):
        _mix_one_request(sink_ref, p_ref.at[r], s_ref.at[r], ck_ref.at[r], cv_ref.at[r], lb_ref,
                         gn_ref, qn_ref, kn_ref, o_ref.at[r], ns_ref.at[r], nk_ref.at[r],
                         nv_ref.at[r])


def _mix_one_request(sink_ref, p_ref, s_ref, ck_ref, cv_ref, lb_ref,
                     gn_ref, qn_ref, kn_ref, o_ref, ns_ref, nk_ref, nv_ref):
    W = WINDOW
    er = lax.broadcasted_iota(jnp.int32, (HEAD_DIM, HEAD_DIM), 0)
    ec = lax.broadcasted_iota(jnp.int32, (HEAD_DIM, HEAD_DIM), 1)
    eye = er == ec
    g_norm = gn_ref[...]
    q_norm = qn_ref[...]
    k_norm = kn_ref[...]

    def cols(off, width=HEAD_DIM):
        return p_ref[:, off:off + width]

    for h in range(N_LIN_HEADS):
        c0 = h * HEAD_DIM
        q = _silu(cols(c0))
        lb = lb_ref[:, c0:c0 + HEAD_DIM]
        _, fp = _forget_gate(cols(LIN_WIDTH + c0), lb)
        f_col = _row_to_col(lb + fp, eye)
        k_col = 1.0 - f_col
        v = cols(2 * LIN_WIDTH + c0)
        s_new = s_ref[h] * f_col + k_col * v
        ns_ref[h] = s_new
        q8 = jnp.broadcast_to(q, (8, HEAD_DIM)).astype(BF16)
        o = jnp.dot(q8, s_new.astype(BF16), preferred_element_type=F32)[0:1, :]
        o = _rms_rows(o, g_norm) * _silu(cols(3 * LIN_WIDTH + c0))
        o_ref[:, c0:c0 + HEAD_DIM] = o

    row8 = lax.broadcasted_iota(jnp.int32, (8, 1), 0)
    row8_k = lax.broadcasted_iota(jnp.int32, (8, HEAD_DIM), 0)
    lane = lax.broadcasted_iota(jnp.int32, (8, W), 1)
    dist_c = (W - lane).astype(F32)
    rows_w = lax.broadcasted_iota(jnp.int32, (W, HEAD_DIM), 0)
    for h in range(N_KV_HEADS):
        hs = slice(h * HEAD_DIM, (h + 1) * HEAD_DIM)
        k_new = _rms_rows(cols(OFF_K_A + h * HEAD_DIM), k_norm)
        v_new = cols(OFF_V_A + h * HEAD_DIM)
        kc = ck_ref[:, hs]
        vc = cv_ref[:, hs]
        nk_ref[:, hs] = jnp.where(rows_w == W - 1, k_new, pltpu.roll(kc, W - 1, 0))
        nv_ref[:, hs] = jnp.where(rows_w == W - 1, v_new, pltpu.roll(vc, W - 1, 0))

        q4 = jnp.zeros((8, HEAD_DIM), F32)
        slope = jnp.zeros((8, 1), F32)
        sink = jnp.zeros((8, 1), F32)
        for g in range(GROUP):
            a = h * GROUP + g
            qg = _rms_rows(cols(OFF_Q_A + a * HEAD_DIM), q_norm)
            q4 = jnp.where(row8_k == g, qg, q4)
            slope = jnp.where(row8 == g, ALIBI_SLOPES[a], slope)
            sink = jnp.where(row8 == g, sink_ref[a], sink)
        q4b = q4.astype(BF16)
        s_c = lax.dot_general(q4b, kc.astype(BF16), (((1,), (1,)), ((), ())),
                              preferred_element_type=F32)
        s_c = s_c * ATT_SCALE - slope * dist_c
        s_n = jnp.sum(q4 * k_new, axis=-1, keepdims=True) * ATT_SCALE
        m = jnp.maximum(jnp.maximum(jnp.max(s_c, axis=-1, keepdims=True), s_n), sink)
        p_c = jnp.exp(s_c - m)
        p_n = jnp.exp(s_n - m)
        den = jnp.sum(p_c, axis=-1, keepdims=True) + p_n + jnp.exp(sink - m)
        o = jnp.dot(p_c.astype(BF16), vc.astype(BF16), preferred_element_type=F32)
        o = (o + p_n * v_new) / den
        for g in range(GROUP):
            a = h * GROUP + g
            o_ref[:, LIN_WIDTH + a * HEAD_DIM:LIN_WIDTH + (a + 1) * HEAD_DIM] = o[g:g + 1, :]


def _mix_sample(proj, state, cache_k, cache_v, layer, sinks, lb, g_norm, q_norm, k_norm):
    B = proj.shape[0]
    rb = MIX_RB
    vec = pl.BlockSpec((1, HEAD_DIM), lambda b: (0, 0))
    lbs = pl.BlockSpec((1, LIN_WIDTH), lambda b: (0, 0))
    cache_in = pl.BlockSpec((None, rb, WINDOW, KV_WIDTH), lambda b: (layer, b, 0, 0))
    st_in = pl.BlockSpec((None, rb, N_LIN_HEADS, HEAD_DIM, HEAD_DIM),
                         lambda b: (layer, b, 0, 0, 0))
    cache = pl.BlockSpec((rb, WINDOW, KV_WIDTH), lambda b: (b, 0, 0))
    st = pl.BlockSpec((rb, N_LIN_HEADS, HEAD_DIM, HEAD_DIM), lambda b: (b, 0, 0, 0))
    return pl.pallas_call(
        _mix_sample_kernel,
        grid=(B // rb,),
        in_specs=[
            pl.BlockSpec(memory_space=pltpu.SMEM),
            pl.BlockSpec((rb, 1, IN_WIDTH), lambda b: (b, 0, 0)),
            st_in, cache_in, cache_in, lbs, vec, vec, vec,
        ],
        out_specs=(
            pl.BlockSpec((rb, 1, MIX_WIDTH), lambda b: (b, 0, 0)),
            st, cache, cache,
        ),
        out_shape=(
            jax.ShapeDtypeStruct((B, 1, MIX_WIDTH), F32),
            jax.ShapeDtypeStruct(state.shape[1:], F32),
            jax.ShapeDtypeStruct((B, WINDOW, KV_WIDTH), F32),
            jax.ShapeDtypeStruct((B, WINDOW, KV_WIDTH), F32),
        ),
        compiler_params=_cparams(("parallel",)),
        name="mix_sample",
    )(sinks, proj, state, cache_k, cache_v, lb, g_norm, q_norm, k_norm)


TILE_N = 512
P_TM = 1024
F32_SLABS = (LIN_WIDTH // TILE_N, LIN_WIDTH // TILE_N + 1, OFF_K_A // TILE_N)


def kernel(x_prompt, x_sample, cache_k, cache_v, state_hgrn, c_prompt, c_sample, lower_bounds,
           w_ada, b_ada, norm_mix, w_in, q_norm, k_norm, attn_sinks, g_norm, w_out, norm_ffn,
           w_gate, w_up, w_down):
    B, T, _ = x_prompt.shape
    BS = x_sample.shape[0]

    lb_all = _lower_bounds(lower_bounds)

    c_all = jnp.concatenate(
        [c_prompt, c_sample, jnp.zeros((MOD_ROWS - B - BS, D_MODEL), F32)], axis=0)
    mod = _modulation(c_all, w_ada, b_ada)

    cache_k2 = cache_k.reshape(DEPTH, BS, WINDOW, KV_WIDTH)
    cache_v2 = cache_v.reshape(DEPTH, BS, WINDOW, KV_WIDTH)

    yp = x_prompt
    ys = x_sample.reshape(1, BS, D_MODEL)
    pk, pv, ps, sk, sv, ss = [], [], [], [], [], []
    for l in range(DEPTH):
        mod_p = mod[l, :B].reshape(B, 1, 6 * D_MODEL)
        mod_s = mod[l, B:B + BS].reshape(1, BS, 6 * D_MODEL)
        gain_mix = norm_mix[l].reshape(1, D_MODEL)
        gain_ffn = norm_ffn[l].reshape(1, D_MODEL)
        lb = lb_all[l].reshape(1, LIN_WIDTH)
        gn = g_norm[l].reshape(1, HEAD_DIM)
        qn = q_norm[l].reshape(1, HEAD_DIM)
        kn = k_norm[l].reshape(1, HEAD_DIM)
        sinks = attn_sinks[l].astype(F32)

        proj_s, w_in_b = _norm_matmul(ys, mod_s, 0, gain_mix, w_in, l, BS, TILE_N)
        mixed_s, s_s, k_s, v_s = _mix_sample(
            proj_s.reshape(BS, 1, IN_WIDTH), state_hgrn, cache_k2, cache_v2, l,
            sinks, lb, gn, qn, kn)
        ys, w_out_b = _matmul_residual(mixed_s.reshape(1, BS, MIX_WIDTH), w_out, l, ys, mod_s,
                                       2 * D_MODEL, BS, TILE_N)
        ff_s, w_gate_b, w_up_b = _norm_gate_up(ys, mod_s, 3, gain_ffn, w_gate, w_up, l, BS, TILE_N)
        ys, w_down_b = _matmul_residual(ff_s, w_down, l, ys, mod_s, 5 * D_MODEL, BS, TILE_N)
        sk.append(k_s.reshape(BS, WINDOW, N_KV_HEADS, HEAD_DIM))
        sv.append(v_s.reshape(BS, WINDOW, N_KV_HEADS, HEAD_DIM))
        ss.append(s_s)

        proj32, proj16 = _in_proj_prompt(yp, mod_p, gain_mix, w_in_b, F32_SLABS, P_TM)
        o_l, s_p = _hgrn_prompt(proj32, proj16, lb, gn)
        o_a, k_p, v_p = _attn_prompt(proj32, proj16, sinks, qn, kn)
        yp = _res_matmul_prompt([o_l, o_a], [(w_out_b, 0), (w_out_b, 1)], yp, mod_p, 2, P_TM,
                                "out_proj")
        ff = _gate_up_prompt(yp, mod_p, 3, gain_ffn, w_gate_b, w_up_b, P_TM)
        yp = _res_matmul_prompt([ff], [w_down_b], yp, mod_p, 5, P_TM, "down_proj")
        pk.append(k_p.reshape(B, WINDOW, N_KV_HEADS, HEAD_DIM))
        pv.append(v_p.reshape(B, WINDOW, N_KV_HEADS, HEAD_DIM))
        ps.append(s_p)

    return (yp, ys.reshape(BS, 1, D_MODEL), jnp.stack(pk), jnp.stack(pv), jnp.stack(ps),
            jnp.stack(sk), jnp.stack(sv), jnp.stack(ss))
```

```python
import functools
import math

import jax
import jax.numpy as jnp
from jax import lax
from jax.experimental import pallas as pl
from jax.experimental.pallas import tpu as pltpu

F32 = jnp.float32
BF16 = jnp.bfloat16

D_MODEL = 2048
DEPTH = 4
HEAD_DIM = 128
N_LIN_HEADS = 8
LIN_WIDTH = N_LIN_HEADS * HEAD_DIM
N_ATT_HEADS = 8
N_KV_HEADS = 2
GROUP = N_ATT_HEADS // N_KV_HEADS
ATT_WIDTH = N_ATT_HEADS * HEAD_DIM
KV_WIDTH = N_KV_HEADS * HEAD_DIM
MIX_WIDTH = LIN_WIDTH + ATT_WIDTH
WINDOW = 128
D_FF = 5632
IN_WIDTH = 4 * LIN_WIDTH + ATT_WIDTH + 2 * KV_WIDTH
EPS = 1e-6
ATT_SCALE = HEAD_DIM ** -0.5
ALIBI_SLOPES = tuple(2.0 ** (-8.0 * (a + 1) / N_ATT_HEADS) for a in range(N_ATT_HEADS))

OFF_Q_A = 4 * LIN_WIDTH
OFF_K_A = OFF_Q_A + ATT_WIDTH
OFF_V_A = OFF_K_A + KV_WIDTH

MOD_ROWS = 40
HGRN_CHUNK = 128
HGRN_SUB = 8
LOG2E = math.log2(math.e)

VMEM_LIMIT = 56 * 1024 * 1024


def _cparams(sem):
    return pltpu.CompilerParams(dimension_semantics=sem, vmem_limit_bytes=VMEM_LIMIT)


def _silu(x):
    return x / (1.0 + jnp.exp(-x))


def _rms_rows(x, g):
    ms = jnp.mean(x * x, axis=-1, keepdims=True)
    return x * lax.rsqrt(ms + EPS) * g


def _forget_gate(f_raw, lb):
    t = jnp.exp(-jnp.abs(f_raw))
    inv = 1.0 / (1.0 + t)
    sig = jnp.where(f_raw >= 0.0, inv, t * inv)
    return t, (1.0 - lb) * sig


def _log_forget(f_raw, lb):
    t, fp = _forget_gate(f_raw, lb)
    log_sig = jnp.minimum(f_raw, 0.0) - jnp.log(1.0 + t)
    return jnp.where(lb > 0.0, jnp.log(lb + fp), log_sig), fp


def _lb_kernel(lb_ref, o_ref):
    x = lb_ref[...]
    m = jnp.max(x, axis=0, keepdims=True)
    e = jnp.exp(x - m)
    p = e / jnp.sum(e, axis=0, keepdims=True)
    acc = jnp.zeros_like(p[0:1])
    for l in range(DEPTH):
        if l > 0:
            acc = acc + p[l:l + 1]
        o_ref[l:l + 1, :] = acc


def _lower_bounds(lower_bounds):
    return pl.pallas_call(
        _lb_kernel,
        out_shape=jax.ShapeDtypeStruct((DEPTH, LIN_WIDTH), F32),
        name="lower_bounds",
    )(lower_bounds.astype(F32))


ADA_TN = 1024


def _ada_kernel(c_ref, w_ref, b_ref, o_ref):
    s = _silu(c_ref[...]).astype(BF16)
    acc = jnp.dot(s, w_ref[...].astype(BF16), preferred_element_type=F32)
    o_ref[...] = acc + b_ref[...]


def _modulation(c_all, w_ada, b_ada):
    n = 6 * D_MODEL
    return pl.pallas_call(
        _ada_kernel,
        grid=(DEPTH, n // ADA_TN),
        in_specs=[
            pl.BlockSpec((MOD_ROWS, D_MODEL), lambda l, j: (0, 0)),
            pl.BlockSpec((None, D_MODEL, ADA_TN), lambda l, j: (l, 0, j)),
            pl.BlockSpec((None, 1, ADA_TN), lambda l, j: (l, 0, j)),
        ],
        out_specs=pl.BlockSpec((None, MOD_ROWS, ADA_TN), lambda l, j: (l, 0, j)),
        out_shape=jax.ShapeDtypeStruct((DEPTH, MOD_ROWS, n), F32),
        compiler_params=_cparams(("parallel", "parallel")),
        name="modulation",
    )(c_all, w_ada, b_ada.reshape(DEPTH, 1, n))


NORM_ROWS = 16
NORM_UNROLL = 4


def _modulated_norm_into(h_scr, x_ref, sh_ref, sc_ref, g_ref):
    rows = x_ref.shape[0]
    ch = min(rows, NORM_ROWS)
    per_row = sc_ref.shape[0] != 1
    gain = g_ref[...]
    if not per_row:
        gain = gain * (1.0 + sc_ref[...])
        shift = sh_ref[...]

    def body(c, carry):
        rs = pl.ds(pl.multiple_of(c * ch, ch), ch)
        x = x_ref[rs, :]
        inv = lax.rsqrt(jnp.mean(x * x, axis=-1, keepdims=True) + EPS)
        if per_row:
            h = (x * inv * gain) * (1.0 + sc_ref[rs, :]) + sh_ref[rs, :]
        else:
            h = (x * inv) * gain + shift
        h_scr[rs, :] = h.astype(BF16)
        return carry

    lax.fori_loop(0, rows // ch, body, 0, unroll=min(NORM_UNROLL, rows // ch))


def _w_spec(w, layer, tn):
    return pl.BlockSpec((None, w.shape[1], tn), lambda b, i, j: (layer, 0, j))


def _wb_out(w, tn):
    rows, n = w.shape[-2:]
    return (pl.BlockSpec((None, rows, tn), lambda b, i, j: (j, 0, 0)),
            jax.ShapeDtypeStruct((n // tn, rows, tn), BF16))


def _norm_mm_kernel(x_ref, sh_ref, sc_ref, g_ref, w_ref, o_ref, wb_ref, h_scr):
    @pl.when(pl.program_id(2) == 0)
    def _():
        _modulated_norm_into(h_scr, x_ref, sh_ref, sc_ref, g_ref)

    w = w_ref[...].astype(BF16)
    wb_ref[...] = w
    o_ref[...] = jnp.dot(h_scr[...], w, preferred_element_type=F32)


def _mod_specs(mod, mod_col, tm):
    if mod.shape[1] == 1:
        return [pl.BlockSpec((None, 1, D_MODEL), lambda b, i, j, c=mod_col + k: (b, 0, c))
                for k in range(2)]
    return [pl.BlockSpec((None, tm, D_MODEL), lambda b, i, j, c=mod_col + k: (b, i, c))
            for k in range(2)]


def _lookup(j, table):
    out = table[-1]
    for k in range(len(table) - 2, -1, -1):
        out = jnp.where(j == k, table[k], out)
    return out


def _norm_matmul(x, mod, mod_col, gain, w, layer, tm, tn):
    G, R, _ = x.shape
    nt = w.shape[-1] // tn
    assert G * (R // tm) == 1
    wb_spec, wb_shape = _wb_out(w, tn)
    out_specs = [pl.BlockSpec((None, tm, tn), lambda b, i, j: (b, i, j)), wb_spec]
    out_shape = [jax.ShapeDtypeStruct((G, R, nt * tn), F32), wb_shape]
    return pl.pallas_call(
        _norm_mm_kernel,
        grid=(G, R // tm, nt),
        in_specs=[pl.BlockSpec((None, tm, D_MODEL), lambda b, i, j: (b, i, 0))]
        + _mod_specs(mod, mod_col, tm)
        + [pl.BlockSpec((1, D_MODEL), lambda b, i, j: (0, 0)), _w_spec(w, layer, tn)],
        out_specs=out_specs,
        out_shape=out_shape,
        scratch_shapes=[pltpu.VMEM((tm, D_MODEL), BF16)],
        compiler_params=_cparams(("parallel", "parallel", "arbitrary")),
        name="norm_in_proj",
    )(x, mod, mod, gain, w)


def _norm_gu_kernel(x_ref, sh_ref, sc_ref, g_ref, wg_ref, wu_ref, o_ref, wgb_ref, wub_ref, h_scr):
    @pl.when(pl.program_id(2) == 0)
    def _():
        _modulated_norm_into(h_scr, x_ref, sh_ref, sc_ref, g_ref)

    h = h_scr[...]
    wg = wg_ref[...].astype(BF16)
    wu = wu_ref[...].astype(BF16)
    wgb_ref[...] = wg
    wub_ref[...] = wu
    gate = jnp.dot(h, wg, preferred_element_type=F32)
    up = jnp.dot(h, wu, preferred_element_type=F32)
    o_ref[...] = (_silu(gate) * up).astype(o_ref.dtype)


def _norm_gate_up(x, mod, mod_col, gain, wg, wu, layer, tm, tn):
    G, R, _ = x.shape
    nt = wg.shape[-1] // tn
    assert G * (R // tm) == 1
    wb_spec, wb_shape = _wb_out(wg, tn)
    return pl.pallas_call(
        _norm_gu_kernel,
        grid=(G, R // tm, nt),
        in_specs=[pl.BlockSpec((None, tm, D_MODEL), lambda b, i, j: (b, i, 0))]
        + _mod_specs(mod, mod_col, tm)
        + [pl.BlockSpec((1, D_MODEL), lambda b, i, j: (0, 0)),
           _w_spec(wg, layer, tn), _w_spec(wu, layer, tn)],
        out_specs=[pl.BlockSpec((None, tm, tn), lambda b, i, j: (b, i, j)), wb_spec, wb_spec],
        out_shape=[jax.ShapeDtypeStruct((G, R, nt * tn), BF16), wb_shape, wb_shape],
        scratch_shapes=[pltpu.VMEM((tm, D_MODEL), BF16)],
        compiler_params=_cparams(("parallel", "parallel", "arbitrary")),
        name="norm_gate_up",
    )(x, mod, mod, gain, wg, wu)


def _res_matmul_kernel(*refs, n_a):
    a_refs, w_refs = refs[:n_a], refs[n_a:2 * n_a]
    x_ref, gt_ref, o_ref = refs[2 * n_a:]
    acc = None
    for a_ref, w_ref in zip(a_refs, w_refs):
        slabs = [a_ref[s] for s in range(a_ref.shape[0])] if len(a_ref.shape) == 3 else [a_ref[...]]
        ts = slabs[0].shape[1]
        for s, a in enumerate(slabs):
            part = jnp.dot(a, w_ref[s * ts:(s + 1) * ts, :], preferred_element_type=F32)
            acc = part if acc is None else acc + part
    o_ref[...] = x_ref[...] + gt_ref[...] * acc


def _res_matmul_prompt(a_list, w_list, x, mod, gate_col, tm, name):
    G, R, _ = x.shape
    a_specs, w_specs, ws = [], [], []
    for a, w in zip(a_list, w_list):
        w, row_blk = w if isinstance(w, tuple) else (w, 0)
        if a.ndim == 4:
            S, _, _, ts = a.shape
            a_specs.append(pl.BlockSpec((S, None, tm, ts), lambda b, i, j: (0, b, i, 0)))
            k = S * ts
        else:
            k = a.shape[2]
            a_specs.append(pl.BlockSpec((None, tm, k), lambda b, i, j: (b, i, 0)))
        nt, _, tn = w.shape
        w_specs.append(pl.BlockSpec((None, k, tn), lambda b, i, j, r=row_blk: (j, r, 0)))
        ws.append(w)
    tile = pl.BlockSpec((None, tm, tn), lambda b, i, j: (b, i, j))
    return pl.pallas_call(
        functools.partial(_res_matmul_kernel, n_a=len(a_list)),
        grid=(G, R // tm, nt),
        in_specs=a_specs + w_specs
        + [tile, pl.BlockSpec((None, 1, tn), lambda b, i, j: (b, 0, gate_col * nt + j))],
        out_specs=tile,
        out_shape=jax.ShapeDtypeStruct((G, R, D_MODEL), F32),
        compiler_params=_cparams(("parallel", "parallel", "arbitrary")),
        name=name,
    )(*a_list, *ws, x, mod)


def _mm_res_kernel(a_ref, w_ref, x_ref, gt_ref, o_ref, wb_ref):
    w = w_ref[...].astype(BF16)
    wb_ref[...] = w
    acc = jnp.dot(a_ref[...].astype(BF16), w, preferred_element_type=F32)
    o_ref[...] = x_ref[...] + gt_ref[...] * acc


def _matmul_residual(a, w, layer, x, mod, mod_off, tm, tn):
    G, R, K = a.shape
    assert G * (R // tm) == 1 and mod.shape[1] == R
    gcol = mod_off // tn
    wb_spec, wb_shape = _wb_out(w, tn)
    return pl.pallas_call(
        _mm_res_kernel,
        grid=(G, R // tm, D_MODEL // tn),
        in_specs=[
            pl.BlockSpec((None, tm, K), lambda b, i, j: (b, i, 0)),
            _w_spec(w, layer, tn),
            pl.BlockSpec((None, tm, tn), lambda b, i, j: (b, i, j)),
            pl.BlockSpec((None, tm, tn), lambda b, i, j: (b, i, gcol + j)),
        ],
        out_specs=[pl.BlockSpec((None, tm, tn), lambda b, i, j: (b, i, j)), wb_spec],
        out_shape=[jax.ShapeDtypeStruct((G, R, D_MODEL), F32), wb_shape],
        compiler_params=_cparams(("parallel", "parallel", "arbitrary")),
        name="matmul_residual",
    )(a, w, x, mod)


def _split3_bf16(x):
    hi = x.astype(BF16)
    r1 = x - hi.astype(F32)
    mid = r1.astype(BF16)
    lo = (r1 - mid.astype(F32)).astype(BF16)
    return hi, mid, lo


HGRN_HP = 4
HGRN_SEP_MIN = 32


def _hgrn_consts():
    C, P, D = HGRN_CHUNK, HGRN_SUB, HEAD_DIM
    row = lax.broadcasted_iota(jnp.int32, (C, C), 0)
    col = lax.broadcasted_iota(jnp.int32, (C, C), 1)
    tri = (row >= col).astype(BF16)
    sel_r = lax.broadcasted_iota(jnp.int32, (P * D, C), 0)
    sel_c = lax.broadcasted_iota(jnp.int32, (P * D, C), 1)
    psh = P.bit_length() - 1
    sel = ((sel_c & (P - 1)) == (sel_r >> (D.bit_length() - 1))).astype(BF16)
    keep_diag = ((col >> psh) == (row >> psh)) & ((col & (P - 1)) <= (row & (P - 1)))
    levels = []
    L = C // 2
    while L >= P:
        levels.append(L)
        L //= 2
    keep_level = {}
    for L in levels:
        if L < HGRN_SEP_MIN:
            sh = L.bit_length() - 1
            keep_level[L] = (((row >> sh) & 1) == 1) & ((col >> sh) == (row >> sh) - 1)
    return tri, sel, keep_diag, levels, keep_level


def _hgrn_heads(q_raw, f_raw, i_raw, g_raw, lbs, g_norm, states, consts):
    C, P, D = HGRN_CHUNK, HGRN_SUB, HEAD_DIM
    tri, sel, keep_diag, levels, keep_level = consts
    n = len(q_raw)
    qs, ks, vs, parts = [], [], [], []
    for hh in range(n):
        logf, fp = _log_forget(f_raw[hh], lbs[hh])
        qs.append(_silu(q_raw[hh].astype(F32)))
        ks.append((1.0 - lbs[hh]) - fp)
        vs.append(i_raw[hh].astype(BF16))
        parts.extend(_split3_bf16(logf * LOG2E))
    cum = jnp.dot(tri, jnp.concatenate(parts, axis=1), preferred_element_type=F32)

    outs, new_states = [], []
    for hh in range(n):
        q, k, v16, st = qs[hh], ks[hh], vs[hh], states[hh]
        b = (cum[:, (3 * hh) * D:(3 * hh + 1) * D] + cum[:, (3 * hh + 1) * D:(3 * hh + 2) * D]
             + cum[:, (3 * hh + 2) * D:(3 * hh + 3) * D])
        b_end = b[C - 1:C, :]
        o = lax.dot_general((q * jnp.exp2(b)).astype(BF16), st.astype(BF16),
                            (((1,), (1,)), ((), ())), preferred_element_type=F32)

        q_sep, k_sep, a_masked = [], [], []
        for L in levels:
            q_rows, k_rows = [], []
            for p in range(C // (2 * L)):
                lo, mid, hi = 2 * L * p, 2 * L * p + L, 2 * L * (p + 1)
                ref = b[mid - 1:mid, :]
                zf = jnp.zeros((L, D), F32)
                kk = jnp.concatenate([k[lo:mid] * jnp.exp2(ref - b[lo:mid]), zf], axis=0)
                qq = jnp.concatenate([zf, q[mid:hi] * jnp.exp2(b[mid:hi] - ref)], axis=0)
                q_rows.append(qq.astype(BF16))
                k_rows.append(kk.astype(BF16))
            if L >= HGRN_SEP_MIN:
                for p in range(len(q_rows)):
                    above = jnp.zeros((2 * L * p, D), BF16)
                    below = jnp.zeros((C - 2 * L * (p + 1), D), BF16)
                    for rows_p, dst in ((q_rows[p], q_sep), (k_rows[p], k_sep)):
                        dst.append(jnp.concatenate(
                            [y for y in (above, rows_p, below) if y.shape[0] > 0], axis=0))
            else:
                a_l = lax.dot_general(jnp.concatenate(q_rows, axis=0),
                                      jnp.concatenate(k_rows, axis=0),
                                      (((1,), (1,)), ((), ())), preferred_element_type=F32)
                a_masked.append((keep_level[L], a_l))
        a = lax.dot_general(jnp.concatenate(q_sep, axis=1), jnp.concatenate(k_sep, axis=1),
                            (((1,), (1,)), ((), ())), preferred_element_type=F32)
        for keep, a_l in a_masked:
            a = jnp.where(keep, a_l, a)

        b3 = b.reshape(C // P, P, D)
        k3 = k.reshape(C // P, P, D)
        zs = []
        for s in range(P):
            bs = jnp.broadcast_to(b3[:, s:s + 1, :], (C // P, P, D)).reshape(C, D)
            ksb = jnp.broadcast_to(k3[:, s:s + 1, :], (C // P, P, D)).reshape(C, D)
            e = jnp.exp2(jnp.minimum(b - bs, 0.0))
            zs.append((q * ksb * e).astype(BF16))
        a_d = jnp.dot(jnp.concatenate(zs, axis=1), sel, preferred_element_type=F32)
        a = jnp.where(keep_diag, a_d, a)

        o = o + jnp.dot(a.astype(BF16), v16, preferred_element_type=F32)

        k_end = (k * jnp.exp2(b_end - b)).astype(BF16)
        upd = lax.dot_general(v16, k_end, (((0,), (0,)), ((), ())), preferred_element_type=F32)
        new_states.append(st * jnp.exp2(b_end) + upd)
        outs.append(_rms_rows(o, g_norm) * _silu(g_raw[hh].astype(F32)))
    return outs, new_states


def _lookahead_slot(x_ref, sh0_ref, sc0_ref, g_ref, h_scr, n_i):
    tile = pl.program_id(0) * n_i + pl.program_id(1)

    @pl.when((tile == 0) & (pl.program_id(2) == 0))
    def _():
        _modulated_norm_into(h_scr.at[0], x_ref, sh0_ref, sc0_ref, g_ref)

    return tile % 2


def _lookahead_slab(x_ref, shn_ref, scn_ref, g_ref, h_scr, cur, n_steps):
    j = pl.program_id(2)
    tm = x_ref.shape[0]
    per_step = -(-tm // ((n_steps - 1) * NORM_ROWS)) * NORM_ROWS
    gain = g_ref[...] * (1.0 + scn_ref[...])
    shift = shn_ref[...]
    r0 = jnp.clip((j - 1) * per_step, 0, tm - per_step)
    nxt = h_scr.at[1 - cur]
    for k in range(per_step // NORM_ROWS):
        rs = pl.ds(pl.multiple_of(r0 + k * NORM_ROWS, NORM_ROWS), NORM_ROWS)
        x = x_ref[rs, :]
        inv = lax.rsqrt(jnp.mean(x * x, axis=-1, keepdims=True) + EPS)
        nxt[rs, :] = ((x * inv) * gain + shift).astype(BF16)


def _lookahead_specs(x, mod, mod_col, tm, n_j):
    G, R, _ = x.shape
    n_i = R // tm
    last = G * n_i - 1

    def x_tile(b, i, j):
        t = jnp.minimum(b * n_i + i + (j > 0).astype(jnp.int32), last)
        return t // n_i, t % n_i, 0

    nxt_b = lambda b, i: jnp.minimum(b * n_i + i + 1, last) // n_i
    specs = [pl.BlockSpec((None, tm, D_MODEL), x_tile)]
    specs += [pl.BlockSpec((None, 1, D_MODEL), lambda b, i, j, c=mod_col + k: (0, 0, c))
              for k in range(2)]
    specs += [pl.BlockSpec((None, 1, D_MODEL), lambda b, i, j, c=mod_col + k: (nxt_b(b, i), 0, c))
              for k in range(2)]
    return specs, pltpu.VMEM((2, tm, D_MODEL), BF16), n_i


def _in_proj_kernel(x_ref, sh0_ref, sc0_ref, shn_ref, scn_ref, g_ref, w_ref, o32_ref, o16_ref,
                    h_scr, *, n_i, n_steps):
    cur = _lookahead_slot(x_ref, sh0_ref, sc0_ref, g_ref, h_scr, n_i)
    acc = jnp.dot(h_scr[cur], w_ref[...], preferred_element_type=F32)
    o32_ref[...] = acc
    o16_ref[...] = acc.astype(BF16)
    _lookahead_slab(x_ref, shn_ref, scn_ref, g_ref, h_scr, cur, n_steps)


def _in_proj_prompt(x, mod, gain, wt, f32_slabs, tm):
    G, R, _ = x.shape
    nt, _, tn = wt.shape
    n32 = len(f32_slabs)
    n16 = nt - n32
    order = tuple(f32_slabs) + tuple(s for s in range(nt) if s not in f32_slabs)
    x_specs, h_scratch, n_i = _lookahead_specs(x, mod, 0, tm, nt)
    return pl.pallas_call(
        functools.partial(_in_proj_kernel, n_i=n_i, n_steps=nt),
        grid=(G, R // tm, nt),
        in_specs=x_specs
        + [pl.BlockSpec((1, D_MODEL), lambda b, i, j: (0, 0)),
           pl.BlockSpec((None, D_MODEL, tn), lambda b, i, j: (_lookup(j, order), 0, 0))],
        out_specs=[
            pl.BlockSpec((None, None, tm, tn), lambda b, i, j: (jnp.minimum(j, n32), b, i, 0)),
            pl.BlockSpec((None, None, tm, tn),
                         lambda b, i, j: (jnp.where(j < n32, n16, j - n32), b, i, 0)),
        ],
        out_shape=[jax.ShapeDtypeStruct((n32 + 1, G, R, tn), F32),
                   jax.ShapeDtypeStruct((n16 + 1, G, R, tn), BF16)],
        scratch_shapes=[h_scratch],
        compiler_params=_cparams(("arbitrary", "arbitrary", "arbitrary")),
        name="in_proj_prompt",
    )(x, mod, mod, mod, mod, gain, wt)


def _gate_up_kernel(x_ref, sh0_ref, sc0_ref, shn_ref, scn_ref, g_ref, wg_ref, wu_ref, o_ref,
                    h_scr, *, n_i, n_steps):
    cur = _lookahead_slot(x_ref, sh0_ref, sc0_ref, g_ref, h_scr, n_i)
    h = h_scr[cur]
    gate = jnp.dot(h, wg_ref[...], preferred_element_type=F32)
    up = jnp.dot(h, wu_ref[...], preferred_element_type=F32)
    o_ref[...] = (_silu(gate) * up).astype(o_ref.dtype)
    _lookahead_slab(x_ref, shn_ref, scn_ref, g_ref, h_scr, cur, n_steps)


def _gate_up_prompt(x, mod, mod_col, gain, wgt, wut, tm):
    G, R, _ = x.shape
    nt, _, tn = wgt.shape
    x_specs, h_scratch, n_i = _lookahead_specs(x, mod, mod_col, tm, nt)
    w_spec = pl.BlockSpec((None, D_MODEL, tn), lambda b, i, j: (j, 0, 0))
    return pl.pallas_call(
        functools.partial(_gate_up_kernel, n_i=n_i, n_steps=nt),
        grid=(G, R // tm, nt),
        in_specs=x_specs + [pl.BlockSpec((1, D_MODEL), lambda b, i, j: (0, 0)), w_spec, w_spec],
        out_specs=pl.BlockSpec((None, None, tm, tn), lambda b, i, j: (j, b, i, 0)),
        out_shape=jax.ShapeDtypeStruct((nt, G, R, tn), BF16),
        scratch_shapes=[h_scratch],
        compiler_params=_cparams(("arbitrary", "arbitrary", "arbitrary")),
        name="gate_up_prompt",
    )(x, mod, mod, mod, mod, gain, wgt, wut)


HGRN_TB = 2048


def _hgrn_prompt_kernel(q_ref, f_ref, i_ref, g_ref, lb_ref, gn_ref, o_ref, s_ref, st_scr):
    C, D = HGRN_CHUNK, HEAD_DIM
    t_idx = pl.program_id(2)
    consts = _hgrn_consts()
    g_norm = gn_ref[...]
    heads = range(HGRN_HP)
    lane = lambda hh: slice(hh * D, (hh + 1) * D)
    lbs = [lb_ref[:, lane(hh)] for hh in heads]

    @pl.when(t_idx == 0)
    def _():
        st_scr[...] = jnp.zeros_like(st_scr)

    def chunk_body(c, carry):
        rs = pl.ds(pl.multiple_of(c * C, C), C)
        outs, new_states = _hgrn_heads(
            [q_ref[rs, lane(hh)] for hh in heads], [f_ref[rs, lane(hh)] for hh in heads],
            [i_ref[rs, lane(hh)] for hh in heads], [g_ref[rs, lane(hh)] for hh in heads],
            lbs, g_norm, [st_scr[hh] for hh in heads], consts)
        for hh in heads:
            st_scr[hh] = new_states[hh]
            o_ref[rs, lane(hh)] = outs[hh].astype(o_ref.dtype)
        return carry

    lax.fori_loop(0, q_ref.shape[0] // C, chunk_body, 0)

    @pl.when(t_idx == pl.num_programs(2) - 1)
    def _():
        for hh in heads:
            s_ref[hh] = st_scr[hh].T


def _hgrn_prompt(proj32, proj16, lb, g_norm):
    _, B, T, W = proj32.shape
    assert W == HGRN_HP * HEAD_DIM
    seq = lambda slab: pl.BlockSpec((None, None, HGRN_TB, W),
                                    lambda b, h, t: (slab + h, b, t, 0))
    return pl.pallas_call(
        _hgrn_prompt_kernel,
        grid=(B, N_LIN_HEADS // HGRN_HP, T // HGRN_TB),
        in_specs=[
            seq(0), seq(0), seq(2), seq(4),
            pl.BlockSpec((1, W), lambda b, h, t: (0, h)),
            pl.BlockSpec((1, HEAD_DIM), lambda b, h, t: (0, 0)),
        ],
        out_specs=(
            pl.BlockSpec((None, HGRN_TB, W), lambda b, h, t: (b, t, h)),
            pl.BlockSpec((None, HGRN_HP, HEAD_DIM, HEAD_DIM), lambda b, h, t: (b, h, 0, 0)),
        ),
        out_shape=(
            jax.ShapeDtypeStruct((B, T, LIN_WIDTH), BF16),
            jax.ShapeDtypeStruct((B, N_LIN_HEADS, HEAD_DIM, HEAD_DIM), F32),
        ),
        scratch_shapes=[pltpu.VMEM((HGRN_HP, HEAD_DIM, HEAD_DIM), F32)],
        compiler_params=_cparams(("parallel", "parallel", "arbitrary")),
        name="hgrn_prompt",
    )(proj16, proj32, proj16, proj16, lb, g_norm)


ATT_QB = 4


def _attn_prompt_kernel(sink_ref, q_ref, kvc_ref, kvp_ref, qn_ref, kn_ref,
                        o_ref, nk_ref, nv_ref):
    n = pl.program_id(1)
    W = WINDOW
    qi = lax.broadcasted_iota(jnp.int32, (W, 2 * W), 0)
    kj = lax.broadcasted_iota(jnp.int32, (W, 2 * W), 1)
    dist = W + qi - kj
    in_window = (dist >= 0) & (dist <= W)
    dist_in = jnp.where(in_window, dist.astype(F32), jnp.inf)
    dist_first = jnp.where(in_window & ((n > 0) | (kj >= W)), dist.astype(F32), jnp.inf)
    q_gain = qn_ref[...] * (ATT_SCALE * LOG2E)
    k_norm = kn_ref[...]

    new_k = []
    for h in range(N_KV_HEADS):
        hs = slice(h * HEAD_DIM, (h + 1) * HEAD_DIM)
        vs = slice(KV_WIDTH + h * HEAD_DIM, KV_WIDTH + (h + 1) * HEAD_DIM)
        kc = _rms_rows(kvc_ref[:, hs], k_norm)
        kp = _rms_rows(kvp_ref[:, hs], k_norm)
        new_k.append(kc[(ATT_QB - 1) * W:, :])
        k_all = jnp.concatenate([kp, kc], axis=0).astype(BF16)
        v_all = jnp.concatenate([kvp_ref[:, vs], kvc_ref[:, vs]], axis=0).astype(BF16)

        for blk in range(ATT_QB):
            rows = slice(blk * W, (blk + 1) * W)
            k2 = k_all[blk * W:(blk + 2) * W, :]
            v2 = v_all[blk * W:(blk + 2) * W, :]
            dist_m = dist_first if blk == 0 else dist_in
            for g in range(GROUP):
                a = h * GROUP + g
                cs = slice(a * HEAD_DIM, (a + 1) * HEAD_DIM)
                qh = _rms_rows(q_ref[h, rows, g * HEAD_DIM:(g + 1) * HEAD_DIM].astype(F32),
                               q_gain).astype(BF16)
                s = lax.dot_general(qh, k2, (((1,), (1,)), ((), ())), preferred_element_type=F32)
                s = s + (-ALIBI_SLOPES[a] * LOG2E) * dist_m
                sink = sink_ref[a] * LOG2E
                m = jnp.maximum(jnp.max(s, axis=-1, keepdims=True), sink)
                p = jnp.exp2(s - m)
                den = jnp.sum(p, axis=-1, keepdims=True) + jnp.exp2(sink - m)
                o = jnp.dot(p.astype(BF16), v2, preferred_element_type=F32) * (1.0 / den)
                o_ref[rows, cs] = o.astype(o_ref.dtype)

    @pl.when(n == pl.num_programs(1) - 1)
    def _():
        for h in range(N_KV_HEADS):
            hs = slice(h * HEAD_DIM, (h + 1) * HEAD_DIM)
            nk_ref[:, hs] = new_k[h]
            nv_ref[:, hs] = kvc_ref[(ATT_QB - 1) * W:,
                                    KV_WIDTH + h * HEAD_DIM:KV_WIDTH + (h + 1) * HEAD_DIM]


def _attn_prompt(proj32, proj16, sinks, q_norm, k_norm):
    _, B, T, W = proj32.shape
    assert W == GROUP * HEAD_DIM == 2 * KV_WIDTH
    rows = ATT_QB * WINDOW
    q_slab = 6 // N_KV_HEADS
    kv_slab = 2
    prev = lambda n: jnp.maximum(ATT_QB * n - 1, 0)
    return pl.pallas_call(
        _attn_prompt_kernel,
        grid=(B, T // rows),
        in_specs=[
            pl.BlockSpec(memory_space=pltpu.SMEM),
            pl.BlockSpec((N_KV_HEADS, None, rows, W), lambda b, n: (q_slab, b, n, 0)),
            pl.BlockSpec((None, None, rows, W), lambda b, n: (kv_slab, b, n, 0)),
            pl.BlockSpec((None, None, WINDOW, W), lambda b, n: (kv_slab, b, prev(n), 0)),
            pl.BlockSpec((1, HEAD_DIM), lambda b, n: (0, 0)),
            pl.BlockSpec((1, HEAD_DIM), lambda b, n: (0, 0)),
        ],
        out_specs=(
            pl.BlockSpec((None, rows, ATT_WIDTH), lambda b, n: (b, n, 0)),
            pl.BlockSpec((None, WINDOW, KV_WIDTH), lambda b, n: (b, 0, 0)),
            pl.BlockSpec((None, WINDOW, KV_WIDTH), lambda b, n: (b, 0, 0)),
        ),
        out_shape=(
            jax.ShapeDtypeStruct((B, T, ATT_WIDTH), BF16),
            jax.ShapeDtypeStruct((B, WINDOW, KV_WIDTH), F32),
            jax.ShapeDtypeStruct((B, WINDOW, KV_WIDTH), F32),
        ),
        compiler_params=_cparams(("parallel", "arbitrary")),
        name="attn_prompt",
    )(sinks, proj16, proj32, proj32, q_norm, k_norm)


def _row_to_col(x_row, eye):
    return jnp.sum(jnp.where(eye, x_row, 0.0), axis=1, keepdims=True)


MIX_RB = 4


def _mix_sample_kernel(sink_ref, p_ref, s_ref, ck_ref, cv_ref, lb_ref,
                       gn_ref, qn_ref, kn_ref, o_ref, ns_ref, nk_ref, nv_ref):
    for r in range(p_ref.shape[0]):
        _mix_one_request(sink_ref, p_ref.at[r], s_ref.at[r], ck_ref.at[r], cv_ref.at[r], lb_ref,
                         gn_ref, qn_ref, kn_ref, o_ref.at[r], ns_ref.at[r], nk_ref.at[r],
                         nv_ref.at[r])


def _mix_one_request(sink_ref, p_ref, s_ref, ck_ref, cv_ref, lb_ref,
                     gn_ref, qn_ref, kn_ref, o_ref, ns_ref, nk_ref, nv_ref):
    W = WINDOW
    er = lax.broadcasted_iota(jnp.int32, (HEAD_DIM, HEAD_DIM), 0)
    ec = lax.broadcasted_iota(jnp.int32, (HEAD_DIM, HEAD_DIM), 1)
    eye = er == ec
    g_norm = gn_ref[...]
    q_norm = qn_ref[...]
    k_norm = kn_ref[...]

    def cols(off, width=HEAD_DIM):
        return p_ref[:, off:off + width]

    for h in range(N_LIN_HEADS):
        c0 = h * HEAD_DIM
        q = _silu(cols(c0))
        lb = lb_ref[:, c0:c0 + HEAD_DIM]
        _, fp = _forget_gate(cols(LIN_WIDTH + c0), lb)
        f_col = _row_to_col(lb + fp, eye)
        k_col = 1.0 - f_col
        v = cols(2 * LIN_WIDTH + c0)
        s_new = s_ref[h] * f_col + k_col * v
        ns_ref[h] = s_new
        q8 = jnp.broadcast_to(q, (8, HEAD_DIM)).astype(BF16)
        o = jnp.dot(q8, s_new.astype(BF16), preferred_element_type=F32)[0:1, :]
        o = _rms_rows(o, g_norm) * _silu(cols(3 * LIN_WIDTH + c0))
        o_ref[:, c0:c0 + HEAD_DIM] = o

    row8 = lax.broadcasted_iota(jnp.int32, (8, 1), 0)
    row8_k = lax.broadcasted_iota(jnp.int32, (8, HEAD_DIM), 0)
    lane = lax.broadcasted_iota(jnp.int32, (8, W), 1)
    dist_c = (W - lane).astype(F32)
    rows_w = lax.broadcasted_iota(jnp.int32, (W, HEAD_DIM), 0)
    for h in range(N_KV_HEADS):
        hs = slice(h * HEAD_DIM, (h + 1) * HEAD_DIM)
        k_new = _rms_rows(cols(OFF_K_A + h * HEAD_DIM), k_norm)
        v_new = cols(OFF_V_A + h * HEAD_DIM)
        kc = ck_ref[:, hs]
        vc = cv_ref[:, hs]
        nk_ref[:, hs] = jnp.where(rows_w == W - 1, k_new, pltpu.roll(kc, W - 1, 0))
        nv_ref[:, hs] = jnp.where(rows_w == W - 1, v_new, pltpu.roll(vc, W - 1, 0))

        q4 = jnp.zeros((8, HEAD_DIM), F32)
        slope = jnp.zeros((8, 1), F32)
        sink = jnp.zeros((8, 1), F32)
        for g in range(GROUP):
            a = h * GROUP + g
            qg = _rms_rows(cols(OFF_Q_A + a * HEAD_DIM), q_norm)
            q4 = jnp.where(row8_k == g, qg, q4)
            slope = jnp.where(row8 == g, ALIBI_SLOPES[a], slope)
            sink = jnp.where(row8 == g, sink_ref[a], sink)
        q4b = q4.astype(BF16)
        s_c = lax.dot_general(q4b, kc.astype(BF16), (((1,), (1,)), ((), ())),
                              preferred_element_type=F32)
        s_c = s_c * ATT_SCALE - slope * dist_c
        s_n = jnp.sum(q4 * k_new, axis=-1, keepdims=True) * ATT_SCALE
        m = jnp.maximum(jnp.maximum(jnp.max(s_c, axis=-1, keepdims=True), s_n), sink)
        p_c = jnp.exp(s_c - m)
        p_n = jnp.exp(s_n - m)
        den = jnp.sum(p_c, axis=-1, keepdims=True) + p_n + jnp.exp(sink - m)
        o = jnp.dot(p_c.astype(BF16), vc.astype(BF16), preferred_element_type=F32)
        o = (o + p_n * v_new) / den
        for g in range(GROUP):
            a = h * GROUP + g
            o_ref[:, LIN_WIDTH + a * HEAD_DIM:LIN_WIDTH + (a + 1) * HEAD_DIM] = o[g:g + 1, :]


def _mix_sample(proj, state, cache_k, cache_v, layer, sinks, lb, g_norm, q_norm, k_norm):
    B = proj.shape[0]
    rb = MIX_RB
    vec = pl.BlockSpec((1, HEAD_DIM), lambda b: (0, 0))
    lbs = pl.BlockSpec((1, LIN_WIDTH), lambda b: (0, 0))
    cache_in = pl.BlockSpec((None, rb, WINDOW, KV_WIDTH), lambda b: (layer, b, 0, 0))
    st_in = pl.BlockSpec((None, rb, N_LIN_HEADS, HEAD_DIM, HEAD_DIM),
                         lambda b: (layer, b, 0, 0, 0))
    cache = pl.BlockSpec((rb, WINDOW, KV_WIDTH), lambda b: (b, 0, 0))
    st = pl.BlockSpec((rb, N_LIN_HEADS, HEAD_DIM, HEAD_DIM), lambda b: (b, 0, 0, 0))
    return pl.pallas_call(
        _mix_sample_kernel,
        grid=(B // rb,),
        in_specs=[
            pl.BlockSpec(memory_space=pltpu.SMEM),
            pl.BlockSpec((rb, 1, IN_WIDTH), lambda b: (b, 0, 0)),
            st_in, cache_in, cache_in, lbs, vec, vec, vec,
        ],
        out_specs=(
            pl.BlockSpec((rb, 1, MIX_WIDTH), lambda b: (b, 0, 0)),
            st, cache, cache,
        ),
        out_shape=(
            jax.ShapeDtypeStruct((B, 1, MIX_WIDTH), F32),
            jax.ShapeDtypeStruct(state.shape[1:], F32),
            jax.ShapeDtypeStruct((B, WINDOW, KV_WIDTH), F32),
            jax.ShapeDtypeStruct((B, WINDOW, KV_WIDTH), F32),
        ),
        compiler_params=_cparams(("parallel",)),
        name="mix_sample",
    )(sinks, proj, state, cache_k, cache_v, lb, g_norm, q_norm, k_norm)


TILE_N = 512
P_TM = 1024
P_TM_OUT = 2048
F32_SLABS = (LIN_WIDTH // TILE_N, LIN_WIDTH // TILE_N + 1, OFF_K_A // TILE_N)


def kernel(x_prompt, x_sample, cache_k, cache_v, state_hgrn, c_prompt, c_sample, lower_bounds,
           w_ada, b_ada, norm_mix, w_in, q_norm, k_norm, attn_sinks, g_norm, w_out, norm_ffn,
           w_gate, w_up, w_down):
    B, T, _ = x_prompt.shape
    BS = x_sample.shape[0]

    lb_all = _lower_bounds(lower_bounds)

    c_all = jnp.concatenate(
        [c_prompt, c_sample, jnp.zeros((MOD_ROWS - B - BS, D_MODEL), F32)], axis=0)
    mod = _modulation(c_all, w_ada, b_ada)

    cache_k2 = cache_k.reshape(DEPTH, BS, WINDOW, KV_WIDTH)
    cache_v2 = cache_v.reshape(DEPTH, BS, WINDOW, KV_WIDTH)

    yp = x_prompt
    ys = x_sample.reshape(1, BS, D_MODEL)
    pk, pv, ps, sk, sv, ss = [], [], [], [], [], []
    for l in range(DEPTH):
        mod_p = mod[l, :B].reshape(B, 1, 6 * D_MODEL)
        mod_s = mod[l, B:B + BS].reshape(1, BS, 6 * D_MODEL)
        gain_mix = norm_mix[l].reshape(1, D_MODEL)
        gain_ffn = norm_ffn[l].reshape(1, D_MODEL)
        lb = lb_all[l].reshape(1, LIN_WIDTH)
        gn = g_norm[l].reshape(1, HEAD_DIM)
        qn = q_norm[l].reshape(1, HEAD_DIM)
        kn = k_norm[l].reshape(1, HEAD_DIM)
        sinks = attn_sinks[l].astype(F32)

        proj_s, w_in_b = _norm_matmul(ys, mod_s, 0, gain_mix, w_in, l, BS, TILE_N)
        mixed_s, s_s, k_s, v_s = _mix_sample(
            proj_s.reshape(BS, 1, IN_WIDTH), state_hgrn, cache_k2, cache_v2, l,
            sinks, lb, gn, qn, kn)
        ys, w_out_b = _matmul_residual(mixed_s.reshape(1, BS, MIX_WIDTH), w_out, l, ys, mod_s,
                                       2 * D_MODEL, BS, TILE_N)
        ff_s, w_gate_b, w_up_b = _norm_gate_up(ys, mod_s, 3, gain_ffn, w_gate, w_up, l, BS, TILE_N)
        ys, w_down_b = _matmul_residual(ff_s, w_down, l, ys, mod_s, 5 * D_MODEL, BS, TILE_N)
        sk.append(k_s.reshape(BS, WINDOW, N_KV_HEADS, HEAD_DIM))
        sv.append(v_s.reshape(BS, WINDOW, N_KV_HEADS, HEAD_DIM))
        ss.append(s_s)

        proj32, proj16 = _in_proj_prompt(yp, mod_p, gain_mix, w_in_b, F32_SLABS, P_TM)
        o_l, s_p = _hgrn_prompt(proj32, proj16, lb, gn)
        o_a, k_p, v_p = _attn_prompt(proj32, proj16, sinks, qn, kn)
        yp = _res_matmul_prompt([o_l, o_a], [(w_out_b, 0), (w_out_b, 1)], yp, mod_p, 2,
                                P_TM_OUT, "out_proj")
        ff = _gate_up_prompt(yp, mod_p, 3, gain_ffn, w_gate_b, w_up_b, P_TM)
        yp = _res_matmul_prompt([ff], [w_down_b], yp, mod_p, 5, P_TM, "down_proj")
        pk.append(k_p.reshape(B, WINDOW, N_KV_HEADS, HEAD_DIM))
        pv.append(v_p.reshape(B, WINDOW, N_KV_HEADS, HEAD_DIM))
        ps.append(s_p)

    return (yp, ys.reshape(BS, 1, D_MODEL), jnp.stack(pk), jnp.stack(pv), jnp.stack(ps),
            jnp.stack(sk), jnp.stack(sv), jnp.stack(ss))
```

```python
import functools
import math

import jax
import jax.numpy as jnp
from jax import lax
from jax.experimental import pallas as pl
from jax.experimental.pallas import tpu as pltpu

F32 = jnp.float32
BF16 = jnp.bfloat16

D_MODEL = 2048
DEPTH = 4
HEAD_DIM = 128
N_LIN_HEADS = 8
LIN_WIDTH = N_LIN_HEADS * HEAD_DIM
N_ATT_HEADS = 8
N_KV_HEADS = 2
GROUP = N_ATT_HEADS // N_KV_HEADS
ATT_WIDTH = N_ATT_HEADS * HEAD_DIM
KV_WIDTH = N_KV_HEADS * HEAD_DIM
MIX_WIDTH = LIN_WIDTH + ATT_WIDTH
WINDOW = 128
D_FF = 5632
IN_WIDTH = 4 * LIN_WIDTH + ATT_WIDTH + 2 * KV_WIDTH
EPS = 1e-6
ATT_SCALE = HEAD_DIM ** -0.5
ALIBI_SLOPES = tuple(2.0 ** (-8.0 * (a + 1) / N_ATT_HEADS) for a in range(N_ATT_HEADS))

OFF_Q_A = 4 * LIN_WIDTH
OFF_K_A = OFF_Q_A + ATT_WIDTH
OFF_V_A = OFF_K_A + KV_WIDTH

MOD_ROWS = 40
HGRN_CHUNK = 128
HGRN_SUB = 8
LOG2E = math.log2(math.e)

VMEM_LIMIT = 56 * 1024 * 1024


def _cparams(sem):
    return pltpu.CompilerParams(dimension_semantics=sem, vmem_limit_bytes=VMEM_LIMIT)


def _silu(x):
    return x / (1.0 + jnp.exp(-x))


def _rms_rows(x, g):
    ms = jnp.mean(x * x, axis=-1, keepdims=True)
    return x * lax.rsqrt(ms + EPS) * g


def _forget_gate(f_raw, lb):
    t = jnp.exp(-jnp.abs(f_raw))
    inv = 1.0 / (1.0 + t)
    sig = jnp.where(f_raw >= 0.0, inv, t * inv)
    return t, (1.0 - lb) * sig


def _log_forget(f_raw, lb):
    t, fp = _forget_gate(f_raw, lb)
    log_sig = jnp.minimum(f_raw, 0.0) - jnp.log(1.0 + t)
    return jnp.where(lb > 0.0, jnp.log(lb + fp), log_sig), fp


def _lb_kernel(lb_ref, o_ref):
    x = lb_ref[...]
    m = jnp.max(x, axis=0, keepdims=True)
    e = jnp.exp(x - m)
    p = e / jnp.sum(e, axis=0, keepdims=True)
    acc = jnp.zeros_like(p[0:1])
    for l in range(DEPTH):
        if l > 0:
            acc = acc + p[l:l + 1]
        o_ref[l:l + 1, :] = acc


def _lower_bounds(lower_bounds):
    return pl.pallas_call(
        _lb_kernel,
        out_shape=jax.ShapeDtypeStruct((DEPTH, LIN_WIDTH), F32),
        name="lower_bounds",
    )(lower_bounds.astype(F32))


ADA_TN = 1024


def _ada_kernel(c_ref, w_ref, b_ref, o_ref):
    s = _silu(c_ref[...]).astype(BF16)
    acc = jnp.dot(s, w_ref[...].astype(BF16), preferred_element_type=F32)
    o_ref[...] = acc + b_ref[...]


def _modulation(c_all, w_ada, b_ada):
    n = 6 * D_MODEL
    return pl.pallas_call(
        _ada_kernel,
        grid=(DEPTH, n // ADA_TN),
        in_specs=[
            pl.BlockSpec((MOD_ROWS, D_MODEL), lambda l, j: (0, 0)),
            pl.BlockSpec((None, D_MODEL, ADA_TN), lambda l, j: (l, 0, j)),
            pl.BlockSpec((None, 1, ADA_TN), lambda l, j: (l, 0, j)),
        ],
        out_specs=pl.BlockSpec((None, MOD_ROWS, ADA_TN), lambda l, j: (l, 0, j)),
        out_shape=jax.ShapeDtypeStruct((DEPTH, MOD_ROWS, n), F32),
        compiler_params=_cparams(("parallel", "parallel")),
        name="modulation",
    )(c_all, w_ada, b_ada.reshape(DEPTH, 1, n))


NORM_ROWS = 16
NORM_UNROLL = 4


def _modulated_norm_into(h_scr, x_ref, sh_ref, sc_ref, g_ref):
    rows = x_ref.shape[0]
    ch = min(rows, NORM_ROWS)
    per_row = sc_ref.shape[0] != 1
    gain = g_ref[...]
    if not per_row:
        gain = gain * (1.0 + sc_ref[...])
        shift = sh_ref[...]

    def body(c, carry):
        rs = pl.ds(pl.multiple_of(c * ch, ch), ch)
        x = x_ref[rs, :]
        inv = lax.rsqrt(jnp.mean(x * x, axis=-1, keepdims=True) + EPS)
        if per_row:
            h = (x * inv * gain) * (1.0 + sc_ref[rs, :]) + sh_ref[rs, :]
        else:
            h = (x * inv) * gain + shift
        h_scr[rs, :] = h.astype(BF16)
        return carry

    lax.fori_loop(0, rows // ch, body, 0, unroll=min(NORM_UNROLL, rows // ch))


def _w_spec(w, layer, tn):
    return pl.BlockSpec((None, w.shape[1], tn), lambda b, i, j: (layer, 0, j))


def _wb_out(w, tn):
    rows, n = w.shape[-2:]
    return (pl.BlockSpec((None, rows, tn), lambda b, i, j: (j, 0, 0)),
            jax.ShapeDtypeStruct((n // tn, rows, tn), BF16))


def _norm_mm_kernel(x_ref, sh_ref, sc_ref, g_ref, w_ref, o_ref, wb_ref, h_scr):
    @pl.when(pl.program_id(2) == 0)
    def _():
        _modulated_norm_into(h_scr, x_ref, sh_ref, sc_ref, g_ref)

    w = w_ref[...].astype(BF16)
    wb_ref[...] = w
    o_ref[...] = jnp.dot(h_scr[...], w, preferred_element_type=F32)


def _mod_specs(mod, mod_col, tm):
    if mod.shape[1] == 1:
        return [pl.BlockSpec((None, 1, D_MODEL), lambda b, i, j, c=mod_col + k: (b, 0, c))
                for k in range(2)]
    return [pl.BlockSpec((None, tm, D_MODEL), lambda b, i, j, c=mod_col + k: (b, i, c))
            for k in range(2)]


def _lookup(j, table):
    out = table[-1]
    for k in range(len(table) - 2, -1, -1):
        out = jnp.where(j == k, table[k], out)
    return out


def _norm_matmul(x, mod, mod_col, gain, w, layer, tm, tn):
    G, R, _ = x.shape
    nt = w.shape[-1] // tn
    assert G * (R // tm) == 1
    wb_spec, wb_shape = _wb_out(w, tn)
    out_specs = [pl.BlockSpec((None, tm, tn), lambda b, i, j: (b, i, j)), wb_spec]
    out_shape = [jax.ShapeDtypeStruct((G, R, nt * tn), F32), wb_shape]
    return pl.pallas_call(
        _norm_mm_kernel,
        grid=(G, R // tm, nt),
        in_specs=[pl.BlockSpec((None, tm, D_MODEL), lambda b, i, j: (b, i, 0))]
        + _mod_specs(mod, mod_col, tm)
        + [pl.BlockSpec((1, D_MODEL), lambda b, i, j: (0, 0)), _w_spec(w, layer, tn)],
        out_specs=out_specs,
        out_shape=out_shape,
        scratch_shapes=[pltpu.VMEM((tm, D_MODEL), BF16)],
        compiler_params=_cparams(("parallel", "parallel", "arbitrary")),
        name="norm_in_proj",
    )(x, mod, mod, gain, w)


def _norm_gu_kernel(x_ref, sh_ref, sc_ref, g_ref, wg_ref, wu_ref, o_ref, wgb_ref, wub_ref, h_scr):
    @pl.when(pl.program_id(2) == 0)
    def _():
        _modulated_norm_into(h_scr, x_ref, sh_ref, sc_ref, g_ref)

    h = h_scr[...]
    wg = wg_ref[...].astype(BF16)
    wu = wu_ref[...].astype(BF16)
    wgb_ref[...] = wg
    wub_ref[...] = wu
    gate = jnp.dot(h, wg, preferred_element_type=F32)
    up = jnp.dot(h, wu, preferred_element_type=F32)
    o_ref[...] = (_silu(gate) * up).astype(o_ref.dtype)


def _norm_gate_up(x, mod, mod_col, gain, wg, wu, layer, tm, tn):
    G, R, _ = x.shape
    nt = wg.shape[-1] // tn
    assert G * (R // tm) == 1
    wb_spec, wb_shape = _wb_out(wg, tn)
    return pl.pallas_call(
        _norm_gu_kernel,
        grid=(G, R // tm, nt),
        in_specs=[pl.BlockSpec((None, tm, D_MODEL), lambda b, i, j: (b, i, 0))]
        + _mod_specs(mod, mod_col, tm)
        + [pl.BlockSpec((1, D_MODEL), lambda b, i, j: (0, 0)),
           _w_spec(wg, layer, tn), _w_spec(wu, layer, tn)],
        out_specs=[pl.BlockSpec((None, tm, tn), lambda b, i, j: (b, i, j)), wb_spec, wb_spec],
        out_shape=[jax.ShapeDtypeStruct((G, R, nt * tn), BF16), wb_shape, wb_shape],
        scratch_shapes=[pltpu.VMEM((tm, D_MODEL), BF16)],
        compiler_params=_cparams(("parallel", "parallel", "arbitrary")),
        name="norm_gate_up",
    )(x, mod, mod, gain, wg, wu)


def _res_matmul_kernel(*refs, n_a):
    a_refs, w_refs = refs[:n_a], refs[n_a:2 * n_a]
    x_ref, gt_ref, o_ref = refs[2 * n_a:]
    acc = None
    for a_ref, w_ref in zip(a_refs, w_refs):
        slabs = [a_ref[s] for s in range(a_ref.shape[0])] if len(a_ref.shape) == 3 else [a_ref[...]]
        ts = slabs[0].shape[1]
        for s, a in enumerate(slabs):
            part = jnp.dot(a, w_ref[s * ts:(s + 1) * ts, :], preferred_element_type=F32)
            acc = part if acc is None else acc + part
    o_ref[...] = x_ref[...] + gt_ref[...] * acc


def _res_matmul_prompt(a_list, w_list, x, mod, gate_col, tm, name):
    G, R, _ = x.shape
    a_specs, w_specs, ws = [], [], []
    for a, w in zip(a_list, w_list):
        w, row_blk = w if isinstance(w, tuple) else (w, 0)
        if a.ndim == 4:
            S, _, _, ts = a.shape
            a_specs.append(pl.BlockSpec((S, None, tm, ts), lambda b, i, j: (0, b, i, 0)))
            k = S * ts
        else:
            k = a.shape[2]
            a_specs.append(pl.BlockSpec((None, tm, k), lambda b, i, j: (b, i, 0)))
        nt, _, tn = w.shape
        w_specs.append(pl.BlockSpec((None, k, tn), lambda b, i, j, r=row_blk: (j, r, 0)))
        ws.append(w)
    tile = pl.BlockSpec((None, tm, tn), lambda b, i, j: (b, i, j))
    return pl.pallas_call(
        functools.partial(_res_matmul_kernel, n_a=len(a_list)),
        grid=(G, R // tm, nt),
        in_specs=a_specs + w_specs
        + [tile, pl.BlockSpec((None, 1, tn), lambda b, i, j: (b, 0, gate_col * nt + j))],
        out_specs=tile,
        out_shape=jax.ShapeDtypeStruct((G, R, D_MODEL), F32),
        compiler_params=_cparams(("parallel", "parallel", "arbitrary")),
        name=name,
    )(*a_list, *ws, x, mod)


def _mm_res_kernel(a_ref, w_ref, x_ref, gt_ref, o_ref, wb_ref):
    w = w_ref[...].astype(BF16)
    wb_ref[...] = w
    acc = jnp.dot(a_ref[...].astype(BF16), w, preferred_element_type=F32)
    o_ref[...] = x_ref[...] + gt_ref[...] * acc


def _matmul_residual(a, w, layer, x, mod, mod_off, tm, tn):
    G, R, K = a.shape
    assert G * (R // tm) == 1 and mod.shape[1] == R
    gcol = mod_off // tn
    wb_spec, wb_shape = _wb_out(w, tn)
    return pl.pallas_call(
        _mm_res_kernel,
        grid=(G, R // tm, D_MODEL // tn),
        in_specs=[
            pl.BlockSpec((None, tm, K), lambda b, i, j: (b, i, 0)),
            _w_spec(w, layer, tn),
            pl.BlockSpec((None, tm, tn), lambda b, i, j: (b, i, j)),
            pl.BlockSpec((None, tm, tn), lambda b, i, j: (b, i, gcol + j)),
        ],
        out_specs=[pl.BlockSpec((None, tm, tn), lambda b, i, j: (b, i, j)), wb_spec],
        out_shape=[jax.ShapeDtypeStruct((G, R, D_MODEL), F32), wb_shape],
        compiler_params=_cparams(("parallel", "parallel", "arbitrary")),
        name="matmul_residual",
    )(a, w, x, mod)


def _split3_bf16(x):
    hi = x.astype(BF16)
    r1 = x - hi.astype(F32)
    mid = r1.astype(BF16)
    lo = (r1 - mid.astype(F32)).astype(BF16)
    return hi, mid, lo


HGRN_HP = 4
HGRN_SEP_MIN = 32


def _hgrn_consts():
    C, P, D = HGRN_CHUNK, HGRN_SUB, HEAD_DIM
    row = lax.broadcasted_iota(jnp.int32, (C, C), 0)
    col = lax.broadcasted_iota(jnp.int32, (C, C), 1)
    tri = (row >= col).astype(BF16)
    sel_r = lax.broadcasted_iota(jnp.int32, (P * D, C), 0)
    sel_c = lax.broadcasted_iota(jnp.int32, (P * D, C), 1)
    psh = P.bit_length() - 1
    sel = ((sel_c & (P - 1)) == (sel_r >> (D.bit_length() - 1))).astype(BF16)
    keep_diag = ((col >> psh) == (row >> psh)) & ((col & (P - 1)) <= (row & (P - 1)))
    levels = []
    L = C // 2
    while L >= P:
        levels.append(L)
        L //= 2
    keep_level = {}
    for L in levels:
        if L < HGRN_SEP_MIN:
            sh = L.bit_length() - 1
            keep_level[L] = (((row >> sh) & 1) == 1) & ((col >> sh) == (row >> sh) - 1)
    return tri, sel, keep_diag, levels, keep_level


def _hgrn_heads(q_raw, f_raw, i_raw, g_raw, lbs, g_norm, states, consts, bk_scr):
    C, P, D = HGRN_CHUNK, HGRN_SUB, HEAD_DIM
    tri, sel, keep_diag, levels, keep_level = consts
    n = len(q_raw)
    qs, ks, vs, parts = [], [], [], []
    for hh in range(n):
        logf, fp = _log_forget(f_raw[hh], lbs[hh])
        qs.append(_silu(q_raw[hh].astype(F32)))
        ks.append((1.0 - lbs[hh]) - fp)
        vs.append(i_raw[hh].astype(BF16))
        parts.extend(_split3_bf16(logf * LOG2E))
    cum = jnp.dot(tri, jnp.concatenate(parts, axis=1), preferred_element_type=F32)

    outs, new_states = [], []
    for hh in range(n):
        q, k, v16, st = qs[hh], ks[hh], vs[hh], states[hh]
        b = (cum[:, (3 * hh) * D:(3 * hh + 1) * D] + cum[:, (3 * hh + 1) * D:(3 * hh + 2) * D]
             + cum[:, (3 * hh + 2) * D:(3 * hh + 3) * D])
        b_end = b[C - 1:C, :]
        o = lax.dot_general((q * jnp.exp2(b)).astype(BF16), st.astype(BF16),
                            (((1,), (1,)), ((), ())), preferred_element_type=F32)

        q_sep, k_sep, a_masked = [], [], []
        for L in levels:
            q_rows, k_rows = [], []
            for p in range(C // (2 * L)):
                lo, mid, hi = 2 * L * p, 2 * L * p + L, 2 * L * (p + 1)
                ref = b[mid - 1:mid, :]
                zf = jnp.zeros((L, D), F32)
                kk = jnp.concatenate([k[lo:mid] * jnp.exp2(ref - b[lo:mid]), zf], axis=0)
                qq = jnp.concatenate([zf, q[mid:hi] * jnp.exp2(b[mid:hi] - ref)], axis=0)
                q_rows.append(qq.astype(BF16))
                k_rows.append(kk.astype(BF16))
            if L >= HGRN_SEP_MIN:
                for p in range(len(q_rows)):
                    above = jnp.zeros((2 * L * p, D), BF16)
                    below = jnp.zeros((C - 2 * L * (p + 1), D), BF16)
                    for rows_p, dst in ((q_rows[p], q_sep), (k_rows[p], k_sep)):
                        dst.append(jnp.concatenate(
                            [y for y in (above, rows_p, below) if y.shape[0] > 0], axis=0))
            else:
                a_l = lax.dot_general(jnp.concatenate(q_rows, axis=0),
                                      jnp.concatenate(k_rows, axis=0),
                                      (((1,), (1,)), ((), ())), preferred_element_type=F32)
                a_masked.append((keep_level[L], a_l))
        a = lax.dot_general(jnp.concatenate(q_sep, axis=1), jnp.concatenate(k_sep, axis=1),
                            (((1,), (1,)), ((), ())), preferred_element_type=F32)
        for keep, a_l in a_masked:
            a = jnp.where(keep, a_l, a)

        bk_scr[hh, 0] = b
        bk_scr[hh, 1] = k
        zs = []
        for s in range(P):
            bs, ksb = [jnp.concatenate(
                [bk_scr[hh, w, pl.ds(m * P + s, P, stride=0), :] for m in range(C // P)], axis=0)
                for w in range(2)]
            e = jnp.exp2(jnp.minimum(b - bs, 0.0))
            zs.append((q * ksb * e).astype(BF16))
        a_d = jnp.dot(jnp.concatenate(zs, axis=1), sel, preferred_element_type=F32)
        a = jnp.where(keep_diag, a_d, a)

        o = o + jnp.dot(a.astype(BF16), v16, preferred_element_type=F32)

        k_end = (k * jnp.exp2(b_end - b)).astype(BF16)
        upd = lax.dot_general(v16, k_end, (((0,), (0,)), ((), ())), preferred_element_type=F32)
        new_states.append(st * jnp.exp2(b_end) + upd)
        outs.append(_rms_rows(o, g_norm) * _silu(g_raw[hh].astype(F32)))
    return outs, new_states


def _lookahead_slot(x_ref, sh0_ref, sc0_ref, g_ref, h_scr, n_i):
    tile = pl.program_id(0) * n_i + pl.program_id(1)

    @pl.when((tile == 0) & (pl.program_id(2) == 0))
    def _():
        _modulated_norm_into(h_scr.at[0], x_ref, sh0_ref, sc0_ref, g_ref)

    return tile % 2


def _lookahead_slab(x_ref, shn_ref, scn_ref, g_ref, h_scr, cur, n_steps):
    j = pl.program_id(2)
    tm = x_ref.shape[0]
    per_step = -(-tm // ((n_steps - 1) * NORM_ROWS)) * NORM_ROWS
    gain = g_ref[...] * (1.0 + scn_ref[...])
    shift = shn_ref[...]
    r0 = jnp.clip((j - 1) * per_step, 0, tm - per_step)
    nxt = h_scr.at[1 - cur]
    for k in range(per_step // NORM_ROWS):
        rs = pl.ds(pl.multiple_of(r0 + k * NORM_ROWS, NORM_ROWS), NORM_ROWS)
        x = x_ref[rs, :]
        inv = lax.rsqrt(jnp.mean(x * x, axis=-1, keepdims=True) + EPS)
        nxt[rs, :] = ((x * inv) * gain + shift).astype(BF16)


def _lookahead_specs(x, mod, mod_col, tm, n_j):
    G, R, _ = x.shape
    n_i = R // tm
    last = G * n_i - 1

    def x_tile(b, i, j):
        t = jnp.minimum(b * n_i + i + (j > 0).astype(jnp.int32), last)
        return t // n_i, t % n_i, 0

    nxt_b = lambda b, i: jnp.minimum(b * n_i + i + 1, last) // n_i
    specs = [pl.BlockSpec((None, tm, D_MODEL), x_tile)]
    specs += [pl.BlockSpec((None, 1, D_MODEL), lambda b, i, j, c=mod_col + k: (0, 0, c))
              for k in range(2)]
    specs += [pl.BlockSpec((None, 1, D_MODEL), lambda b, i, j, c=mod_col + k: (nxt_b(b, i), 0, c))
              for k in range(2)]
    return specs, pltpu.VMEM((2, tm, D_MODEL), BF16), n_i


def _in_proj_kernel(x_ref, sh0_ref, sc0_ref, shn_ref, scn_ref, g_ref, w_ref, o32_ref, o16_ref,
                    h_scr, *, n_i, n_steps):
    cur = _lookahead_slot(x_ref, sh0_ref, sc0_ref, g_ref, h_scr, n_i)
    acc = jnp.dot(h_scr[cur], w_ref[...], preferred_element_type=F32)
    o32_ref[...] = acc
    o16_ref[...] = acc.astype(BF16)
    _lookahead_slab(x_ref, shn_ref, scn_ref, g_ref, h_scr, cur, n_steps)


def _in_proj_prompt(x, mod, gain, wt, f32_slabs, tm):
    G, R, _ = x.shape
    nt, _, tn = wt.shape
    n32 = len(f32_slabs)
    n16 = nt - n32
    order = tuple(f32_slabs) + tuple(s for s in range(nt) if s not in f32_slabs)
    x_specs, h_scratch, n_i = _lookahead_specs(x, mod, 0, tm, nt)
    return pl.pallas_call(
        functools.partial(_in_proj_kernel, n_i=n_i, n_steps=nt),
        grid=(G, R // tm, nt),
        in_specs=x_specs
        + [pl.BlockSpec((1, D_MODEL), lambda b, i, j: (0, 0)),
           pl.BlockSpec((None, D_MODEL, tn), lambda b, i, j: (_lookup(j, order), 0, 0))],
        out_specs=[
            pl.BlockSpec((None, None, tm, tn), lambda b, i, j: (jnp.minimum(j, n32), b, i, 0)),
            pl.BlockSpec((None, None, tm, tn),
                         lambda b, i, j: (jnp.where(j < n32, n16, j - n32), b, i, 0)),
        ],
        out_shape=[jax.ShapeDtypeStruct((n32 + 1, G, R, tn), F32),
                   jax.ShapeDtypeStruct((n16 + 1, G, R, tn), BF16)],
        scratch_shapes=[h_scratch],
        compiler_params=_cparams(("arbitrary", "arbitrary", "arbitrary")),
        name="in_proj_prompt",
    )(x, mod, mod, mod, mod, gain, wt)


def _gate_up_kernel(x_ref, sh0_ref, sc0_ref, shn_ref, scn_ref, g_ref, wg_ref, wu_ref, o_ref,
                    h_scr, *, n_i, n_steps):
    cur = _lookahead_slot(x_ref, sh0_ref, sc0_ref, g_ref, h_scr, n_i)
    h = h_scr[cur]
    gate = jnp.dot(h, wg_ref[...], preferred_element_type=F32)
    up = jnp.dot(h, wu_ref[...], preferred_element_type=F32)
    o_ref[...] = (_silu(gate) * up).astype(o_ref.dtype)
    _lookahead_slab(x_ref, shn_ref, scn_ref, g_ref, h_scr, cur, n_steps)


def _gate_up_prompt(x, mod, mod_col, gain, wgt, wut, tm):
    G, R, _ = x.shape
    nt, _, tn = wgt.shape
    x_specs, h_scratch, n_i = _lookahead_specs(x, mod, mod_col, tm, nt)
    w_spec = pl.BlockSpec((None, D_MODEL, tn), lambda b, i, j: (j, 0, 0))
    return pl.pallas_call(
        functools.partial(_gate_up_kernel, n_i=n_i, n_steps=nt),
        grid=(G, R // tm, nt),
        in_specs=x_specs + [pl.BlockSpec((1, D_MODEL), lambda b, i, j: (0, 0)), w_spec, w_spec],
        out_specs=pl.BlockSpec((None, None, tm, tn), lambda b, i, j: (j, b, i, 0)),
        out_shape=jax.ShapeDtypeStruct((nt, G, R, tn), BF16),
        scratch_shapes=[h_scratch],
        compiler_params=_cparams(("arbitrary", "arbitrary", "arbitrary")),
        name="gate_up_prompt",
    )(x, mod, mod, mod, mod, gain, wgt, wut)


HGRN_TB = 2048


def _hgrn_prompt_kernel(q_ref, f_ref, i_ref, g_ref, lb_ref, gn_ref, o_ref, s_ref, st_scr, bk_scr):
    C, D = HGRN_CHUNK, HEAD_DIM
    t_idx = pl.program_id(2)
    consts = _hgrn_consts()
    g_norm = gn_ref[...]
    heads = range(HGRN_HP)
    lane = lambda hh: slice(hh * D, (hh + 1) * D)
    lbs = [lb_ref[:, lane(hh)] for hh in heads]

    @pl.when(t_idx == 0)
    def _():
        st_scr[...] = jnp.zeros_like(st_scr)

    def chunk_body(c, carry):
        rs = pl.ds(pl.multiple_of(c * C, C), C)
        outs, new_states = _hgrn_heads(
            [q_ref[rs, lane(hh)] for hh in heads], [f_ref[rs, lane(hh)] for hh in heads],
            [i_ref[rs, lane(hh)] for hh in heads], [g_ref[rs, lane(hh)] for hh in heads],
            lbs, g_norm, [st_scr[hh] for hh in heads], consts, bk_scr)
        for hh in heads:
            st_scr[hh] = new_states[hh]
            o_ref[rs, lane(hh)] = outs[hh].astype(o_ref.dtype)
        return carry

    lax.fori_loop(0, q_ref.shape[0] // C, chunk_body, 0)

    @pl.when(t_idx == pl.num_programs(2) - 1)
    def _():
        for hh in heads:
            s_ref[hh] = st_scr[hh].T


def _hgrn_prompt(proj32, proj16, lb, g_norm):
    _, B, T, W = proj32.shape
    assert W == HGRN_HP * HEAD_DIM
    seq = lambda slab: pl.BlockSpec((None, None, HGRN_TB, W),
                                    lambda b, h, t: (slab + h, b, t, 0))
    return pl.pallas_call(
        _hgrn_prompt_kernel,
        grid=(B, N_LIN_HEADS // HGRN_HP, T // HGRN_TB),
        in_specs=[
            seq(0), seq(0), seq(2), seq(4),
            pl.BlockSpec((1, W), lambda b, h, t: (0, h)),
            pl.BlockSpec((1, HEAD_DIM), lambda b, h, t: (0, 0)),
        ],
        out_specs=(
            pl.BlockSpec((None, HGRN_TB, W), lambda b, h, t: (b, t, h)),
            pl.BlockSpec((None, HGRN_HP, HEAD_DIM, HEAD_DIM), lambda b, h, t: (b, h, 0, 0)),
        ),
        out_shape=(
            jax.ShapeDtypeStruct((B, T, LIN_WIDTH), BF16),
            jax.ShapeDtypeStruct((B, N_LIN_HEADS, HEAD_DIM, HEAD_DIM), F32),
        ),
        scratch_shapes=[pltpu.VMEM((HGRN_HP, HEAD_DIM, HEAD_DIM), F32),
                        pltpu.VMEM((HGRN_HP, 2, HGRN_CHUNK, HEAD_DIM), F32)],
        compiler_params=_cparams(("parallel", "parallel", "arbitrary")),
        name="hgrn_prompt",
    )(proj16, proj32, proj16, proj16, lb, g_norm)


ATT_QB = 4


def _attn_prompt_kernel(sink_ref, q_ref, kvc_ref, kvp_ref, qn_ref, kn_ref,
                        o_ref, nk_ref, nv_ref):
    n = pl.program_id(1)
    W = WINDOW
    qi = lax.broadcasted_iota(jnp.int32, (W, 2 * W), 0)
    kj = lax.broadcasted_iota(jnp.int32, (W, 2 * W), 1)
    dist = W + qi - kj
    in_window = (dist >= 0) & (dist <= W)
    dist_in = jnp.where(in_window, dist.astype(F32), jnp.inf)
    dist_first = jnp.where(in_window & ((n > 0) | (kj >= W)), dist.astype(F32), jnp.inf)
    q_gain = qn_ref[...] * (ATT_SCALE * LOG2E)
    k_norm = kn_ref[...]

    new_k = []
    for h in range(N_KV_HEADS):
        hs = slice(h * HEAD_DIM, (h + 1) * HEAD_DIM)
        vs = slice(KV_WIDTH + h * HEAD_DIM, KV_WIDTH + (h + 1) * HEAD_DIM)
        kc = _rms_rows(kvc_ref[:, hs], k_norm)
        kp = _rms_rows(kvp_ref[:, hs], k_norm)
        new_k.append(kc[(ATT_QB - 1) * W:, :])
        k_all = jnp.concatenate([kp, kc], axis=0).astype(BF16)
        v_all = jnp.concatenate([kvp_ref[:, vs], kvc_ref[:, vs]], axis=0).astype(BF16)

        for blk in range(ATT_QB):
            rows = slice(blk * W, (blk + 1) * W)
            k2 = k_all[blk * W:(blk + 2) * W, :]
            v2 = v_all[blk * W:(blk + 2) * W, :]
            dist_m = dist_first if blk == 0 else dist_in
            for g in range(GROUP):
                a = h * GROUP + g
                cs = slice(a * HEAD_DIM, (a + 1) * HEAD_DIM)
                qh = _rms_rows(q_ref[h, rows, g * HEAD_DIM:(g + 1) * HEAD_DIM].astype(F32),
                               q_gain).astype(BF16)
                s = lax.dot_general(qh, k2, (((1,), (1,)), ((), ())), preferred_element_type=F32)
                s = s + (-ALIBI_SLOPES[a] * LOG2E) * dist_m
                sink = sink_ref[a] * LOG2E
                m = jnp.maximum(jnp.max(s, axis=-1, keepdims=True), sink)
                p = jnp.exp2(s - m)
                den = jnp.sum(p, axis=-1, keepdims=True) + jnp.exp2(sink - m)
                o = jnp.dot(p.astype(BF16), v2, preferred_element_type=F32) * (1.0 / den)
                o_ref[rows, cs] = o.astype(o_ref.dtype)

    @pl.when(n == pl.num_programs(1) - 1)
    def _():
        for h in range(N_KV_HEADS):
            hs = slice(h * HEAD_DIM, (h + 1) * HEAD_DIM)
            nk_ref[:, hs] = new_k[h]
            nv_ref[:, hs] = kvc_ref[(ATT_QB - 1) * W:,
                                    KV_WIDTH + h * HEAD_DIM:KV_WIDTH + (h + 1) * HEAD_DIM]


def _attn_prompt(proj32, proj16, sinks, q_norm, k_norm):
    _, B, T, W = proj32.shape
    assert W == GROUP * HEAD_DIM == 2 * KV_WIDTH
    rows = ATT_QB * WINDOW
    q_slab = 6 // N_KV_HEADS
    kv_slab = 2
    prev = lambda n: jnp.maximum(ATT_QB * n - 1, 0)
    return pl.pallas_call(
        _attn_prompt_kernel,
        grid=(B, T // rows),
        in_specs=[
            pl.BlockSpec(memory_space=pltpu.SMEM),
            pl.BlockSpec((N_KV_HEADS, None, rows, W), lambda b, n: (q_slab, b, n, 0)),
            pl.BlockSpec((None, None, rows, W), lambda b, n: (kv_slab, b, n, 0)),
            pl.BlockSpec((None, None, WINDOW, W), lambda b, n: (kv_slab, b, prev(n), 0)),
            pl.BlockSpec((1, HEAD_DIM), lambda b, n: (0, 0)),
            pl.BlockSpec((1, HEAD_DIM), lambda b, n: (0, 0)),
        ],
        out_specs=(
            pl.BlockSpec((None, rows, ATT_WIDTH), lambda b, n: (b, n, 0)),
            pl.BlockSpec((None, WINDOW, KV_WIDTH), lambda b, n: (b, 0, 0)),
            pl.BlockSpec((None, WINDOW, KV_WIDTH), lambda b, n: (b, 0, 0)),
        ),
        out_shape=(
            jax.ShapeDtypeStruct((B, T, ATT_WIDTH), BF16),
            jax.ShapeDtypeStruct((B, WINDOW, KV_WIDTH), F32),
            jax.ShapeDtypeStruct((B, WINDOW, KV_WIDTH), F32),
        ),
        compiler_params=_cparams(("parallel", "arbitrary")),
        name="attn_prompt",
    )(sinks, proj16, proj32, proj32, q_norm, k_norm)


def _row_to_col(x_row, eye):
    return jnp.sum(jnp.where(eye, x_row, 0.0), axis=1, keepdims=True)


MIX_RB = 4


def _mix_sample_kernel(sink_ref, p_ref, s_ref, ck_ref, cv_ref, lb_ref,
                       gn_ref, qn_ref, kn_ref, o_ref, ns_ref, nk_ref, nv_ref):
    for r in range(p_ref.shape[0]):
        _mix_one_request(sink_ref, p_ref.at[r], s_ref.at[r], ck_ref.at[r], cv_ref.at[r], lb_ref,
                         gn_ref, qn_ref, kn_ref, o_ref.at[r], ns_ref.at[r], nk_ref.at[r],
                         nv_ref.at[r])


def _mix_one_request(sink_ref, p_ref, s_ref, ck_ref, cv_ref, lb_ref,
                     gn_ref, qn_ref, kn_ref, o_ref, ns_ref, nk_ref, nv_ref):
    W = WINDOW
    er = lax.broadcasted_iota(jnp.int32, (HEAD_DIM, HEAD_DIM), 0)
    ec = lax.broadcasted_iota(jnp.int32, (HEAD_DIM, HEAD_DIM), 1)
    eye = er == ec
    g_norm = gn_ref[...]
    q_norm = qn_ref[...]
    k_norm = kn_ref[...]

    def cols(off, width=HEAD_DIM):
        return p_ref[:, off:off + width]

    for h in range(N_LIN_HEADS):
        c0 = h * HEAD_DIM
        q = _silu(cols(c0))
        lb = lb_ref[:, c0:c0 + HEAD_DIM]
        _, fp = _forget_gate(cols(LIN_WIDTH + c0), lb)
        f_col = _row_to_col(lb + fp, eye)
        k_col = 1.0 - f_col
        v = cols(2 * LIN_WIDTH + c0)
        s_new = s_ref[h] * f_col + k_col * v
        ns_ref[h] = s_new
        q8 = jnp.broadcast_to(q, (8, HEAD_DIM)).astype(BF16)
        o = jnp.dot(q8, s_new.astype(BF16), preferred_element_type=F32)[0:1, :]
        o = _rms_rows(o, g_norm) * _silu(cols(3 * LIN_WIDTH + c0))
        o_ref[:, c0:c0 + HEAD_DIM] = o

    row8 = lax.broadcasted_iota(jnp.int32, (8, 1), 0)
    row8_k = lax.broadcasted_iota(jnp.int32, (8, HEAD_DIM), 0)
    lane = lax.broadcasted_iota(jnp.int32, (8, W), 1)
    dist_c = (W - lane).astype(F32)
    rows_w = lax.broadcasted_iota(jnp.int32, (W, HEAD_DIM), 0)
    for h in range(N_KV_HEADS):
        hs = slice(h * HEAD_DIM, (h + 1) * HEAD_DIM)
        k_new = _rms_rows(cols(OFF_K_A + h * HEAD_DIM), k_norm)
        v_new = cols(OFF_V_A + h * HEAD_DIM)
        kc = ck_ref[:, hs]
        vc = cv_ref[:, hs]
        nk_ref[:, hs] = jnp.where(rows_w == W - 1, k_new, pltpu.roll(kc, W - 1, 0))
        nv_ref[:, hs] = jnp.where(rows_w == W - 1, v_new, pltpu.roll(vc, W - 1, 0))

        q4 = jnp.zeros((8, HEAD_DIM), F32)
        slope = jnp.zeros((8, 1), F32)
        sink = jnp.zeros((8, 1), F32)
        for g in range(GROUP):
            a = h * GROUP + g
            qg = _rms_rows(cols(OFF_Q_A + a * HEAD_DIM), q_norm)
            q4 = jnp.where(row8_k == g, qg, q4)
            slope = jnp.where(row8 == g, ALIBI_SLOPES[a], slope)
            sink = jnp.where(row8 == g, sink_ref[a], sink)
        q4b = q4.astype(BF16)
        s_c = lax.dot_general(q4b, kc.astype(BF16), (((1,), (1,)), ((), ())),
                              preferred_element_type=F32)
        s_c = s_c * ATT_SCALE - slope * dist_c
        s_n = jnp.sum(q4 * k_new, axis=-1, keepdims=True) * ATT_SCALE
        m = jnp.maximum(jnp.maximum(jnp.max(s_c, axis=-1, keepdims=True), s_n), sink)
        p_c = jnp.exp(s_c - m)
        p_n = jnp.exp(s_n - m)
        den = jnp.sum(p_c, axis=-1, keepdims=True) + p_n + jnp.exp(sink - m)
        o = jnp.dot(p_c.astype(BF16), vc.astype(BF16), preferred_element_type=F32)
        o = (o + p_n * v_new) / den
        for g in range(GROUP):
            a = h * GROUP + g
            o_ref[:, LIN_WIDTH + a * HEAD_DIM:LIN_WIDTH + (a + 1) * HEAD_DIM] = o[g:g + 1, :]


def _mix_sample(proj, state, cache_k, cache_v, layer, sinks, lb, g_norm, q_norm, k_norm):
    B = proj.shape[0]
    rb = MIX_RB
    vec = pl.BlockSpec((1, HEAD_DIM), lambda b: (0, 0))
    lbs = pl.BlockSpec((1, LIN_WIDTH), lambda b: (0, 0))
    cache_in = pl.BlockSpec((None, rb, WINDOW, KV_WIDTH), lambda b: (layer, b, 0, 0))
    st_in = pl.BlockSpec((None, rb, N_LIN_HEADS, HEAD_DIM, HEAD_DIM),
                         lambda b: (layer, b, 0, 0, 0))
    cache = pl.BlockSpec((rb, WINDOW, KV_WIDTH), lambda b: (b, 0, 0))
    st = pl.BlockSpec((rb, N_LIN_HEADS, HEAD_DIM, HEAD_DIM), lambda b: (b, 0, 0, 0))
    return pl.pallas_call(
        _mix_sample_kernel,
        grid=(B // rb,),
        in_specs=[
            pl.BlockSpec(memory_space=pltpu.SMEM),
            pl.BlockSpec((rb, 1, IN_WIDTH), lambda b: (b, 0, 0)),
            st_in, cache_in, cache_in, lbs, vec, vec, vec,
        ],
        out_specs=(
            pl.BlockSpec((rb, 1, MIX_WIDTH), lambda b: (b, 0, 0)),
            st, cache, cache,
        ),
        out_shape=(
            jax.ShapeDtypeStruct((B, 1, MIX_WIDTH), F32),
            jax.ShapeDtypeStruct(state.shape[1:], F32),
            jax.ShapeDtypeStruct((B, WINDOW, KV_WIDTH), F32),
            jax.ShapeDtypeStruct((B, WINDOW, KV_WIDTH), F32),
        ),
        compiler_params=_cparams(("parallel",)),
        name="mix_sample",
    )(sinks, proj, state, cache_k, cache_v, lb, g_norm, q_norm, k_norm)


TILE_N = 512
P_TM = 1024
P_TM_OUT = 2048
F32_SLABS = (LIN_WIDTH // TILE_N, LIN_WIDTH // TILE_N + 1, OFF_K_A // TILE_N)


def kernel(x_prompt, x_sample, cache_k, cache_v, state_hgrn, c_prompt, c_sample, lower_bounds,
           w_ada, b_ada, norm_mix, w_in, q_norm, k_norm, attn_sinks, g_norm, w_out, norm_ffn,
           w_gate, w_up, w_down):
    B, T, _ = x_prompt.shape
    BS = x_sample.shape[0]

    lb_all = _lower_bounds(lower_bounds)

    c_all = jnp.concatenate(
        [c_prompt, c_sample, jnp.zeros((MOD_ROWS - B - BS, D_MODEL), F32)], axis=0)
    mod = _modulation(c_all, w_ada, b_ada)

    cache_k2 = cache_k.reshape(DEPTH, BS, WINDOW, KV_WIDTH)
    cache_v2 = cache_v.reshape(DEPTH, BS, WINDOW, KV_WIDTH)

    yp = x_prompt
    ys = x_sample.reshape(1, BS, D_MODEL)
    pk, pv, ps, sk, sv, ss = [], [], [], [], [], []
    for l in range(DEPTH):
        mod_p = mod[l, :B].reshape(B, 1, 6 * D_MODEL)
        mod_s = mod[l, B:B + BS].reshape(1, BS, 6 * D_MODEL)
        gain_mix = norm_mix[l].reshape(1, D_MODEL)
        gain_ffn = norm_ffn[l].reshape(1, D_MODEL)
        lb = lb_all[l].reshape(1, LIN_WIDTH)
        gn = g_norm[l].reshape(1, HEAD_DIM)
        qn = q_norm[l].reshape(1, HEAD_DIM)
        kn = k_norm[l].reshape(1, HEAD_DIM)
        sinks = attn_sinks[l].astype(F32)

        proj_s, w_in_b = _norm_matmul(ys, mod_s, 0, gain_mix, w_in, l, BS, TILE_N)
        mixed_s, s_s, k_s, v_s = _mix_sample(
            proj_s.reshape(BS, 1, IN_WIDTH), state_hgrn, cache_k2, cache_v2, l,
            sinks, lb, gn, qn, kn)
        ys, w_out_b = _matmul_residual(mixed_s.reshape(1, BS, MIX_WIDTH), w_out, l, ys, mod_s,
                                       2 * D_MODEL, BS, TILE_N)
        ff_s, w_gate_b, w_up_b = _norm_gate_up(ys, mod_s, 3, gain_ffn, w_gate, w_up, l, BS, TILE_N)
        ys, w_down_b = _matmul_residual(ff_s, w_down, l, ys, mod_s, 5 * D_MODEL, BS, TILE_N)
        sk.append(k_s.reshape(BS, WINDOW, N_KV_HEADS, HEAD_DIM))
        sv.append(v_s.reshape(BS, WINDOW, N_KV_HEADS, HEAD_DIM))
        ss.append(s_s)

        proj32, proj16 = _in_proj_prompt(yp, mod_p, gain_mix, w_in_b, F32_SLABS, P_TM)
        o_l, s_p = _hgrn_prompt(proj32, proj16, lb, gn)
        o_a, k_p, v_p = _attn_prompt(proj32, proj16, sinks, qn, kn)
        yp = _res_matmul_prompt([o_l, o_a], [(w_out_b, 0), (w_out_b, 1)], yp, mod_p, 2,
                                P_TM_OUT, "out_proj")
        ff = _gate_up_prompt(yp, mod_p, 3, gain_ffn, w_gate_b, w_up_b, P_TM)
        yp = _res_matmul_prompt([ff], [w_down_b], yp, mod_p, 5, P_TM, "down_proj")
        pk.append(k_p.reshape(B, WINDOW, N_KV_HEADS, HEAD_DIM))
        pv.append(v_p.reshape(B, WINDOW, N_KV_HEADS, HEAD_DIM))
        ps.append(s_p)

    return (yp, ys.reshape(BS, 1, D_MODEL), jnp.stack(pk), jnp.stack(pv), jnp.stack(ps),
            jnp.stack(sk), jnp.stack(sv), jnp.stack(ss))
```

```python
import functools
import math

import jax
import jax.numpy as jnp
from jax import lax
from jax.experimental import pallas as pl
from jax.experimental.pallas import tpu as pltpu

F32 = jnp.float32
BF16 = jnp.bfloat16

D_MODEL = 2048
DEPTH = 4
HEAD_DIM = 128
N_LIN_HEADS = 8
LIN_WIDTH = N_LIN_HEADS * HEAD_DIM
N_ATT_HEADS = 8
N_KV_HEADS = 2
GROUP = N_ATT_HEADS // N_KV_HEADS
ATT_WIDTH = N_ATT_HEADS * HEAD_DIM
KV_WIDTH = N_KV_HEADS * HEAD_DIM
MIX_WIDTH = LIN_WIDTH + ATT_WIDTH
WINDOW = 128
D_FF = 5632
IN_WIDTH = 4 * LIN_WIDTH + ATT_WIDTH + 2 * KV_WIDTH
EPS = 1e-6
ATT_SCALE = HEAD_DIM ** -0.5
ALIBI_SLOPES = tuple(2.0 ** (-8.0 * (a + 1) / N_ATT_HEADS) for a in range(N_ATT_HEADS))

OFF_Q_A = 4 * LIN_WIDTH
OFF_K_A = OFF_Q_A + ATT_WIDTH
OFF_V_A = OFF_K_A + KV_WIDTH

MOD_ROWS = 40
HGRN_CHUNK = 128
HGRN_SUB = 8
LOG2E = math.log2(math.e)

VMEM_LIMIT = 56 * 1024 * 1024


def _cparams(sem):
    return pltpu.CompilerParams(dimension_semantics=sem, vmem_limit_bytes=VMEM_LIMIT)


def _silu(x):
    return x / (1.0 + jnp.exp(-x))


def _rms_rows(x, g):
    ms = jnp.mean(x * x, axis=-1, keepdims=True)
    return x * lax.rsqrt(ms + EPS) * g


def _forget_gate(f_raw, lb):
    t = jnp.exp(-jnp.abs(f_raw))
    inv = 1.0 / (1.0 + t)
    sig = jnp.where(f_raw >= 0.0, inv, t * inv)
    return t, (1.0 - lb) * sig


def _log_forget(f_raw, lb):
    t, fp = _forget_gate(f_raw, lb)
    log_sig = jnp.minimum(f_raw, 0.0) - jnp.log(1.0 + t)
    return jnp.where(lb > 0.0, jnp.log(lb + fp), log_sig), fp


def _lb_kernel(lb_ref, o_ref):
    x = lb_ref[...]
    m = jnp.max(x, axis=0, keepdims=True)
    e = jnp.exp(x - m)
    p = e / jnp.sum(e, axis=0, keepdims=True)
    acc = jnp.zeros_like(p[0:1])
    for l in range(DEPTH):
        if l > 0:
            acc = acc + p[l:l + 1]
        o_ref[l:l + 1, :] = acc


def _lower_bounds(lower_bounds):
    return pl.pallas_call(
        _lb_kernel,
        out_shape=jax.ShapeDtypeStruct((DEPTH, LIN_WIDTH), F32),
        name="lower_bounds",
    )(lower_bounds.astype(F32))


ADA_TN = 1024


def _ada_kernel(c_ref, w_ref, b_ref, o_ref):
    s = _silu(c_ref[...]).astype(BF16)
    acc = jnp.dot(s, w_ref[...].astype(BF16), preferred_element_type=F32)
    o_ref[...] = acc + b_ref[...]


def _modulation(c_all, w_ada, b_ada):
    n = 6 * D_MODEL
    return pl.pallas_call(
        _ada_kernel,
        grid=(DEPTH, n // ADA_TN),
        in_specs=[
            pl.BlockSpec((MOD_ROWS, D_MODEL), lambda l, j: (0, 0)),
            pl.BlockSpec((None, D_MODEL, ADA_TN), lambda l, j: (l, 0, j)),
            pl.BlockSpec((None, 1, ADA_TN), lambda l, j: (l, 0, j)),
        ],
        out_specs=pl.BlockSpec((None, MOD_ROWS, ADA_TN), lambda l, j: (l, 0, j)),
        out_shape=jax.ShapeDtypeStruct((DEPTH, MOD_ROWS, n), F32),
        compiler_params=_cparams(("parallel", "parallel")),
        name="modulation",
    )(c_all, w_ada, b_ada.reshape(DEPTH, 1, n))


NORM_ROWS = 16
NORM_UNROLL = 4


def _modulated_norm_into(h_scr, x_ref, sh_ref, sc_ref, g_ref):
    rows = x_ref.shape[0]
    ch = min(rows, NORM_ROWS)
    per_row = sc_ref.shape[0] != 1
    gain = g_ref[...]
    if not per_row:
        gain = gain * (1.0 + sc_ref[...])
        shift = sh_ref[...]

    def body(c, carry):
        rs = pl.ds(pl.multiple_of(c * ch, ch), ch)
        x = x_ref[rs, :]
        inv = lax.rsqrt(jnp.mean(x * x, axis=-1, keepdims=True) + EPS)
        if per_row:
            h = (x * inv * gain) * (1.0 + sc_ref[rs, :]) + sh_ref[rs, :]
        else:
            h = (x * inv) * gain + shift
        h_scr[rs, :] = h.astype(BF16)
        return carry

    lax.fori_loop(0, rows // ch, body, 0, unroll=min(NORM_UNROLL, rows // ch))


def _w_spec(w, layer, tn):
    return pl.BlockSpec((None, w.shape[1], tn), lambda b, i, j: (layer, 0, j))


def _wb_out(w, tn):
    rows, n = w.shape[-2:]
    return (pl.BlockSpec((None, rows, tn), lambda b, i, j: (j, 0, 0)),
            jax.ShapeDtypeStruct((n // tn, rows, tn), BF16))


def _norm_mm_kernel(x_ref, sh_ref, sc_ref, g_ref, w_ref, o_ref, wb_ref, h_scr):
    @pl.when(pl.program_id(2) == 0)
    def _():
        _modulated_norm_into(h_scr, x_ref, sh_ref, sc_ref, g_ref)

    w = w_ref[...].astype(BF16)
    wb_ref[...] = w
    o_ref[...] = jnp.dot(h_scr[...], w, preferred_element_type=F32)


def _mod_specs(mod, mod_col, tm):
    if mod.shape[1] == 1:
        return [pl.BlockSpec((None, 1, D_MODEL), lambda b, i, j, c=mod_col + k: (b, 0, c))
                for k in range(2)]
    return [pl.BlockSpec((None, tm, D_MODEL), lambda b, i, j, c=mod_col + k: (b, i, c))
            for k in range(2)]


def _lookup(j, table):
    out = table[-1]
    for k in range(len(table) - 2, -1, -1):
        out = jnp.where(j == k, table[k], out)
    return out


def _norm_matmul(x, mod, mod_col, gain, w, layer, tm, tn):
    G, R, _ = x.shape
    nt = w.shape[-1] // tn
    assert G * (R // tm) == 1
    wb_spec, wb_shape = _wb_out(w, tn)
    out_specs = [pl.BlockSpec((None, tm, tn), lambda b, i, j: (b, i, j)), wb_spec]
    out_shape = [jax.ShapeDtypeStruct((G, R, nt * tn), F32), wb_shape]
    return pl.pallas_call(
        _norm_mm_kernel,
        grid=(G, R // tm, nt),
        in_specs=[pl.BlockSpec((None, tm, D_MODEL), lambda b, i, j: (b, i, 0))]
        + _mod_specs(mod, mod_col, tm)
        + [pl.BlockSpec((1, D_MODEL), lambda b, i, j: (0, 0)), _w_spec(w, layer, tn)],
        out_specs=out_specs,
        out_shape=out_shape,
        scratch_shapes=[pltpu.VMEM((tm, D_MODEL), BF16)],
        compiler_params=_cparams(("parallel", "parallel", "arbitrary")),
        name="norm_in_proj",
    )(x, mod, mod, gain, w)


def _norm_gu_kernel(x_ref, sh_ref, sc_ref, g_ref, wg_ref, wu_ref, o_ref, wgb_ref, wub_ref, h_scr):
    @pl.when(pl.program_id(2) == 0)
    def _():
        _modulated_norm_into(h_scr, x_ref, sh_ref, sc_ref, g_ref)

    h = h_scr[...]
    wg = wg_ref[...].astype(BF16)
    wu = wu_ref[...].astype(BF16)
    wgb_ref[...] = wg
    wub_ref[...] = wu
    gate = jnp.dot(h, wg, preferred_element_type=F32)
    up = jnp.dot(h, wu, preferred_element_type=F32)
    o_ref[...] = (_silu(gate) * up).astype(o_ref.dtype)


def _norm_gate_up(x, mod, mod_col, gain, wg, wu, layer, tm, tn):
    G, R, _ = x.shape
    nt = wg.shape[-1] // tn
    assert G * (R // tm) == 1
    wb_spec, wb_shape = _wb_out(wg, tn)
    return pl.pallas_call(
        _norm_gu_kernel,
        grid=(G, R // tm, nt),
        in_specs=[pl.BlockSpec((None, tm, D_MODEL), lambda b, i, j: (b, i, 0))]
        + _mod_specs(mod, mod_col, tm)
        + [pl.BlockSpec((1, D_MODEL), lambda b, i, j: (0, 0)),
           _w_spec(wg, layer, tn), _w_spec(wu, layer, tn)],
        out_specs=[pl.BlockSpec((None, tm, tn), lambda b, i, j: (b, i, j)), wb_spec, wb_spec],
        out_shape=[jax.ShapeDtypeStruct((G, R, nt * tn), BF16), wb_shape, wb_shape],
        scratch_shapes=[pltpu.VMEM((tm, D_MODEL), BF16)],
        compiler_params=_cparams(("parallel", "parallel", "arbitrary")),
        name="norm_gate_up",
    )(x, mod, mod, gain, wg, wu)


def _res_matmul_kernel(*refs, n_a):
    a_refs, w_refs = refs[:n_a], refs[n_a:2 * n_a]
    x_ref, gt_ref, o_ref = refs[2 * n_a:]
    acc = None
    for a_ref, w_ref in zip(a_refs, w_refs):
        slabs = [a_ref[s] for s in range(a_ref.shape[0])] if len(a_ref.shape) == 3 else [a_ref[...]]
        ts = slabs[0].shape[1]
        for s, a in enumerate(slabs):
            part = jnp.dot(a, w_ref[s * ts:(s + 1) * ts, :], preferred_element_type=F32)
            acc = part if acc is None else acc + part
    o_ref[...] = x_ref[...] + gt_ref[...] * acc


def _res_matmul_prompt(a_list, w_list, x, mod, gate_col, tm, name):
    G, R, _ = x.shape
    a_specs, w_specs, ws = [], [], []
    for a, w in zip(a_list, w_list):
        w, row_blk = w if isinstance(w, tuple) else (w, 0)
        if a.ndim == 4:
            S, _, _, ts = a.shape
            a_specs.append(pl.BlockSpec((S, None, tm, ts), lambda b, i, j: (0, b, i, 0)))
            k = S * ts
        else:
            k = a.shape[2]
            a_specs.append(pl.BlockSpec((None, tm, k), lambda b, i, j: (b, i, 0)))
        nt, _, tn = w.shape
        w_specs.append(pl.BlockSpec((None, k, tn), lambda b, i, j, r=row_blk: (j, r, 0)))
        ws.append(w)
    tile = pl.BlockSpec((None, tm, tn), lambda b, i, j: (b, i, j))
    return pl.pallas_call(
        functools.partial(_res_matmul_kernel, n_a=len(a_list)),
        grid=(G, R // tm, nt),
        in_specs=a_specs + w_specs
        + [tile, pl.BlockSpec((None, 1, tn), lambda b, i, j: (b, 0, gate_col * nt + j))],
        out_specs=tile,
        out_shape=jax.ShapeDtypeStruct((G, R, D_MODEL), F32),
        compiler_params=_cparams(("parallel", "parallel", "arbitrary")),
        name=name,
    )(*a_list, *ws, x, mod)


def _mm_res_kernel(a_ref, w_ref, x_ref, gt_ref, o_ref, wb_ref):
    w = w_ref[...].astype(BF16)
    wb_ref[...] = w
    acc = jnp.dot(a_ref[...].astype(BF16), w, preferred_element_type=F32)
    o_ref[...] = x_ref[...] + gt_ref[...] * acc


def _matmul_residual(a, w, layer, x, mod, mod_off, tm, tn):
    G, R, K = a.shape
    assert G * (R // tm) == 1 and mod.shape[1] == R
    gcol = mod_off // tn
    wb_spec, wb_shape = _wb_out(w, tn)
    return pl.pallas_call(
        _mm_res_kernel,
        grid=(G, R // tm, D_MODEL // tn),
        in_specs=[
            pl.BlockSpec((None, tm, K), lambda b, i, j: (b, i, 0)),
            _w_spec(w, layer, tn),
            pl.BlockSpec((None, tm, tn), lambda b, i, j: (b, i, j)),
            pl.BlockSpec((None, tm, tn), lambda b, i, j: (b, i, gcol + j)),
        ],
        out_specs=[pl.BlockSpec((None, tm, tn), lambda b, i, j: (b, i, j)), wb_spec],
        out_shape=[jax.ShapeDtypeStruct((G, R, D_MODEL), F32), wb_shape],
        compiler_params=_cparams(("parallel", "parallel", "arbitrary")),
        name="matmul_residual",
    )(a, w, x, mod)


def _split3_bf16(x):
    hi = x.astype(BF16)
    r1 = x - hi.astype(F32)
    mid = r1.astype(BF16)
    lo = (r1 - mid.astype(F32)).astype(BF16)
    return hi, mid, lo


HGRN_HP = 4
HGRN_SEP_MIN = 32


def _hgrn_consts():
    C, P, D = HGRN_CHUNK, HGRN_SUB, HEAD_DIM
    row = lax.broadcasted_iota(jnp.int32, (C, C), 0)
    col = lax.broadcasted_iota(jnp.int32, (C, C), 1)
    tri = (row >= col).astype(BF16)
    sel_r = lax.broadcasted_iota(jnp.int32, (P * D, C), 0)
    sel_c = lax.broadcasted_iota(jnp.int32, (P * D, C), 1)
    psh = P.bit_length() - 1
    sel = ((sel_c & (P - 1)) == (sel_r >> (D.bit_length() - 1))).astype(BF16)
    keep_diag = ((col >> psh) == (row >> psh)) & ((col & (P - 1)) <= (row & (P - 1)))
    levels = []
    L = C // 2
    while L >= P:
        levels.append(L)
        L //= 2
    keep_level = {}
    for L in levels:
        if L < HGRN_SEP_MIN:
            sh = L.bit_length() - 1
            keep_level[L] = (((row >> sh) & 1) == 1) & ((col >> sh) == (row >> sh) - 1)
    return tri, sel, keep_diag, levels, keep_level


def _hgrn_heads(q_raw, f_raw, i_raw, g_raw, lbs, g_norm, states, consts, bk_scr):
    C, P, D = HGRN_CHUNK, HGRN_SUB, HEAD_DIM
    tri, sel, keep_diag, levels, keep_level = consts
    n = len(q_raw)
    qs, ks, vs, parts = [], [], [], []
    for hh in range(n):
        logf, fp = _log_forget(f_raw[hh], lbs[hh])
        qs.append(_silu(q_raw[hh].astype(F32)))
        ks.append((1.0 - lbs[hh]) - fp)
        vs.append(i_raw[hh].astype(BF16))
        parts.extend(_split3_bf16(logf * LOG2E))
    cum = jnp.dot(tri, jnp.concatenate(parts, axis=1), preferred_element_type=F32)

    outs, new_states = [], []
    for hh in range(n):
        q, k, v16, st = qs[hh], ks[hh], vs[hh], states[hh]
        b = (cum[:, (3 * hh) * D:(3 * hh + 1) * D] + cum[:, (3 * hh + 1) * D:(3 * hh + 2) * D]
             + cum[:, (3 * hh + 2) * D:(3 * hh + 3) * D])
        b_end = b[C - 1:C, :]
        o = lax.dot_general((q * jnp.exp2(b)).astype(BF16), st.astype(BF16),
                            (((1,), (1,)), ((), ())), preferred_element_type=F32)

        q_sep, k_sep, a_masked = [], [], []
        for L in levels:
            q_rows, k_rows = [], []
            for p in range(C // (2 * L)):
                lo, mid, hi = 2 * L * p, 2 * L * p + L, 2 * L * (p + 1)
                ref = b[mid - 1:mid, :]
                zf = jnp.zeros((L, D), F32)
                kk = jnp.concatenate([k[lo:mid] * jnp.exp2(ref - b[lo:mid]), zf], axis=0)
                qq = jnp.concatenate([zf, q[mid:hi] * jnp.exp2(b[mid:hi] - ref)], axis=0)
                q_rows.append(qq.astype(BF16))
                k_rows.append(kk.astype(BF16))
            if L >= HGRN_SEP_MIN:
                for p in range(len(q_rows)):
                    above = jnp.zeros((2 * L * p, D), BF16)
                    below = jnp.zeros((C - 2 * L * (p + 1), D), BF16)
                    for rows_p, dst in ((q_rows[p], q_sep), (k_rows[p], k_sep)):
                        dst.append(jnp.concatenate(
                            [y for y in (above, rows_p, below) if y.shape[0] > 0], axis=0))
            else:
                a_l = lax.dot_general(jnp.concatenate(q_rows, axis=0),
                                      jnp.concatenate(k_rows, axis=0),
                                      (((1,), (1,)), ((), ())), preferred_element_type=F32)
                a_masked.append((keep_level[L], a_l))
        a = lax.dot_general(jnp.concatenate(q_sep, axis=1), jnp.concatenate(k_sep, axis=1),
                            (((1,), (1,)), ((), ())), preferred_element_type=F32)
        for keep, a_l in a_masked:
            a = jnp.where(keep, a_l, a)

        bk_scr[hh, 0] = b
        bk_scr[hh, 1] = k
        zs = []
        for s in range(P):
            bs, ksb = [jnp.concatenate(
                [bk_scr[hh, w, pl.ds(m * P + s, P, stride=0), :] for m in range(C // P)], axis=0)
                for w in range(2)]
            e = jnp.exp2(jnp.minimum(b - bs, 0.0))
            zs.append((q * ksb * e).astype(BF16))
        a_d = jnp.dot(jnp.concatenate(zs, axis=1), sel, preferred_element_type=F32)
        a = jnp.where(keep_diag, a_d, a)

        o = o + jnp.dot(a.astype(BF16), v16, preferred_element_type=F32)

        k_end = (k * jnp.exp2(b_end - b)).astype(BF16)
        upd = lax.dot_general(v16, k_end, (((0,), (0,)), ((), ())), preferred_element_type=F32)
        new_states.append(st * jnp.exp2(b_end) + upd)
        outs.append(_rms_rows(o, g_norm) * _silu(g_raw[hh].astype(F32)))
    return outs, new_states


def _lookahead_slot(x_ref, sh0_ref, sc0_ref, g_ref, h_scr, n_i):
    tile = pl.program_id(0) * n_i + pl.program_id(1)

    @pl.when((tile == 0) & (pl.program_id(2) == 0))
    def _():
        _modulated_norm_into(h_scr.at[0], x_ref, sh0_ref, sc0_ref, g_ref)

    return tile % 2


def _lookahead_slab(x_ref, shn_ref, scn_ref, g_ref, h_scr, cur, n_steps):
    j = pl.program_id(2)
    tm = x_ref.shape[0]
    per_step = -(-tm // ((n_steps - 1) * NORM_ROWS)) * NORM_ROWS
    gain = g_ref[...] * (1.0 + scn_ref[...])
    shift = shn_ref[...]
    r0 = jnp.clip((j - 1) * per_step, 0, tm - per_step)
    nxt = h_scr.at[1 - cur]
    for k in range(per_step // NORM_ROWS):
        rs = pl.ds(pl.multiple_of(r0 + k * NORM_ROWS, NORM_ROWS), NORM_ROWS)
        x = x_ref[rs, :]
        inv = lax.rsqrt(jnp.mean(x * x, axis=-1, keepdims=True) + EPS)
        nxt[rs, :] = ((x * inv) * gain + shift).astype(BF16)


def _lookahead_specs(x, mod, mod_col, tm, n_j):
    G, R, _ = x.shape
    n_i = R // tm
    last = G * n_i - 1

    def x_tile(b, i, j):
        t = jnp.minimum(b * n_i + i + (j > 0).astype(jnp.int32), last)
        return t // n_i, t % n_i, 0

    nxt_b = lambda b, i: jnp.minimum(b * n_i + i + 1, last) // n_i
    specs = [pl.BlockSpec((None, tm, D_MODEL), x_tile)]
    specs += [pl.BlockSpec((None, 1, D_MODEL), lambda b, i, j, c=mod_col + k: (0, 0, c))
              for k in range(2)]
    specs += [pl.BlockSpec((None, 1, D_MODEL), lambda b, i, j, c=mod_col + k: (nxt_b(b, i), 0, c))
              for k in range(2)]
    return specs, pltpu.VMEM((2, tm, D_MODEL), BF16), n_i


def _in_proj_kernel(x_ref, sh0_ref, sc0_ref, shn_ref, scn_ref, g_ref, w_ref, o32_ref, o16_ref,
                    h_scr, *, n_i, n_steps):
    cur = _lookahead_slot(x_ref, sh0_ref, sc0_ref, g_ref, h_scr, n_i)
    acc = jnp.dot(h_scr[cur], w_ref[...], preferred_element_type=F32)
    o32_ref[...] = acc
    o16_ref[...] = acc.astype(BF16)
    _lookahead_slab(x_ref, shn_ref, scn_ref, g_ref, h_scr, cur, n_steps)


def _in_proj_prompt(x, mod, gain, wt, f32_slabs, tm):
    G, R, _ = x.shape
    nt, _, tn = wt.shape
    n32 = len(f32_slabs)
    n16 = nt - n32
    order = tuple(f32_slabs) + tuple(s for s in range(nt) if s not in f32_slabs)
    x_specs, h_scratch, n_i = _lookahead_specs(x, mod, 0, tm, nt)
    return pl.pallas_call(
        functools.partial(_in_proj_kernel, n_i=n_i, n_steps=nt),
        grid=(G, R // tm, nt),
        in_specs=x_specs
        + [pl.BlockSpec((1, D_MODEL), lambda b, i, j: (0, 0)),
           pl.BlockSpec((None, D_MODEL, tn), lambda b, i, j: (_lookup(j, order), 0, 0))],
        out_specs=[
            pl.BlockSpec((None, None, tm, tn), lambda b, i, j: (jnp.minimum(j, n32), b, i, 0)),
            pl.BlockSpec((None, None, tm, tn),
                         lambda b, i, j: (jnp.where(j < n32, n16, j - n32), b, i, 0)),
        ],
        out_shape=[jax.ShapeDtypeStruct((n32 + 1, G, R, tn), F32),
                   jax.ShapeDtypeStruct((n16 + 1, G, R, tn), BF16)],
        scratch_shapes=[h_scratch],
        compiler_params=_cparams(("arbitrary", "arbitrary", "arbitrary")),
        name="in_proj_prompt",
    )(x, mod, mod, mod, mod, gain, wt)


def _gate_up_kernel(x_ref, sh0_ref, sc0_ref, shn_ref, scn_ref, g_ref, wg_ref, wu_ref, o_ref,
                    h_scr, *, n_i, n_steps):
    cur = _lookahead_slot(x_ref, sh0_ref, sc0_ref, g_ref, h_scr, n_i)
    h = h_scr[cur]
    gate = jnp.dot(h, wg_ref[...], preferred_element_type=F32)
    up = jnp.dot(h, wu_ref[...], preferred_element_type=F32)
    o_ref[...] = (_silu(gate) * up).astype(o_ref.dtype)
    _lookahead_slab(x_ref, shn_ref, scn_ref, g_ref, h_scr, cur, n_steps)


def _gate_up_prompt(x, mod, mod_col, gain, wgt, wut, tm):
    G, R, _ = x.shape
    nt, _, tn = wgt.shape
    x_specs, h_scratch, n_i = _lookahead_specs(x, mod, mod_col, tm, nt)
    w_spec = pl.BlockSpec((None, D_MODEL, tn), lambda b, i, j: (j, 0, 0))
    return pl.pallas_call(
        functools.partial(_gate_up_kernel, n_i=n_i, n_steps=nt),
        grid=(G, R // tm, nt),
        in_specs=x_specs + [pl.BlockSpec((1, D_MODEL), lambda b, i, j: (0, 0)), w_spec, w_spec],
        out_specs=pl.BlockSpec((None, None, tm, tn), lambda b, i, j: (j, b, i, 0)),
        out_shape=jax.ShapeDtypeStruct((nt, G, R, tn), BF16),
        scratch_shapes=[h_scratch],
        compiler_params=_cparams(("arbitrary", "arbitrary", "arbitrary")),
        name="gate_up_prompt",
    )(x, mod, mod, mod, mod, gain, wgt, wut)


HGRN_TB = 2048


def _hgrn_prompt_kernel(q_ref, f_ref, i_ref, g_ref, lb_ref, gn_ref, o_ref, s_ref, st_scr, bk_scr):
    C, D = HGRN_CHUNK, HEAD_DIM
    t_idx = pl.program_id(2)
    consts = _hgrn_consts()
    g_norm = gn_ref[...]
    heads = range(HGRN_HP)
    lane = lambda hh: slice(hh * D, (hh + 1) * D)
    lbs = [lb_ref[:, lane(hh)] for hh in heads]

    @pl.when(t_idx == 0)
    def _():
        st_scr[...] = jnp.zeros_like(st_scr)

    def chunk_body(c, carry):
        rs = pl.ds(pl.multiple_of(c * C, C), C)
        outs, new_states = _hgrn_heads(
            [q_ref[rs, lane(hh)] for hh in heads], [f_ref[rs, lane(hh)] for hh in heads],
            [i_ref[rs, lane(hh)] for hh in heads], [g_ref[rs, lane(hh)] for hh in heads],
            lbs, g_norm, [st_scr[hh] for hh in heads], consts, bk_scr)
        for hh in heads:
            st_scr[hh] = new_states[hh]
            o_ref[rs, lane(hh)] = outs[hh].astype(o_ref.dtype)
        return carry

    lax.fori_loop(0, q_ref.shape[0] // C, chunk_body, 0)

    @pl.when(t_idx == pl.num_programs(2) - 1)
    def _():
        for hh in heads:
            s_ref[hh] = st_scr[hh].T


def _hgrn_prompt(proj32, proj16, lb, g_norm):
    _, B, T, W = proj32.shape
    assert W == HGRN_HP * HEAD_DIM
    seq = lambda slab: pl.BlockSpec((None, None, HGRN_TB, W),
                                    lambda b, h, t: (slab + h, b, t, 0))
    return pl.pallas_call(
        _hgrn_prompt_kernel,
        grid=(B, N_LIN_HEADS // HGRN_HP, T // HGRN_TB),
        in_specs=[
            seq(0), seq(0), seq(2), seq(4),
            pl.BlockSpec((1, W), lambda b, h, t: (0, h)),
            pl.BlockSpec((1, HEAD_DIM), lambda b, h, t: (0, 0)),
        ],
        out_specs=(
            pl.BlockSpec((None, HGRN_TB, W), lambda b, h, t: (b, t, h)),
            pl.BlockSpec((None, HGRN_HP, HEAD_DIM, HEAD_DIM), lambda b, h, t: (b, h, 0, 0)),
        ),
        out_shape=(
            jax.ShapeDtypeStruct((B, T, LIN_WIDTH), BF16),
            jax.ShapeDtypeStruct((B, N_LIN_HEADS, HEAD_DIM, HEAD_DIM), F32),
        ),
        scratch_shapes=[pltpu.VMEM((HGRN_HP, HEAD_DIM, HEAD_DIM), F32),
                        pltpu.VMEM((HGRN_HP, 2, HGRN_CHUNK, HEAD_DIM), F32)],
        compiler_params=_cparams(("parallel", "parallel", "arbitrary")),
        name="hgrn_prompt",
    )(proj16, proj32, proj16, proj16, lb, g_norm)


ATT_QB = 4


def _attn_prompt_kernel(sink_ref, q_ref, kvc_ref, kvp_ref, qn_ref, kn_ref,
                        o_ref, nk_ref, nv_ref):
    n = pl.program_id(1)
    W = WINDOW
    qi = lax.broadcasted_iota(jnp.int32, (W, 2 * W), 0)
    kj = lax.broadcasted_iota(jnp.int32, (W, 2 * W), 1)
    dist = W + qi - kj
    in_window = (dist >= 0) & (dist <= W)
    dist_in = jnp.where(in_window, dist.astype(F32), jnp.inf)
    dist_first = jnp.where(in_window & ((n > 0) | (kj >= W)), dist.astype(F32), jnp.inf)
    q_gain = qn_ref[...] * (ATT_SCALE * LOG2E)
    k_norm = kn_ref[...]

    new_k = []
    for h in range(N_KV_HEADS):
        hs = slice(h * HEAD_DIM, (h + 1) * HEAD_DIM)
        vs = slice(KV_WIDTH + h * HEAD_DIM, KV_WIDTH + (h + 1) * HEAD_DIM)
        kc = _rms_rows(kvc_ref[:, hs], k_norm)
        kp = _rms_rows(kvp_ref[:, hs], k_norm)
        new_k.append(kc[(ATT_QB - 1) * W:, :])
        k_all = jnp.concatenate([kp, kc], axis=0).astype(BF16)
        v_all = jnp.concatenate([kvp_ref[:, vs], kvc_ref[:, vs]], axis=0).astype(BF16)

        for blk in range(ATT_QB):
            rows = slice(blk * W, (blk + 1) * W)
            k2 = k_all[blk * W:(blk + 2) * W, :]
            v2 = v_all[blk * W:(blk + 2) * W, :]
            dist_m = dist_first if blk == 0 else dist_in
            for g in range(GROUP):
                a = h * GROUP + g
                cs = slice(a * HEAD_DIM, (a + 1) * HEAD_DIM)
                qh = _rms_rows(q_ref[h, rows, g * HEAD_DIM:(g + 1) * HEAD_DIM].astype(F32),
                               q_gain).astype(BF16)
                s = lax.dot_general(qh, k2, (((1,), (1,)), ((), ())), preferred_element_type=F32)
                s = s + (-ALIBI_SLOPES[a] * LOG2E) * dist_m
                sink = sink_ref[a] * LOG2E
                m = jnp.maximum(jnp.max(s, axis=-1, keepdims=True), sink)
                p = jnp.exp2(s - m)
                den = jnp.sum(p, axis=-1, keepdims=True) + jnp.exp2(sink - m)
                o = jnp.dot(p.astype(BF16), v2, preferred_element_type=F32) * (1.0 / den)
                o_ref[rows, cs] = o.astype(o_ref.dtype)

    @pl.when(n == pl.num_programs(1) - 1)
    def _():
        for h in range(N_KV_HEADS):
            hs = slice(h * HEAD_DIM, (h + 1) * HEAD_DIM)
            nk_ref[:, hs] = new_k[h]
            nv_ref[:, hs] = kvc_ref[(ATT_QB - 1) * W:,
                                    KV_WIDTH + h * HEAD_DIM:KV_WIDTH + (h + 1) * HEAD_DIM]


def _attn_prompt(proj32, proj16, sinks, q_norm, k_norm):
    _, B, T, W = proj32.shape
    assert W == GROUP * HEAD_DIM == 2 * KV_WIDTH
    rows = ATT_QB * WINDOW
    q_slab = 6 // N_KV_HEADS
    kv_slab = 2
    prev = lambda n: jnp.maximum(ATT_QB * n - 1, 0)
    return pl.pallas_call(
        _attn_prompt_kernel,
        grid=(B, T // rows),
        in_specs=[
            pl.BlockSpec(memory_space=pltpu.SMEM),
            pl.BlockSpec((N_KV_HEADS, None, rows, W), lambda b, n: (q_slab, b, n, 0)),
            pl.BlockSpec((None, None, rows, W), lambda b, n: (kv_slab, b, n, 0)),
            pl.BlockSpec((None, None, WINDOW, W), lambda b, n: (kv_slab, b, prev(n), 0)),
            pl.BlockSpec((1, HEAD_DIM), lambda b, n: (0, 0)),
            pl.BlockSpec((1, HEAD_DIM), lambda b, n: (0, 0)),
        ],
        out_specs=(
            pl.BlockSpec((None, rows, ATT_WIDTH), lambda b, n: (b, n, 0)),
            pl.BlockSpec((None, WINDOW, KV_WIDTH), lambda b, n: (b, 0, 0)),
            pl.BlockSpec((None, WINDOW, KV_WIDTH), lambda b, n: (b, 0, 0)),
        ),
        out_shape=(
            jax.ShapeDtypeStruct((B, T, ATT_WIDTH), BF16),
            jax.ShapeDtypeStruct((B, WINDOW, KV_WIDTH), F32),
            jax.ShapeDtypeStruct((B, WINDOW, KV_WIDTH), F32),
        ),
        compiler_params=_cparams(("parallel", "arbitrary")),
        name="attn_prompt",
    )(sinks, proj16, proj32, proj32, q_norm, k_norm)


def _row_to_col(x_row, eye):
    return jnp.sum(jnp.where(eye, x_row, 0.0), axis=1, keepdims=True)


MIX_RB = 4


def _mix_sample_kernel(sink_ref, p_ref, s_ref, ck_ref, cv_ref, lb_ref,
                       gn_ref, qn_ref, kn_ref, o_ref, ns_ref, nk_ref, nv_ref):
    for r in range(p_ref.shape[0]):
        _mix_one_request(sink_ref, p_ref.at[r], s_ref.at[r], ck_ref.at[r], cv_ref.at[r], lb_ref,
                         gn_ref, qn_ref, kn_ref, o_ref.at[r], ns_ref.at[r], nk_ref.at[r],
                         nv_ref.at[r])


def _mix_one_request(sink_ref, p_ref, s_ref, ck_ref, cv_ref, lb_ref,
                     gn_ref, qn_ref, kn_ref, o_ref, ns_ref, nk_ref, nv_ref):
    W = WINDOW
    er = lax.broadcasted_iota(jnp.int32, (HEAD_DIM, HEAD_DIM), 0)
    ec = lax.broadcasted_iota(jnp.int32, (HEAD_DIM, HEAD_DIM), 1)
    eye = er == ec
    g_norm = gn_ref[...]
    q_norm = qn_ref[...]
    k_norm = kn_ref[...]

    def cols(off, width=HEAD_DIM):
        return p_ref[:, off:off + width]

    for h in range(N_LIN_HEADS):
        c0 = h * HEAD_DIM
        q = _silu(cols(c0))
        lb = lb_ref[:, c0:c0 + HEAD_DIM]
        _, fp = _forget_gate(cols(LIN_WIDTH + c0), lb)
        f_col = _row_to_col(lb + fp, eye)
        k_col = 1.0 - f_col
        v = cols(2 * LIN_WIDTH + c0)
        s_new = s_ref[h] * f_col + k_col * v
        ns_ref[h] = s_new
        q8 = jnp.broadcast_to(q, (8, HEAD_DIM)).astype(BF16)
        o = jnp.dot(q8, s_new.astype(BF16), preferred_element_type=F32)[0:1, :]
        o = _rms_rows(o, g_norm) * _silu(cols(3 * LIN_WIDTH + c0))
        o_ref[:, c0:c0 + HEAD_DIM] = o

    row8 = lax.broadcasted_iota(jnp.int32, (8, 1), 0)
    row8_k = lax.broadcasted_iota(jnp.int32, (8, HEAD_DIM), 0)
    lane = lax.broadcasted_iota(jnp.int32, (8, W), 1)
    dist_c = (W - lane).astype(F32)
    rows_w = lax.broadcasted_iota(jnp.int32, (W, HEAD_DIM), 0)
    for h in range(N_KV_HEADS):
        hs = slice(h * HEAD_DIM, (h + 1) * HEAD_DIM)
        k_new = _rms_rows(cols(OFF_K_A + h * HEAD_DIM), k_norm)
        v_new = cols(OFF_V_A + h * HEAD_DIM)
        kc = ck_ref[:, h, :]
        vc = cv_ref[:, h, :]
        nk_ref[:, h, :] = jnp.where(rows_w == W - 1, k_new, pltpu.roll(kc, W - 1, 0))
        nv_ref[:, h, :] = jnp.where(rows_w == W - 1, v_new, pltpu.roll(vc, W - 1, 0))

        q4 = jnp.zeros((8, HEAD_DIM), F32)
        slope = jnp.zeros((8, 1), F32)
        sink = jnp.zeros((8, 1), F32)
        for g in range(GROUP):
            a = h * GROUP + g
            qg = _rms_rows(cols(OFF_Q_A + a * HEAD_DIM), q_norm)
            q4 = jnp.where(row8_k == g, qg, q4)
            slope = jnp.where(row8 == g, ALIBI_SLOPES[a], slope)
            sink = jnp.where(row8 == g, sink_ref[a], sink)
        q4b = q4.astype(BF16)
        s_c = lax.dot_general(q4b, kc.astype(BF16), (((1,), (1,)), ((), ())),
                              preferred_element_type=F32)
        s_c = s_c * ATT_SCALE - slope * dist_c
        s_n = jnp.sum(q4 * k_new, axis=-1, keepdims=True) * ATT_SCALE
        m = jnp.maximum(jnp.maximum(jnp.max(s_c, axis=-1, keepdims=True), s_n), sink)
        p_c = jnp.exp(s_c - m)
        p_n = jnp.exp(s_n - m)
        den = jnp.sum(p_c, axis=-1, keepdims=True) + p_n + jnp.exp(sink - m)
        o = jnp.dot(p_c.astype(BF16), vc.astype(BF16), preferred_element_type=F32)
        o = (o + p_n * v_new) / den
        for g in range(GROUP):
            a = h * GROUP + g
            o_ref[:, LIN_WIDTH + a * HEAD_DIM:LIN_WIDTH + (a + 1) * HEAD_DIM] = o[g:g + 1, :]


def _mix_sample(proj, state, cache_k, cache_v, layer, sinks, lb, g_norm, q_norm, k_norm):
    B = proj.shape[0]
    rb = MIX_RB
    vec = pl.BlockSpec((1, HEAD_DIM), lambda b: (0, 0))
    lbs = pl.BlockSpec((1, LIN_WIDTH), lambda b: (0, 0))
    cache_in = pl.BlockSpec((None, rb, WINDOW, N_KV_HEADS, HEAD_DIM),
                            lambda b: (layer, b, 0, 0, 0))
    st_in = pl.BlockSpec((None, rb, N_LIN_HEADS, HEAD_DIM, HEAD_DIM),
                         lambda b: (layer, b, 0, 0, 0))
    cache = pl.BlockSpec((rb, WINDOW, N_KV_HEADS, HEAD_DIM), lambda b: (b, 0, 0, 0))
    st = pl.BlockSpec((rb, N_LIN_HEADS, HEAD_DIM, HEAD_DIM), lambda b: (b, 0, 0, 0))
    return pl.pallas_call(
        _mix_sample_kernel,
        grid=(B // rb,),
        in_specs=[
            pl.BlockSpec(memory_space=pltpu.SMEM),
            pl.BlockSpec((rb, 1, IN_WIDTH), lambda b: (b, 0, 0)),
            st_in, cache_in, cache_in, lbs, vec, vec, vec,
        ],
        out_specs=(
            pl.BlockSpec((rb, 1, MIX_WIDTH), lambda b: (b, 0, 0)),
            st, cache, cache,
        ),
        out_shape=(
            jax.ShapeDtypeStruct((B, 1, MIX_WIDTH), F32),
            jax.ShapeDtypeStruct(state.shape[1:], F32),
            jax.ShapeDtypeStruct((B, WINDOW, N_KV_HEADS, HEAD_DIM), F32),
            jax.ShapeDtypeStruct((B, WINDOW, N_KV_HEADS, HEAD_DIM), F32),
        ),
        compiler_params=_cparams(("parallel",)),
        name="mix_sample",
    )(sinks, proj, state, cache_k, cache_v, lb, g_norm, q_norm, k_norm)


TILE_N = 512
P_TM = 1024
P_TM_OUT = 2048
F32_SLABS = (LIN_WIDTH // TILE_N, LIN_WIDTH // TILE_N + 1, OFF_K_A // TILE_N)


def kernel(x_prompt, x_sample, cache_k, cache_v, state_hgrn, c_prompt, c_sample, lower_bounds,
           w_ada, b_ada, norm_mix, w_in, q_norm, k_norm, attn_sinks, g_norm, w_out, norm_ffn,
           w_gate, w_up, w_down):
    B, T, _ = x_prompt.shape
    BS = x_sample.shape[0]

    lb_all = _lower_bounds(lower_bounds)

    c_all = jnp.concatenate(
        [c_prompt, c_sample, jnp.zeros((MOD_ROWS - B - BS, D_MODEL), F32)], axis=0)
    mod = _modulation(c_all, w_ada, b_ada)

    yp = x_prompt
    ys = x_sample.reshape(1, BS, D_MODEL)
    pk, pv, ps, sk, sv, ss = [], [], [], [], [], []
    for l in range(DEPTH):
        mod_p = mod[l, :B].reshape(B, 1, 6 * D_MODEL)
        mod_s = mod[l, B:B + BS].reshape(1, BS, 6 * D_MODEL)
        gain_mix = norm_mix[l].reshape(1, D_MODEL)
        gain_ffn = norm_ffn[l].reshape(1, D_MODEL)
        lb = lb_all[l].reshape(1, LIN_WIDTH)
        gn = g_norm[l].reshape(1, HEAD_DIM)
        qn = q_norm[l].reshape(1, HEAD_DIM)
        kn = k_norm[l].reshape(1, HEAD_DIM)
        sinks = attn_sinks[l].astype(F32)

        proj_s, w_in_b = _norm_matmul(ys, mod_s, 0, gain_mix, w_in, l, BS, TILE_N)
        mixed_s, s_s, k_s, v_s = _mix_sample(
            proj_s.reshape(BS, 1, IN_WIDTH), state_hgrn, cache_k, cache_v, l,
            sinks, lb, gn, qn, kn)
        ys, w_out_b = _matmul_residual(mixed_s.reshape(1, BS, MIX_WIDTH), w_out, l, ys, mod_s,
                                       2 * D_MODEL, BS, TILE_N)
        ff_s, w_gate_b, w_up_b = _norm_gate_up(ys, mod_s, 3, gain_ffn, w_gate, w_up, l, BS, TILE_N)
        ys, w_down_b = _matmul_residual(ff_s, w_down, l, ys, mod_s, 5 * D_MODEL, BS, TILE_N)
        sk.append(k_s)
        sv.append(v_s)
        ss.append(s_s)

        proj32, proj16 = _in_proj_prompt(yp, mod_p, gain_mix, w_in_b, F32_SLABS, P_TM)
        o_l, s_p = _hgrn_prompt(proj32, proj16, lb, gn)
        o_a, k_p, v_p = _attn_prompt(proj32, proj16, sinks, qn, kn)
        yp = _res_matmul_prompt([o_l, o_a], [(w_out_b, 0), (w_out_b, 1)], yp, mod_p, 2,
                                P_TM_OUT, "out_proj")
        ff = _gate_up_prompt(yp, mod_p, 3, gain_ffn, w_gate_b, w_up_b, P_TM)
        yp = _res_matmul_prompt([ff], [w_down_b], yp, mod_p, 5, P_TM, "down_proj")
        pk.append(k_p.reshape(B, WINDOW, N_KV_HEADS, HEAD_DIM))
        pv.append(v_p.reshape(B, WINDOW, N_KV_HEADS, HEAD_DIM))
        ps.append(s_p)

    return (yp, ys.reshape(BS, 1, D_MODEL), jnp.stack(pk), jnp.stack(pv), jnp.stack(ps),
            jnp.stack(sk), jnp.stack(sv), jnp.stack(ss))
```

```python
import functools
import math

import jax
import jax.numpy as jnp
from jax import lax
from jax.experimental import pallas as pl
from jax.experimental.pallas import tpu as pltpu

F32 = jnp.float32
BF16 = jnp.bfloat16

D_MODEL = 2048
DEPTH = 4
HEAD_DIM = 128
N_LIN_HEADS = 8
LIN_WIDTH = N_LIN_HEADS * HEAD_DIM
N_ATT_HEADS = 8
N_KV_HEADS = 2
GROUP = N_ATT_HEADS // N_KV_HEADS
ATT_WIDTH = N_ATT_HEADS * HEAD_DIM
KV_WIDTH = N_KV_HEADS * HEAD_DIM
MIX_WIDTH = LIN_WIDTH + ATT_WIDTH
WINDOW = 128
D_FF = 5632
IN_WIDTH = 4 * LIN_WIDTH + ATT_WIDTH + 2 * KV_WIDTH
EPS = 1e-6
ATT_SCALE = HEAD_DIM ** -0.5
ALIBI_SLOPES = tuple(2.0 ** (-8.0 * (a + 1) / N_ATT_HEADS) for a in range(N_ATT_HEADS))

OFF_Q_A = 4 * LIN_WIDTH
OFF_K_A = OFF_Q_A + ATT_WIDTH
OFF_V_A = OFF_K_A + KV_WIDTH

MOD_ROWS = 40
HGRN_CHUNK = 128
HGRN_SUB = 8
LOG2E = math.log2(math.e)

VMEM_LIMIT = 56 * 1024 * 1024


def _cparams(sem):
    return pltpu.CompilerParams(dimension_semantics=sem, vmem_limit_bytes=VMEM_LIMIT)


def _silu(x):
    return x / (1.0 + jnp.exp(-x))


def _rms_rows(x, g):
    ms = jnp.mean(x * x, axis=-1, keepdims=True)
    return x * lax.rsqrt(ms + EPS) * g


def _forget_gate(f_raw, lb):
    t = jnp.exp(-jnp.abs(f_raw))
    inv = 1.0 / (1.0 + t)
    sig = jnp.where(f_raw >= 0.0, inv, t * inv)
    return t, (1.0 - lb) * sig


def _log_forget(f_raw, lb):
    t, fp = _forget_gate(f_raw, lb)
    log_sig = jnp.minimum(f_raw, 0.0) - jnp.log(1.0 + t)
    return jnp.where(lb > 0.0, jnp.log(lb + fp), log_sig), fp


def _lb_kernel(lb_ref, o_ref):
    x = lb_ref[...]
    m = jnp.max(x, axis=0, keepdims=True)
    e = jnp.exp(x - m)
    p = e / jnp.sum(e, axis=0, keepdims=True)
    acc = jnp.zeros_like(p[0:1])
    for l in range(DEPTH):
        if l > 0:
            acc = acc + p[l:l + 1]
        o_ref[l:l + 1, :] = acc


def _lower_bounds(lower_bounds):
    return pl.pallas_call(
        _lb_kernel,
        out_shape=jax.ShapeDtypeStruct((DEPTH, LIN_WIDTH), F32),
        name="lower_bounds",
    )(lower_bounds.astype(F32))


ADA_TN = 1024


def _ada_kernel(c_ref, w_ref, b_ref, o_ref):
    s = _silu(c_ref[...]).astype(BF16)
    acc = jnp.dot(s, w_ref[...].astype(BF16), preferred_element_type=F32)
    o_ref[...] = acc + b_ref[...]


def _modulation(c_all, w_ada, b_ada):
    n = 6 * D_MODEL
    return pl.pallas_call(
        _ada_kernel,
        grid=(DEPTH, n // ADA_TN),
        in_specs=[
            pl.BlockSpec((MOD_ROWS, D_MODEL), lambda l, j: (0, 0)),
            pl.BlockSpec((None, D_MODEL, ADA_TN), lambda l, j: (l, 0, j)),
            pl.BlockSpec((None, 1, ADA_TN), lambda l, j: (l, 0, j)),
        ],
        out_specs=pl.BlockSpec((None, MOD_ROWS, ADA_TN), lambda l, j: (l, 0, j)),
        out_shape=jax.ShapeDtypeStruct((DEPTH, MOD_ROWS, n), F32),
        compiler_params=_cparams(("parallel", "parallel")),
        name="modulation",
    )(c_all, w_ada, b_ada.reshape(DEPTH, 1, n))


NORM_ROWS = 16
NORM_UNROLL = 4


def _modulated_norm_into(h_scr, x_ref, sh_ref, sc_ref, g_ref):
    rows = x_ref.shape[0]
    ch = min(rows, NORM_ROWS)
    per_row = sc_ref.shape[0] != 1
    gain = g_ref[...]
    if not per_row:
        gain = gain * (1.0 + sc_ref[...])
        shift = sh_ref[...]

    def body(c, carry):
        rs = pl.ds(pl.multiple_of(c * ch, ch), ch)
        x = x_ref[rs, :]
        inv = lax.rsqrt(jnp.mean(x * x, axis=-1, keepdims=True) + EPS)
        if per_row:
            h = (x * inv * gain) * (1.0 + sc_ref[rs, :]) + sh_ref[rs, :]
        else:
            h = (x * inv) * gain + shift
        h_scr[rs, :] = h.astype(BF16)
        return carry

    lax.fori_loop(0, rows // ch, body, 0, unroll=min(NORM_UNROLL, rows // ch))


def _w_spec(w, layer, tn):
    return pl.BlockSpec((None, w.shape[1], tn), lambda b, i, j: (layer, 0, j))


def _wb_out(w, tn):
    rows, n = w.shape[-2:]
    return (pl.BlockSpec((None, rows, tn), lambda b, i, j: (j, 0, 0)),
            jax.ShapeDtypeStruct((n // tn, rows, tn), BF16))


def _norm_mm_kernel(x_ref, sh_ref, sc_ref, g_ref, w_ref, o_ref, wb_ref, h_scr):
    @pl.when(pl.program_id(2) == 0)
    def _():
        _modulated_norm_into(h_scr, x_ref, sh_ref, sc_ref, g_ref)

    w = w_ref[...].astype(BF16)
    wb_ref[...] = w
    o_ref[...] = jnp.dot(h_scr[...], w, preferred_element_type=F32)


def _mod_specs(mod, mod_col, tm):
    if mod.shape[1] == 1:
        return [pl.BlockSpec((None, 1, D_MODEL), lambda b, i, j, c=mod_col + k: (b, 0, c))
                for k in range(2)]
    return [pl.BlockSpec((None, tm, D_MODEL), lambda b, i, j, c=mod_col + k: (b, i, c))
            for k in range(2)]


def _lookup(j, table):
    out = table[-1]
    for k in range(len(table) - 2, -1, -1):
        out = jnp.where(j == k, table[k], out)
    return out


def _norm_matmul(x, mod, mod_col, gain, w, layer, tm, tn):
    G, R, _ = x.shape
    nt = w.shape[-1] // tn
    assert G * (R // tm) == 1
    wb_spec, wb_shape = _wb_out(w, tn)
    out_specs = [pl.BlockSpec((None, tm, tn), lambda b, i, j: (b, i, j)), wb_spec]
    out_shape = [jax.ShapeDtypeStruct((G, R, nt * tn), F32), wb_shape]
    return pl.pallas_call(
        _norm_mm_kernel,
        grid=(G, R // tm, nt),
        in_specs=[pl.BlockSpec((None, tm, D_MODEL), lambda b, i, j: (b, i, 0))]
        + _mod_specs(mod, mod_col, tm)
        + [pl.BlockSpec((1, D_MODEL), lambda b, i, j: (0, 0)), _w_spec(w, layer, tn)],
        out_specs=out_specs,
        out_shape=out_shape,
        scratch_shapes=[pltpu.VMEM((tm, D_MODEL), BF16)],
        compiler_params=_cparams(("parallel", "parallel", "arbitrary")),
        name="norm_in_proj",
    )(x, mod, mod, gain, w)


def _norm_gu_kernel(x_ref, sh_ref, sc_ref, g_ref, wg_ref, wu_ref, o_ref, wgb_ref, wub_ref, h_scr):
    @pl.when(pl.program_id(2) == 0)
    def _():
        _modulated_norm_into(h_scr, x_ref, sh_ref, sc_ref, g_ref)

    h = h_scr[...]
    wg = wg_ref[...].astype(BF16)
    wu = wu_ref[...].astype(BF16)
    wgb_ref[...] = wg
    wub_ref[...] = wu
    gate = jnp.dot(h, wg, preferred_element_type=F32)
    up = jnp.dot(h, wu, preferred_element_type=F32)
    o_ref[...] = (_silu(gate) * up).astype(o_ref.dtype)


def _norm_gate_up(x, mod, mod_col, gain, wg, wu, layer, tm, tn):
    G, R, _ = x.shape
    nt = wg.shape[-1] // tn
    assert G * (R // tm) == 1
    wb_spec, wb_shape = _wb_out(wg, tn)
    return pl.pallas_call(
        _norm_gu_kernel,
        grid=(G, R // tm, nt),
        in_specs=[pl.BlockSpec((None, tm, D_MODEL), lambda b, i, j: (b, i, 0))]
        + _mod_specs(mod, mod_col, tm)
        + [pl.BlockSpec((1, D_MODEL), lambda b, i, j: (0, 0)),
           _w_spec(wg, layer, tn), _w_spec(wu, layer, tn)],
        out_specs=[pl.BlockSpec((None, tm, tn), lambda b, i, j: (b, i, j)), wb_spec, wb_spec],
        out_shape=[jax.ShapeDtypeStruct((G, R, nt * tn), BF16), wb_shape, wb_shape],
        scratch_shapes=[pltpu.VMEM((tm, D_MODEL), BF16)],
        compiler_params=_cparams(("parallel", "parallel", "arbitrary")),
        name="norm_gate_up",
    )(x, mod, mod, gain, wg, wu)


def _res_matmul_kernel(*refs, n_a):
    a_refs, w_refs = refs[:n_a], refs[n_a:2 * n_a]
    x_ref, gt_ref, o_ref = refs[2 * n_a:]
    acc = None
    for a_ref, w_ref in zip(a_refs, w_refs):
        slabs = [a_ref[s] for s in range(a_ref.shape[0])] if len(a_ref.shape) == 3 else [a_ref[...]]
        ts = slabs[0].shape[1]
        for s, a in enumerate(slabs):
            part = jnp.dot(a, w_ref[s * ts:(s + 1) * ts, :], preferred_element_type=F32)
            acc = part if acc is None else acc + part
    o_ref[...] = x_ref[...] + gt_ref[...] * acc


def _res_matmul_prompt(a_list, w_list, x, mod, gate_col, tm, name):
    G, R, _ = x.shape
    a_specs, w_specs, ws = [], [], []
    for a, w in zip(a_list, w_list):
        w, row_blk = w if isinstance(w, tuple) else (w, 0)
        if a.ndim == 4:
            S, _, _, ts = a.shape
            a_specs.append(pl.BlockSpec((S, None, tm, ts), lambda b, i, j: (0, b, i, 0)))
            k = S * ts
        else:
            k = a.shape[2]
            a_specs.append(pl.BlockSpec((None, tm, k), lambda b, i, j: (b, i, 0)))
        nt, _, tn = w.shape
        w_specs.append(pl.BlockSpec((None, k, tn), lambda b, i, j, r=row_blk: (j, r, 0)))
        ws.append(w)
    tile = pl.BlockSpec((None, tm, tn), lambda b, i, j: (b, i, j))
    return pl.pallas_call(
        functools.partial(_res_matmul_kernel, n_a=len(a_list)),
        grid=(G, R // tm, nt),
        in_specs=a_specs + w_specs
        + [tile, pl.BlockSpec((None, 1, tn), lambda b, i, j: (b, 0, gate_col * nt + j))],
        out_specs=tile,
        out_shape=jax.ShapeDtypeStruct((G, R, D_MODEL), F32),
        compiler_params=_cparams(("parallel", "parallel", "arbitrary")),
        name=name,
    )(*a_list, *ws, x, mod)


def _mm_res_kernel(a_ref, w_ref, x_ref, gt_ref, o_ref, wb_ref):
    w = w_ref[...].astype(BF16)
    wb_ref[...] = w
    acc = jnp.dot(a_ref[...].astype(BF16), w, preferred_element_type=F32)
    o_ref[...] = x_ref[...] + gt_ref[...] * acc


def _matmul_residual(a, w, layer, x, mod, mod_off, tm, tn):
    G, R, K = a.shape
    assert G * (R // tm) == 1 and mod.shape[1] == R
    gcol = mod_off // tn
    wb_spec, wb_shape = _wb_out(w, tn)
    return pl.pallas_call(
        _mm_res_kernel,
        grid=(G, R // tm, D_MODEL // tn),
        in_specs=[
            pl.BlockSpec((None, tm, K), lambda b, i, j: (b, i, 0)),
            _w_spec(w, layer, tn),
            pl.BlockSpec((None, tm, tn), lambda b, i, j: (b, i, j)),
            pl.BlockSpec((None, tm, tn), lambda b, i, j: (b, i, gcol + j)),
        ],
        out_specs=[pl.BlockSpec((None, tm, tn), lambda b, i, j: (b, i, j)), wb_spec],
        out_shape=[jax.ShapeDtypeStruct((G, R, D_MODEL), F32), wb_shape],
        compiler_params=_cparams(("parallel", "parallel", "arbitrary")),
        name="matmul_residual",
    )(a, w, x, mod)


def _split3_bf16(x):
    hi = x.astype(BF16)
    r1 = x - hi.astype(F32)
    mid = r1.astype(BF16)
    lo = (r1 - mid.astype(F32)).astype(BF16)
    return hi, mid, lo


HGRN_HP = 4
HGRN_SEP_MIN = 32


def _hgrn_consts():
    C, P, D = HGRN_CHUNK, HGRN_SUB, HEAD_DIM
    row = lax.broadcasted_iota(jnp.int32, (C, C), 0)
    col = lax.broadcasted_iota(jnp.int32, (C, C), 1)
    tri = (row >= col).astype(BF16)
    sel_r = lax.broadcasted_iota(jnp.int32, (P * D, C), 0)
    sel_c = lax.broadcasted_iota(jnp.int32, (P * D, C), 1)
    psh = P.bit_length() - 1
    sel = ((sel_c & (P - 1)) == (sel_r >> (D.bit_length() - 1))).astype(BF16)
    keep_diag = ((col >> psh) == (row >> psh)) & ((col & (P - 1)) <= (row & (P - 1)))
    levels = []
    L = C // 2
    while L >= P:
        levels.append(L)
        L //= 2
    keep_level = {}
    for L in levels:
        if L < HGRN_SEP_MIN:
            sh = L.bit_length() - 1
            keep_level[L] = (((row >> sh) & 1) == 1) & ((col >> sh) == (row >> sh) - 1)
    return tri, sel, keep_diag, levels, keep_level


def _hgrn_heads(q_raw, f_raw, i_raw, g_raw, lbs, g_norm, states, consts, bk_scr):
    C, P, D = HGRN_CHUNK, HGRN_SUB, HEAD_DIM
    tri, sel, keep_diag, levels, keep_level = consts
    n = len(q_raw)
    qs, ks, vs, parts = [], [], [], []
    for hh in range(n):
        logf, fp = _log_forget(f_raw[hh], lbs[hh])
        qs.append(_silu(q_raw[hh].astype(F32)))
        ks.append((1.0 - lbs[hh]) - fp)
        vs.append(i_raw[hh].astype(BF16))
        parts.extend(_split3_bf16(logf * LOG2E))
    cum = jnp.dot(tri, jnp.concatenate(parts, axis=1), preferred_element_type=F32)

    outs, new_states = [], []
    for hh in range(n):
        q, k, v16, st = qs[hh], ks[hh], vs[hh], states[hh]
        b = (cum[:, (3 * hh) * D:(3 * hh + 1) * D] + cum[:, (3 * hh + 1) * D:(3 * hh + 2) * D]
             + cum[:, (3 * hh + 2) * D:(3 * hh + 3) * D])
        b_end = b[C - 1:C, :]
        o = lax.dot_general((q * jnp.exp2(b)).astype(BF16), st.astype(BF16),
                            (((1,), (1,)), ((), ())), preferred_element_type=F32)

        q_sep, k_sep, a_masked = [], [], []
        for L in levels:
            q_rows, k_rows = [], []
            for p in range(C // (2 * L)):
                lo, mid, hi = 2 * L * p, 2 * L * p + L, 2 * L * (p + 1)
                ref = b[mid - 1:mid, :]
                zf = jnp.zeros((L, D), F32)
                kk = jnp.concatenate([k[lo:mid] * jnp.exp2(ref - b[lo:mid]), zf], axis=0)
                qq = jnp.concatenate([zf, q[mid:hi] * jnp.exp2(b[mid:hi] - ref)], axis=0)
                q_rows.append(qq.astype(BF16))
                k_rows.append(kk.astype(BF16))
            if L >= HGRN_SEP_MIN:
                for p in range(len(q_rows)):
                    above = jnp.zeros((2 * L * p, D), BF16)
                    below = jnp.zeros((C - 2 * L * (p + 1), D), BF16)
                    for rows_p, dst in ((q_rows[p], q_sep), (k_rows[p], k_sep)):
                        dst.append(jnp.concatenate(
                            [y for y in (above, rows_p, below) if y.shape[0] > 0], axis=0))
            else:
                a_l = lax.dot_general(jnp.concatenate(q_rows, axis=0),
                                      jnp.concatenate(k_rows, axis=0),
                                      (((1,), (1,)), ((), ())), preferred_element_type=F32)
                a_masked.append((keep_level[L], a_l))
        a = lax.dot_general(jnp.concatenate(q_sep, axis=1), jnp.concatenate(k_sep, axis=1),
                            (((1,), (1,)), ((), ())), preferred_element_type=F32)
        for keep, a_l in a_masked:
            a = jnp.where(keep, a_l, a)

        bk_scr[hh, 0] = b
        bk_scr[hh, 1] = k
        zs = []
        for s in range(P):
            bs, ksb = [jnp.concatenate(
                [bk_scr[hh, w, pl.ds(m * P + s, P, stride=0), :] for m in range(C // P)], axis=0)
                for w in range(2)]
            e = jnp.exp2(jnp.minimum(b - bs, 0.0))
            zs.append((q * ksb * e).astype(BF16))
        a_d = jnp.dot(jnp.concatenate(zs, axis=1), sel, preferred_element_type=F32)
        a = jnp.where(keep_diag, a_d, a)

        o = o + jnp.dot(a.astype(BF16), v16, preferred_element_type=F32)

        k_end = (k * jnp.exp2(b_end - b)).astype(BF16)
        upd = lax.dot_general(v16, k_end, (((0,), (0,)), ((), ())), preferred_element_type=F32)
        new_states.append(st * jnp.exp2(b_end) + upd)
        outs.append(_rms_rows(o, g_norm) * _silu(g_raw[hh].astype(F32)))
    return outs, new_states


def _lookahead_slot(x_ref, sh0_ref, sc0_ref, g_ref, h_scr, n_i):
    tile = pl.program_id(0) * n_i + pl.program_id(1)

    @pl.when((tile == 0) & (pl.program_id(2) == 0))
    def _():
        _modulated_norm_into(h_scr.at[0], x_ref, sh0_ref, sc0_ref, g_ref)

    return tile % 2


def _lookahead_slab(x_ref, shn_ref, scn_ref, g_ref, h_scr, cur, n_steps):
    j = pl.program_id(2)
    tm = x_ref.shape[0]
    per_step = -(-tm // ((n_steps - 1) * NORM_ROWS)) * NORM_ROWS
    gain = g_ref[...] * (1.0 + scn_ref[...])
    shift = shn_ref[...]
    r0 = jnp.clip((j - 1) * per_step, 0, tm - per_step)
    nxt = h_scr.at[1 - cur]
    for k in range(per_step // NORM_ROWS):
        rs = pl.ds(pl.multiple_of(r0 + k * NORM_ROWS, NORM_ROWS), NORM_ROWS)
        x = x_ref[rs, :]
        inv = lax.rsqrt(jnp.mean(x * x, axis=-1, keepdims=True) + EPS)
        nxt[rs, :] = ((x * inv) * gain + shift).astype(BF16)


def _lookahead_specs(x, mod, mod_col, tm, n_j):
    G, R, _ = x.shape
    n_i = R // tm
    last = G * n_i - 1

    def x_tile(b, i, j):
        t = jnp.minimum(b * n_i + i + (j > 0).astype(jnp.int32), last)
        return t // n_i, t % n_i, 0

    nxt_b = lambda b, i: jnp.minimum(b * n_i + i + 1, last) // n_i
    specs = [pl.BlockSpec((None, tm, D_MODEL), x_tile)]
    specs += [pl.BlockSpec((None, 1, D_MODEL), lambda b, i, j, c=mod_col + k: (0, 0, c))
              for k in range(2)]
    specs += [pl.BlockSpec((None, 1, D_MODEL), lambda b, i, j, c=mod_col + k: (nxt_b(b, i), 0, c))
              for k in range(2)]
    return specs, pltpu.VMEM((2, tm, D_MODEL), BF16), n_i


def _in_proj_kernel(x_ref, sh0_ref, sc0_ref, shn_ref, scn_ref, g_ref, w_ref, o32_ref, o16_ref,
                    h_scr, *, n_i, n_steps):
    cur = _lookahead_slot(x_ref, sh0_ref, sc0_ref, g_ref, h_scr, n_i)
    acc = jnp.dot(h_scr[cur], w_ref[...], preferred_element_type=F32)
    o32_ref[...] = acc
    o16_ref[...] = acc.astype(BF16)
    _lookahead_slab(x_ref, shn_ref, scn_ref, g_ref, h_scr, cur, n_steps)


def _in_proj_prompt(x, mod, gain, wt, f32_slabs, tm):
    G, R, _ = x.shape
    nt, _, tn = wt.shape
    n32 = len(f32_slabs)
    n16 = nt - n32
    order = tuple(f32_slabs) + tuple(s for s in range(nt) if s not in f32_slabs)
    x_specs, h_scratch, n_i = _lookahead_specs(x, mod, 0, tm, nt)
    return pl.pallas_call(
        functools.partial(_in_proj_kernel, n_i=n_i, n_steps=nt),
        grid=(G, R // tm, nt),
        in_specs=x_specs
        + [pl.BlockSpec((1, D_MODEL), lambda b, i, j: (0, 0)),
           pl.BlockSpec((None, D_MODEL, tn), lambda b, i, j: (_lookup(j, order), 0, 0))],
        out_specs=[
            pl.BlockSpec((None, None, tm, tn), lambda b, i, j: (jnp.minimum(j, n32), b, i, 0)),
            pl.BlockSpec((None, None, tm, tn),
                         lambda b, i, j: (jnp.where(j < n32, n16, j - n32), b, i, 0)),
        ],
        out_shape=[jax.ShapeDtypeStruct((n32 + 1, G, R, tn), F32),
                   jax.ShapeDtypeStruct((n16 + 1, G, R, tn), BF16)],
        scratch_shapes=[h_scratch],
        compiler_params=_cparams(("arbitrary", "arbitrary", "arbitrary")),
        name="in_proj_prompt",
    )(x, mod, mod, mod, mod, gain, wt)


def _gate_up_kernel(x_ref, sh0_ref, sc0_ref, shn_ref, scn_ref, g_ref, wg_ref, wu_ref, o_ref,
                    h_scr, *, n_i, n_steps):
    cur = _lookahead_slot(x_ref, sh0_ref, sc0_ref, g_ref, h_scr, n_i)
    h = h_scr[cur]
    gate = jnp.dot(h, wg_ref[...], preferred_element_type=F32)
    up = jnp.dot(h, wu_ref[...], preferred_element_type=F32)
    o_ref[...] = (_silu(gate) * up).astype(o_ref.dtype)
    _lookahead_slab(x_ref, shn_ref, scn_ref, g_ref, h_scr, cur, n_steps)


def _gate_up_prompt(x, mod, mod_col, gain, wgt, wut, tm):
    G, R, _ = x.shape
    nt, _, tn = wgt.shape
    x_specs, h_scratch, n_i = _lookahead_specs(x, mod, mod_col, tm, nt)
    w_spec = pl.BlockSpec((None, D_MODEL, tn), lambda b, i, j: (j, 0, 0))
    return pl.pallas_call(
        functools.partial(_gate_up_kernel, n_i=n_i, n_steps=nt),
        grid=(G, R // tm, nt),
        in_specs=x_specs + [pl.BlockSpec((1, D_MODEL), lambda b, i, j: (0, 0)), w_spec, w_spec],
        out_specs=pl.BlockSpec((None, None, tm, tn), lambda b, i, j: (j, b, i, 0)),
        out_shape=jax.ShapeDtypeStruct((nt, G, R, tn), BF16),
        scratch_shapes=[h_scratch],
        compiler_params=_cparams(("arbitrary", "arbitrary", "arbitrary")),
        name="gate_up_prompt",
    )(x, mod, mod, mod, mod, gain, wgt, wut)


HGRN_TB = 2048


def _hgrn_prompt_kernel(q_ref, f_ref, i_ref, g_ref, lb_ref, gn_ref, o_ref, s_ref, st_scr, bk_scr):
    C, D = HGRN_CHUNK, HEAD_DIM
    t_idx = pl.program_id(2)
    consts = _hgrn_consts()
    g_norm = gn_ref[...]
    heads = range(HGRN_HP)
    lane = lambda hh: slice(hh * D, (hh + 1) * D)
    lbs = [lb_ref[:, lane(hh)] for hh in heads]

    @pl.when(t_idx == 0)
    def _():
        st_scr[...] = jnp.zeros_like(st_scr)

    def chunk_body(c, carry):
        rs = pl.ds(pl.multiple_of(c * C, C), C)
        outs, new_states = _hgrn_heads(
            [q_ref[rs, lane(hh)] for hh in heads], [f_ref[rs, lane(hh)] for hh in heads],
            [i_ref[rs, lane(hh)] for hh in heads], [g_ref[rs, lane(hh)] for hh in heads],
            lbs, g_norm, [st_scr[hh] for hh in heads], consts, bk_scr)
        for hh in heads:
            st_scr[hh] = new_states[hh]
            o_ref[rs, lane(hh)] = outs[hh].astype(o_ref.dtype)
        return carry

    lax.fori_loop(0, q_ref.shape[0] // C, chunk_body, 0)

    @pl.when(t_idx == pl.num_programs(2) - 1)
    def _():
        for hh in heads:
            s_ref[hh] = st_scr[hh].T


def _hgrn_prompt(proj32, proj16, lb, g_norm):
    _, B, T, W = proj32.shape
    assert W == HGRN_HP * HEAD_DIM
    seq = lambda slab: pl.BlockSpec((None, None, HGRN_TB, W),
                                    lambda b, h, t: (slab + h, b, t, 0))
    return pl.pallas_call(
        _hgrn_prompt_kernel,
        grid=(B, N_LIN_HEADS // HGRN_HP, T // HGRN_TB),
        in_specs=[
            seq(0), seq(0), seq(2), seq(4),
            pl.BlockSpec((1, W), lambda b, h, t: (0, h)),
            pl.BlockSpec((1, HEAD_DIM), lambda b, h, t: (0, 0)),
        ],
        out_specs=(
            pl.BlockSpec((None, HGRN_TB, W), lambda b, h, t: (b, t, h)),
            pl.BlockSpec((None, HGRN_HP, HEAD_DIM, HEAD_DIM), lambda b, h, t: (b, h, 0, 0)),
        ),
        out_shape=(
            jax.ShapeDtypeStruct((B, T, LIN_WIDTH), BF16),
            jax.ShapeDtypeStruct((B, N_LIN_HEADS, HEAD_DIM, HEAD_DIM), F32),
        ),
        scratch_shapes=[pltpu.VMEM((HGRN_HP, HEAD_DIM, HEAD_DIM), F32),
                        pltpu.VMEM((HGRN_HP, 2, HGRN_CHUNK, HEAD_DIM), F32)],
        compiler_params=_cparams(("parallel", "parallel", "arbitrary")),
        name="hgrn_prompt",
    )(proj16, proj32, proj16, proj16, lb, g_norm)


ATT_QB = 4


def _attn_prompt_kernel(sink_ref, q_ref, kvc_ref, kvp_ref, qn_ref, kn_ref,
                        o_ref, nk_ref, nv_ref):
    n = pl.program_id(1)
    W = WINDOW
    qi = lax.broadcasted_iota(jnp.int32, (W, 2 * W), 0)
    kj = lax.broadcasted_iota(jnp.int32, (W, 2 * W), 1)
    dist = W + qi - kj
    in_window = (dist >= 0) & (dist <= W)
    dist_in = jnp.where(in_window, dist.astype(F32), jnp.inf)
    dist_first = jnp.where(in_window & ((n > 0) | (kj >= W)), dist.astype(F32), jnp.inf)
    q_gain = qn_ref[...] * (ATT_SCALE * LOG2E)
    k_norm = kn_ref[...]

    new_k = []
    for h in range(N_KV_HEADS):
        hs = slice(h * HEAD_DIM, (h + 1) * HEAD_DIM)
        vs = slice(KV_WIDTH + h * HEAD_DIM, KV_WIDTH + (h + 1) * HEAD_DIM)
        kc = _rms_rows(kvc_ref[:, hs], k_norm)
        kp = _rms_rows(kvp_ref[:, hs], k_norm)
        new_k.append(kc[(ATT_QB - 1) * W:, :])
        k_all = jnp.concatenate([kp, kc], axis=0).astype(BF16)
        v_all = jnp.concatenate([kvp_ref[:, vs], kvc_ref[:, vs]], axis=0).astype(BF16)

        for blk in range(ATT_QB):
            rows = slice(blk * W, (blk + 1) * W)
            k2 = k_all[blk * W:(blk + 2) * W, :]
            v2 = v_all[blk * W:(blk + 2) * W, :]
            dist_m = dist_first if blk == 0 else dist_in
            for g in range(GROUP):
                a = h * GROUP + g
                cs = slice(a * HEAD_DIM, (a + 1) * HEAD_DIM)
                qh = _rms_rows(q_ref[h, rows, g * HEAD_DIM:(g + 1) * HEAD_DIM].astype(F32),
                               q_gain).astype(BF16)
                s = lax.dot_general(qh, k2, (((1,), (1,)), ((), ())), preferred_element_type=F32)
                s = s + (-ALIBI_SLOPES[a] * LOG2E) * dist_m
                sink = sink_ref[a] * LOG2E
                m = jnp.maximum(jnp.max(s, axis=-1, keepdims=True), sink)
                p = jnp.exp2(s - m)
                den = jnp.sum(p, axis=-1, keepdims=True) + jnp.exp2(sink - m)
                o = jnp.dot(p.astype(BF16), v2, preferred_element_type=F32) * (1.0 / den)
                o_ref[rows, cs] = o.astype(o_ref.dtype)

    @pl.when(n == pl.num_programs(1) - 1)
    def _():
        for h in range(N_KV_HEADS):
            hs = slice(h * HEAD_DIM, (h + 1) * HEAD_DIM)
            nk_ref[:, hs] = new_k[h]
            nv_ref[:, hs] = kvc_ref[(ATT_QB - 1) * W:,
                                    KV_WIDTH + h * HEAD_DIM:KV_WIDTH + (h + 1) * HEAD_DIM]


def _attn_prompt(proj32, proj16, sinks, q_norm, k_norm):
    _, B, T, W = proj32.shape
    assert W == GROUP * HEAD_DIM == 2 * KV_WIDTH
    rows = ATT_QB * WINDOW
    q_slab = 6 // N_KV_HEADS
    kv_slab = 2
    prev = lambda n: jnp.maximum(ATT_QB * n - 1, 0)
    return pl.pallas_call(
        _attn_prompt_kernel,
        grid=(B, T // rows),
        in_specs=[
            pl.BlockSpec(memory_space=pltpu.SMEM),
            pl.BlockSpec((N_KV_HEADS, None, rows, W), lambda b, n: (q_slab, b, n, 0)),
            pl.BlockSpec((None, None, rows, W), lambda b, n: (kv_slab, b, n, 0)),
            pl.BlockSpec((None, None, WINDOW, W), lambda b, n: (kv_slab, b, prev(n), 0)),
            pl.BlockSpec((1, HEAD_DIM), lambda b, n: (0, 0)),
            pl.BlockSpec((1, HEAD_DIM), lambda b, n: (0, 0)),
        ],
        out_specs=(
            pl.BlockSpec((None, rows, ATT_WIDTH), lambda b, n: (b, n, 0)),
            pl.BlockSpec((None, WINDOW, KV_WIDTH), lambda b, n: (b, 0, 0)),
            pl.BlockSpec((None, WINDOW, KV_WIDTH), lambda b, n: (b, 0, 0)),
        ),
        out_shape=(
            jax.ShapeDtypeStruct((B, T, ATT_WIDTH), BF16),
            jax.ShapeDtypeStruct((B, WINDOW, KV_WIDTH), F32),
            jax.ShapeDtypeStruct((B, WINDOW, KV_WIDTH), F32),
        ),
        compiler_params=_cparams(("parallel", "arbitrary")),
        name="attn_prompt",
    )(sinks, proj16, proj32, proj32, q_norm, k_norm)


def _row_to_col(x_row, eye):
    return jnp.sum(jnp.where(eye, x_row, 0.0), axis=1, keepdims=True)


MIX_RB = 4


def _mix_sample_kernel(sink_ref, p_ref, s_ref, ck_ref, cv_ref, lb_ref,
                       gn_ref, qn_ref, kn_ref, *rest):
    o_ref, ns_ref, nk_ref, nv_ref = rest[-4:]
    for r in range(p_ref.shape[0]):
        _mix_one_request(sink_ref, p_ref.at[r], s_ref.at[r], ck_ref.at[r], cv_ref.at[r], lb_ref,
                         gn_ref, qn_ref, kn_ref, o_ref.at[r], ns_ref.at[r], nk_ref.at[r],
                         nv_ref.at[r])


def _mix_one_request(sink_ref, p_ref, s_ref, ck_ref, cv_ref, lb_ref,
                     gn_ref, qn_ref, kn_ref, o_ref, ns_ref, nk_ref, nv_ref):
    W = WINDOW
    er = lax.broadcasted_iota(jnp.int32, (HEAD_DIM, HEAD_DIM), 0)
    ec = lax.broadcasted_iota(jnp.int32, (HEAD_DIM, HEAD_DIM), 1)
    eye = er == ec
    g_norm = gn_ref[...]
    q_norm = qn_ref[...]
    k_norm = kn_ref[...]

    def cols(off, width=HEAD_DIM):
        return p_ref[:, off:off + width]

    for h in range(N_LIN_HEADS):
        c0 = h * HEAD_DIM
        q = _silu(cols(c0))
        lb = lb_ref[:, c0:c0 + HEAD_DIM]
        _, fp = _forget_gate(cols(LIN_WIDTH + c0), lb)
        f_col = _row_to_col(lb + fp, eye)
        k_col = 1.0 - f_col
        v = cols(2 * LIN_WIDTH + c0)
        s_new = s_ref[h] * f_col + k_col * v
        ns_ref[h] = s_new
        q8 = jnp.broadcast_to(q, (8, HEAD_DIM)).astype(BF16)
        o = jnp.dot(q8, s_new.astype(BF16), preferred_element_type=F32)[0:1, :]
        o = _rms_rows(o, g_norm) * _silu(cols(3 * LIN_WIDTH + c0))
        o_ref[:, c0:c0 + HEAD_DIM] = o

    row8 = lax.broadcasted_iota(jnp.int32, (8, 1), 0)
    row8_k = lax.broadcasted_iota(jnp.int32, (8, HEAD_DIM), 0)
    lane = lax.broadcasted_iota(jnp.int32, (8, W), 1)
    dist_c = (W - lane).astype(F32)
    rows_w = lax.broadcasted_iota(jnp.int32, (W, HEAD_DIM), 0)
    for h in range(N_KV_HEADS):
        hs = slice(h * HEAD_DIM, (h + 1) * HEAD_DIM)
        k_new = _rms_rows(cols(OFF_K_A + h * HEAD_DIM), k_norm)
        v_new = cols(OFF_V_A + h * HEAD_DIM)
        kc = ck_ref[:, h, :]
        vc = cv_ref[:, h, :]
        nk_ref[:, h, :] = jnp.where(rows_w == W - 1, k_new, pltpu.roll(kc, W - 1, 0))
        nv_ref[:, h, :] = jnp.where(rows_w == W - 1, v_new, pltpu.roll(vc, W - 1, 0))

        q4 = jnp.zeros((8, HEAD_DIM), F32)
        slope = jnp.zeros((8, 1), F32)
        sink = jnp.zeros((8, 1), F32)
        for g in range(GROUP):
            a = h * GROUP + g
            qg = _rms_rows(cols(OFF_Q_A + a * HEAD_DIM), q_norm)
            q4 = jnp.where(row8_k == g, qg, q4)
            slope = jnp.where(row8 == g, ALIBI_SLOPES[a], slope)
            sink = jnp.where(row8 == g, sink_ref[a], sink)
        q4b = q4.astype(BF16)
        s_c = lax.dot_general(q4b, kc.astype(BF16), (((1,), (1,)), ((), ())),
                              preferred_element_type=F32)
        s_c = s_c * ATT_SCALE - slope * dist_c
        s_n = jnp.sum(q4 * k_new, axis=-1, keepdims=True) * ATT_SCALE
        m = jnp.maximum(jnp.maximum(jnp.max(s_c, axis=-1, keepdims=True), s_n), sink)
        p_c = jnp.exp(s_c - m)
        p_n = jnp.exp(s_n - m)
        den = jnp.sum(p_c, axis=-1, keepdims=True) + p_n + jnp.exp(sink - m)
        o = jnp.dot(p_c.astype(BF16), vc.astype(BF16), preferred_element_type=F32)
        o = (o + p_n * v_new) / den
        for g in range(GROUP):
            a = h * GROUP + g
            o_ref[:, LIN_WIDTH + a * HEAD_DIM:LIN_WIDTH + (a + 1) * HEAD_DIM] = o[g:g + 1, :]


def _mix_sample(proj, state, cache_k, cache_v, layer, sinks, lb, g_norm, q_norm, k_norm, stacked):
    B = proj.shape[0]
    rb = MIX_RB
    vec = pl.BlockSpec((1, HEAD_DIM), lambda b: (0, 0))
    lbs = pl.BlockSpec((1, LIN_WIDTH), lambda b: (0, 0))
    cache_in = pl.BlockSpec((None, rb, WINDOW, N_KV_HEADS, HEAD_DIM),
                            lambda b: (layer, b, 0, 0, 0))
    st_in = pl.BlockSpec((None, rb, N_LIN_HEADS, HEAD_DIM, HEAD_DIM),
                         lambda b: (layer, b, 0, 0, 0))
    n_in = 9
    aliases = {n_in + k: 1 + k for k in range(len(stacked))}
    return pl.pallas_call(
        _mix_sample_kernel,
        grid=(B // rb,),
        in_specs=[
            pl.BlockSpec(memory_space=pltpu.SMEM),
            pl.BlockSpec((rb, 1, IN_WIDTH), lambda b: (b, 0, 0)),
            st_in, cache_in, cache_in, lbs, vec, vec, vec,
        ] + [pl.BlockSpec(memory_space=pl.ANY)] * len(stacked),
        out_specs=(
            pl.BlockSpec((rb, 1, MIX_WIDTH), lambda b: (b, 0, 0)),
            st_in, cache_in, cache_in,
        ),
        out_shape=(
            jax.ShapeDtypeStruct((B, 1, MIX_WIDTH), F32),
            jax.ShapeDtypeStruct(state.shape, F32),
            jax.ShapeDtypeStruct(cache_k.shape, F32),
            jax.ShapeDtypeStruct(cache_v.shape, F32),
        ),
        input_output_aliases=aliases,
        compiler_params=_cparams(("parallel",)),
        name="mix_sample",
    )(sinks, proj, state, cache_k, cache_v, lb, g_norm, q_norm, k_norm, *stacked)


TILE_N = 512
P_TM = 1024
P_TM_OUT = 2048
F32_SLABS = (LIN_WIDTH // TILE_N, LIN_WIDTH // TILE_N + 1, OFF_K_A // TILE_N)


def kernel(x_prompt, x_sample, cache_k, cache_v, state_hgrn, c_prompt, c_sample, lower_bounds,
           w_ada, b_ada, norm_mix, w_in, q_norm, k_norm, attn_sinks, g_norm, w_out, norm_ffn,
           w_gate, w_up, w_down):
    B, T, _ = x_prompt.shape
    BS = x_sample.shape[0]

    lb_all = _lower_bounds(lower_bounds)

    c_all = jnp.concatenate(
        [c_prompt, c_sample, jnp.zeros((MOD_ROWS - B - BS, D_MODEL), F32)], axis=0)
    mod = _modulation(c_all, w_ada, b_ada)

    yp = x_prompt
    ys = x_sample.reshape(1, BS, D_MODEL)
    pk, pv, ps = [], [], []
    sample_out = ()
    for l in range(DEPTH):
        mod_p = mod[l, :B].reshape(B, 1, 6 * D_MODEL)
        mod_s = mod[l, B:B + BS].reshape(1, BS, 6 * D_MODEL)
        gain_mix = norm_mix[l].reshape(1, D_MODEL)
        gain_ffn = norm_ffn[l].reshape(1, D_MODEL)
        lb = lb_all[l].reshape(1, LIN_WIDTH)
        gn = g_norm[l].reshape(1, HEAD_DIM)
        qn = q_norm[l].reshape(1, HEAD_DIM)
        kn = k_norm[l].reshape(1, HEAD_DIM)
        sinks = attn_sinks[l].astype(F32)

        proj_s, w_in_b = _norm_matmul(ys, mod_s, 0, gain_mix, w_in, l, BS, TILE_N)
        mixed_s, *sample_out = _mix_sample(
            proj_s.reshape(BS, 1, IN_WIDTH), state_hgrn, cache_k, cache_v, l,
            sinks, lb, gn, qn, kn, sample_out)
        ys, w_out_b = _matmul_residual(mixed_s.reshape(1, BS, MIX_WIDTH), w_out, l, ys, mod_s,
                                       2 * D_MODEL, BS, TILE_N)
        ff_s, w_gate_b, w_up_b = _norm_gate_up(ys, mod_s, 3, gain_ffn, w_gate, w_up, l, BS, TILE_N)
        ys, w_down_b = _matmul_residual(ff_s, w_down, l, ys, mod_s, 5 * D_MODEL, BS, TILE_N)

        proj32, proj16 = _in_proj_prompt(yp, mod_p, gain_mix, w_in_b, F32_SLABS, P_TM)
        o_l, s_p = _hgrn_prompt(proj32, proj16, lb, gn)
        o_a, k_p, v_p = _attn_prompt(proj32, proj16, sinks, qn, kn)
        yp = _res_matmul_prompt([o_l, o_a], [(w_out_b, 0), (w_out_b, 1)], yp, mod_p, 2,
                                P_TM_OUT, "out_proj")
        ff = _gate_up_prompt(yp, mod_p, 3, gain_ffn, w_gate_b, w_up_b, P_TM)
        yp = _res_matmul_prompt([ff], [w_down_b], yp, mod_p, 5, P_TM, "down_proj")
        pk.append(k_p.reshape(B, WINDOW, N_KV_HEADS, HEAD_DIM))
        pv.append(v_p.reshape(B, WINDOW, N_KV_HEADS, HEAD_DIM))
        ps.append(s_p)

    s_s, k_s, v_s = sample_out
    return (yp, ys.reshape(BS, 1, D_MODEL), jnp.stack(pk), jnp.stack(pv), jnp.stack(ps),
            k_s, v_s, s_s)
```

```python
import functools
import math

import jax
import jax.numpy as jnp
from jax import lax
from jax.experimental import pallas as pl
from jax.experimental.pallas import tpu as pltpu

F32 = jnp.float32
BF16 = jnp.bfloat16

D_MODEL = 2048
DEPTH = 4
HEAD_DIM = 128
N_LIN_HEADS = 8
LIN_WIDTH = N_LIN_HEADS * HEAD_DIM
N_ATT_HEADS = 8
N_KV_HEADS = 2
GROUP = N_ATT_HEADS // N_KV_HEADS
ATT_WIDTH = N_ATT_HEADS * HEAD_DIM
KV_WIDTH = N_KV_HEADS * HEAD_DIM
MIX_WIDTH = LIN_WIDTH + ATT_WIDTH
WINDOW = 128
D_FF = 5632
IN_WIDTH = 4 * LIN_WIDTH + ATT_WIDTH + 2 * KV_WIDTH
EPS = 1e-6
ATT_SCALE = HEAD_DIM ** -0.5
ALIBI_SLOPES = tuple(2.0 ** (-8.0 * (a + 1) / N_ATT_HEADS) for a in range(N_ATT_HEADS))

OFF_Q_A = 4 * LIN_WIDTH
OFF_K_A = OFF_Q_A + ATT_WIDTH
OFF_V_A = OFF_K_A + KV_WIDTH

MOD_ROWS = 40
HGRN_CHUNK = 128
HGRN_SUB = 8
LOG2E = math.log2(math.e)

VMEM_LIMIT = 56 * 1024 * 1024


def _cparams(sem):
    return pltpu.CompilerParams(dimension_semantics=sem, vmem_limit_bytes=VMEM_LIMIT)


def _silu(x):
    return x / (1.0 + jnp.exp(-x))


def _rms_rows(x, g):
    ms = jnp.mean(x * x, axis=-1, keepdims=True)
    return x * lax.rsqrt(ms + EPS) * g


def _forget_gate(f_raw, lb):
    t = jnp.exp(-jnp.abs(f_raw))
    inv = 1.0 / (1.0 + t)
    sig = jnp.where(f_raw >= 0.0, inv, t * inv)
    return t, (1.0 - lb) * sig


def _log_forget(f_raw, lb):
    t, fp = _forget_gate(f_raw, lb)
    log_sig = jnp.minimum(f_raw, 0.0) - jnp.log(1.0 + t)
    return jnp.where(lb > 0.0, jnp.log(lb + fp), log_sig), fp


def _lb_kernel(lb_ref, o_ref):
    x = lb_ref[...]
    m = jnp.max(x, axis=0, keepdims=True)
    e = jnp.exp(x - m)
    p = e / jnp.sum(e, axis=0, keepdims=True)
    acc = jnp.zeros_like(p[0:1])
    for l in range(DEPTH):
        if l > 0:
            acc = acc + p[l:l + 1]
        o_ref[l:l + 1, :] = acc


def _lower_bounds(lower_bounds):
    return pl.pallas_call(
        _lb_kernel,
        out_shape=jax.ShapeDtypeStruct((DEPTH, LIN_WIDTH), F32),
        name="lower_bounds",
    )(lower_bounds.astype(F32))


ADA_TN = 1024


def _ada_kernel(c_ref, w_ref, b_ref, o_ref):
    s = _silu(c_ref[...]).astype(BF16)
    acc = jnp.dot(s, w_ref[...].astype(BF16), preferred_element_type=F32)
    o_ref[...] = acc + b_ref[...]


def _modulation(c_all, w_ada, b_ada):
    n = 6 * D_MODEL
    return pl.pallas_call(
        _ada_kernel,
        grid=(DEPTH, n // ADA_TN),
        in_specs=[
            pl.BlockSpec((MOD_ROWS, D_MODEL), lambda l, j: (0, 0)),
            pl.BlockSpec((None, D_MODEL, ADA_TN), lambda l, j: (l, 0, j)),
            pl.BlockSpec((None, 1, ADA_TN), lambda l, j: (l, 0, j)),
        ],
        out_specs=pl.BlockSpec((None, MOD_ROWS, ADA_TN), lambda l, j: (l, 0, j)),
        out_shape=jax.ShapeDtypeStruct((DEPTH, MOD_ROWS, n), F32),
        compiler_params=_cparams(("parallel", "parallel")),
        name="modulation",
    )(c_all, w_ada, b_ada.reshape(DEPTH, 1, n))


NORM_ROWS = 16
NORM_UNROLL = 4


def _modulated_norm_into(h_scr, x_ref, sh_ref, sc_ref, g_ref):
    rows = x_ref.shape[0]
    ch = min(rows, NORM_ROWS)
    per_row = sc_ref.shape[0] != 1
    gain = g_ref[...]
    if not per_row:
        gain = gain * (1.0 + sc_ref[...])
        shift = sh_ref[...]

    def body(c, carry):
        rs = pl.ds(pl.multiple_of(c * ch, ch), ch)
        x = x_ref[rs, :]
        inv = lax.rsqrt(jnp.mean(x * x, axis=-1, keepdims=True) + EPS)
        if per_row:
            h = (x * inv * gain) * (1.0 + sc_ref[rs, :]) + sh_ref[rs, :]
        else:
            h = (x * inv) * gain + shift
        h_scr[rs, :] = h.astype(BF16)
        return carry

    lax.fori_loop(0, rows // ch, body, 0, unroll=min(NORM_UNROLL, rows // ch))


def _w_spec(w, layer, tn):
    return pl.BlockSpec((None, w.shape[1], tn), lambda b, i, j: (layer, 0, j))


def _wb_out(w, tn):
    rows, n = w.shape[-2:]
    return (pl.BlockSpec((None, rows, tn), lambda b, i, j: (j, 0, 0)),
            jax.ShapeDtypeStruct((n // tn, rows, tn), BF16))


def _norm_mm_kernel(x_ref, sh_ref, sc_ref, g_ref, w_ref, o_ref, wb_ref, h_scr):
    @pl.when(pl.program_id(2) == 0)
    def _():
        _modulated_norm_into(h_scr, x_ref, sh_ref, sc_ref, g_ref)

    w = w_ref[...].astype(BF16)
    wb_ref[...] = w
    o_ref[...] = jnp.dot(h_scr[...], w, preferred_element_type=F32)


def _mod_specs(mod, mod_col, tm):
    if mod.shape[1] == 1:
        return [pl.BlockSpec((None, 1, D_MODEL), lambda b, i, j, c=mod_col + k: (b, 0, c))
                for k in range(2)]
    return [pl.BlockSpec((None, tm, D_MODEL), lambda b, i, j, c=mod_col + k: (b, i, c))
            for k in range(2)]


def _lookup(j, table):
    out = table[-1]
    for k in range(len(table) - 2, -1, -1):
        out = jnp.where(j == k, table[k], out)
    return out


def _norm_matmul(x, mod, mod_col, gain, w, layer, tm, tn):
    G, R, _ = x.shape
    nt = w.shape[-1] // tn
    assert G * (R // tm) == 1
    wb_spec, wb_shape = _wb_out(w, tn)
    out_specs = [pl.BlockSpec((None, tm, tn), lambda b, i, j: (b, i, j)), wb_spec]
    out_shape = [jax.ShapeDtypeStruct((G, R, nt * tn), F32), wb_shape]
    return pl.pallas_call(
        _norm_mm_kernel,
        grid=(G, R // tm, nt),
        in_specs=[pl.BlockSpec((None, tm, D_MODEL), lambda b, i, j: (b, i, 0))]
        + _mod_specs(mod, mod_col, tm)
        + [pl.BlockSpec((1, D_MODEL), lambda b, i, j: (0, 0)), _w_spec(w, layer, tn)],
        out_specs=out_specs,
        out_shape=out_shape,
        scratch_shapes=[pltpu.VMEM((tm, D_MODEL), BF16)],
        compiler_params=_cparams(("parallel", "parallel", "arbitrary")),
        name="norm_in_proj",
    )(x, mod, mod, gain, w)


def _norm_gu_kernel(x_ref, sh_ref, sc_ref, g_ref, wg_ref, wu_ref, o_ref, wgb_ref, wub_ref, h_scr):
    @pl.when(pl.program_id(2) == 0)
    def _():
        _modulated_norm_into(h_scr, x_ref, sh_ref, sc_ref, g_ref)

    h = h_scr[...]
    wg = wg_ref[...].astype(BF16)
    wu = wu_ref[...].astype(BF16)
    wgb_ref[...] = wg
    wub_ref[...] = wu
    gate = jnp.dot(h, wg, preferred_element_type=F32)
    up = jnp.dot(h, wu, preferred_element_type=F32)
    o_ref[...] = (_silu(gate) * up).astype(o_ref.dtype)


def _norm_gate_up(x, mod, mod_col, gain, wg, wu, layer, tm, tn):
    G, R, _ = x.shape
    nt = wg.shape[-1] // tn
    assert G * (R // tm) == 1
    wb_spec, wb_shape = _wb_out(wg, tn)
    return pl.pallas_call(
        _norm_gu_kernel,
        grid=(G, R // tm, nt),
        in_specs=[pl.BlockSpec((None, tm, D_MODEL), lambda b, i, j: (b, i, 0))]
        + _mod_specs(mod, mod_col, tm)
        + [pl.BlockSpec((1, D_MODEL), lambda b, i, j: (0, 0)),
           _w_spec(wg, layer, tn), _w_spec(wu, layer, tn)],
        out_specs=[pl.BlockSpec((None, tm, tn), lambda b, i, j: (b, i, j)), wb_spec, wb_spec],
        out_shape=[jax.ShapeDtypeStruct((G, R, nt * tn), BF16), wb_shape, wb_shape],
        scratch_shapes=[pltpu.VMEM((tm, D_MODEL), BF16)],
        compiler_params=_cparams(("parallel", "parallel", "arbitrary")),
        name="norm_gate_up",
    )(x, mod, mod, gain, wg, wu)


def _res_matmul_kernel(*refs, n_a):
    a_refs, w_refs = refs[:n_a], refs[n_a:2 * n_a]
    x_ref, gt_ref, o_ref = refs[2 * n_a:]
    acc = None
    for a_ref, w_ref in zip(a_refs, w_refs):
        slabs = [a_ref[s] for s in range(a_ref.shape[0])] if len(a_ref.shape) == 3 else [a_ref[...]]
        ts = slabs[0].shape[1]
        for s, a in enumerate(slabs):
            part = jnp.dot(a, w_ref[s * ts:(s + 1) * ts, :], preferred_element_type=F32)
            acc = part if acc is None else acc + part
    o_ref[...] = x_ref[...] + gt_ref[...] * acc


def _res_matmul_prompt(a_list, w_list, x, mod, gate_col, tm, name):
    G, R, _ = x.shape
    a_specs, w_specs, ws = [], [], []
    for a, w in zip(a_list, w_list):
        w, row_blk = w if isinstance(w, tuple) else (w, 0)
        if a.ndim == 4:
            S, _, _, ts = a.shape
            a_specs.append(pl.BlockSpec((S, None, tm, ts), lambda b, i, j: (0, b, i, 0)))
            k = S * ts
        else:
            k = a.shape[2]
            a_specs.append(pl.BlockSpec((None, tm, k), lambda b, i, j: (b, i, 0)))
        nt, _, tn = w.shape
        w_specs.append(pl.BlockSpec((None, k, tn), lambda b, i, j, r=row_blk: (j, r, 0)))
        ws.append(w)
    tile = pl.BlockSpec((None, tm, tn), lambda b, i, j: (b, i, j))
    return pl.pallas_call(
        functools.partial(_res_matmul_kernel, n_a=len(a_list)),
        grid=(G, R // tm, nt),
        in_specs=a_specs + w_specs
        + [tile, pl.BlockSpec((None, 1, tn), lambda b, i, j: (b, 0, gate_col * nt + j))],
        out_specs=tile,
        out_shape=jax.ShapeDtypeStruct((G, R, D_MODEL), F32),
        compiler_params=_cparams(("parallel", "parallel", "arbitrary")),
        name=name,
    )(*a_list, *ws, x, mod)


def _mm_res_kernel(a_ref, w_ref, x_ref, gt_ref, o_ref, wb_ref):
    w = w_ref[...].astype(BF16)
    wb_ref[...] = w
    acc = jnp.dot(a_ref[...].astype(BF16), w, preferred_element_type=F32)
    o_ref[...] = x_ref[...] + gt_ref[...] * acc


def _matmul_residual(a, w, layer, x, mod, mod_off, tm, tn):
    G, R, K = a.shape
    assert G * (R // tm) == 1 and mod.shape[1] == R
    gcol = mod_off // tn
    wb_spec, wb_shape = _wb_out(w, tn)
    return pl.pallas_call(
        _mm_res_kernel,
        grid=(G, R // tm, D_MODEL // tn),
        in_specs=[
            pl.BlockSpec((None, tm, K), lambda b, i, j: (b, i, 0)),
            _w_spec(w, layer, tn),
            pl.BlockSpec((None, tm, tn), lambda b, i, j: (b, i, j)),
            pl.BlockSpec((None, tm, tn), lambda b, i, j: (b, i, gcol + j)),
        ],
        out_specs=[pl.BlockSpec((None, tm, tn), lambda b, i, j: (b, i, j)), wb_spec],
        out_shape=[jax.ShapeDtypeStruct((G, R, D_MODEL), F32), wb_shape],
        compiler_params=_cparams(("parallel", "parallel", "arbitrary")),
        name="matmul_residual",
    )(a, w, x, mod)


def _split3_bf16(x):
    hi = x.astype(BF16)
    r1 = x - hi.astype(F32)
    mid = r1.astype(BF16)
    lo = (r1 - mid.astype(F32)).astype(BF16)
    return hi, mid, lo


HGRN_HP = 4
HGRN_SEP_MIN = 32


def _hgrn_consts():
    C, P, D = HGRN_CHUNK, HGRN_SUB, HEAD_DIM
    row = lax.broadcasted_iota(jnp.int32, (C, C), 0)
    col = lax.broadcasted_iota(jnp.int32, (C, C), 1)
    tri = (row >= col).astype(BF16)
    sel_r = lax.broadcasted_iota(jnp.int32, (P * D, C), 0)
    sel_c = lax.broadcasted_iota(jnp.int32, (P * D, C), 1)
    psh = P.bit_length() - 1
    sel = ((sel_c & (P - 1)) == (sel_r >> (D.bit_length() - 1))).astype(BF16)
    keep_diag = ((col >> psh) == (row >> psh)) & ((col & (P - 1)) <= (row & (P - 1)))
    levels = []
    L = C // 2
    while L >= P:
        levels.append(L)
        L //= 2
    keep_level = {}
    for L in levels:
        if L < HGRN_SEP_MIN:
            sh = L.bit_length() - 1
            keep_level[L] = (((row >> sh) & 1) == 1) & ((col >> sh) == (row >> sh) - 1)
    return tri, sel, keep_diag, levels, keep_level


def _hgrn_heads(q_raw, f_raw, i_raw, g_raw, lbs, g_norm, states, consts, bk_scr):
    C, P, D = HGRN_CHUNK, HGRN_SUB, HEAD_DIM
    tri, sel, keep_diag, levels, keep_level = consts
    n = len(q_raw)
    qs, ks, vs, parts = [], [], [], []
    for hh in range(n):
        logf, fp = _log_forget(f_raw[hh], lbs[hh])
        qs.append(_silu(q_raw[hh].astype(F32)))
        ks.append((1.0 - lbs[hh]) - fp)
        vs.append(i_raw[hh].astype(BF16))
        parts.extend(_split3_bf16(logf * LOG2E))
    cum = jnp.dot(tri, jnp.concatenate(parts, axis=1), preferred_element_type=F32)

    outs, new_states = [], []
    for hh in range(n):
        q, k, v16, st = qs[hh], ks[hh], vs[hh], states[hh]
        b = (cum[:, (3 * hh) * D:(3 * hh + 1) * D] + cum[:, (3 * hh + 1) * D:(3 * hh + 2) * D]
             + cum[:, (3 * hh + 2) * D:(3 * hh + 3) * D])
        b_end = b[C - 1:C, :]
        o = lax.dot_general((q * jnp.exp2(b)).astype(BF16), st.astype(BF16),
                            (((1,), (1,)), ((), ())), preferred_element_type=F32)

        q_sep, k_sep, a_masked = [], [], []
        for L in levels:
            q_rows, k_rows = [], []
            for p in range(C // (2 * L)):
                lo, mid, hi = 2 * L * p, 2 * L * p + L, 2 * L * (p + 1)
                ref = b[mid - 1:mid, :]
                zf = jnp.zeros((L, D), F32)
                kk = jnp.concatenate([k[lo:mid] * jnp.exp2(ref - b[lo:mid]), zf], axis=0)
                qq = jnp.concatenate([zf, q[mid:hi] * jnp.exp2(b[mid:hi] - ref)], axis=0)
                q_rows.append(qq.astype(BF16))
                k_rows.append(kk.astype(BF16))
            if L >= HGRN_SEP_MIN:
                for p in range(len(q_rows)):
                    above = jnp.zeros((2 * L * p, D), BF16)
                    below = jnp.zeros((C - 2 * L * (p + 1), D), BF16)
                    for rows_p, dst in ((q_rows[p], q_sep), (k_rows[p], k_sep)):
                        dst.append(jnp.concatenate(
                            [y for y in (above, rows_p, below) if y.shape[0] > 0], axis=0))
            else:
                a_l = lax.dot_general(jnp.concatenate(q_rows, axis=0),
                                      jnp.concatenate(k_rows, axis=0),
                                      (((1,), (1,)), ((), ())), preferred_element_type=F32)
                a_masked.append((keep_level[L], a_l))
        a = lax.dot_general(jnp.concatenate(q_sep, axis=1), jnp.concatenate(k_sep, axis=1),
                            (((1,), (1,)), ((), ())), preferred_element_type=F32)
        for keep, a_l in a_masked:
            a = jnp.where(keep, a_l, a)

        bk_scr[hh, 0] = b
        bk_scr[hh, 1] = k
        zs = []
        for s in range(P):
            bs, ksb = [jnp.concatenate(
                [bk_scr[hh, w, pl.ds(m * P + s, P, stride=0), :] for m in range(C // P)], axis=0)
                for w in range(2)]
            e = jnp.exp2(jnp.minimum(b - bs, 0.0))
            zs.append((q * ksb * e).astype(BF16))
        a_d = jnp.dot(jnp.concatenate(zs, axis=1), sel, preferred_element_type=F32)
        a = jnp.where(keep_diag, a_d, a)

        o = o + jnp.dot(a.astype(BF16), v16, preferred_element_type=F32)

        k_end = (k * jnp.exp2(b_end - b)).astype(BF16)
        upd = lax.dot_general(v16, k_end, (((0,), (0,)), ((), ())), preferred_element_type=F32)
        new_states.append(st * jnp.exp2(b_end) + upd)
        outs.append(_rms_rows(o, g_norm) * _silu(g_raw[hh].astype(F32)))
    return outs, new_states


def _lookahead_slot(x_ref, sh0_ref, sc0_ref, g_ref, h_scr, n_i):
    tile = pl.program_id(0) * n_i + pl.program_id(1)

    @pl.when((tile == 0) & (pl.program_id(2) == 0))
    def _():
        _modulated_norm_into(h_scr.at[0], x_ref, sh0_ref, sc0_ref, g_ref)

    return tile % 2


def _lookahead_slab(x_ref, shn_ref, scn_ref, g_ref, h_scr, cur, n_steps):
    j = pl.program_id(2)
    tm = x_ref.shape[0]
    per_step = -(-tm // ((n_steps - 1) * NORM_ROWS)) * NORM_ROWS
    gain = g_ref[...] * (1.0 + scn_ref[...])
    shift = shn_ref[...]
    r0 = jnp.clip((j - 1) * per_step, 0, tm - per_step)
    nxt = h_scr.at[1 - cur]
    for k in range(per_step // NORM_ROWS):
        rs = pl.ds(pl.multiple_of(r0 + k * NORM_ROWS, NORM_ROWS), NORM_ROWS)
        x = x_ref[rs, :]
        inv = lax.rsqrt(jnp.mean(x * x, axis=-1, keepdims=True) + EPS)
        nxt[rs, :] = ((x * inv) * gain + shift).astype(BF16)


def _lookahead_specs(x, mod, mod_col, tm, n_j):
    G, R, _ = x.shape
    n_i = R // tm
    last = G * n_i - 1

    def x_tile(b, i, j):
        t = jnp.minimum(b * n_i + i + (j > 0).astype(jnp.int32), last)
        return t // n_i, t % n_i, 0

    nxt_b = lambda b, i: jnp.minimum(b * n_i + i + 1, last) // n_i
    specs = [pl.BlockSpec((None, tm, D_MODEL), x_tile)]
    specs += [pl.BlockSpec((None, 1, D_MODEL), lambda b, i, j, c=mod_col + k: (0, 0, c))
              for k in range(2)]
    specs += [pl.BlockSpec((None, 1, D_MODEL), lambda b, i, j, c=mod_col + k: (nxt_b(b, i), 0, c))
              for k in range(2)]
    return specs, pltpu.VMEM((2, tm, D_MODEL), BF16), n_i


W_RING = 3


def _in_proj_kernel(x_ref, sh0_ref, sc0_ref, shn_ref, scn_ref, g_ref, w_hbm, o32_ref, o16_ref,
                    h_scr, w_ring, w_sem, *, n_i, n_steps, order, total):
    s = (pl.program_id(0) * n_i + pl.program_id(1)) * n_steps + pl.program_id(2)

    def tile_copy(step):
        slot = step % W_RING
        return pltpu.make_async_copy(w_hbm.at[_lookup(step % n_steps, order)],
                                     w_ring.at[slot], w_sem.at[slot])

    @pl.when(s == 0)
    def _():
        for k in range(W_RING - 1):
            tile_copy(s + k).start()

    @pl.when(s + (W_RING - 1) < total)
    def _():
        tile_copy(s + (W_RING - 1)).start()

    cur = _lookahead_slot(x_ref, sh0_ref, sc0_ref, g_ref, h_scr, n_i)
    tile_copy(s).wait()
    acc = jnp.dot(h_scr[cur], w_ring[s % W_RING], preferred_element_type=F32)
    o32_ref[...] = acc
    o16_ref[...] = acc.astype(BF16)
    _lookahead_slab(x_ref, shn_ref, scn_ref, g_ref, h_scr, cur, n_steps)


def _in_proj_prompt(x, mod, gain, wt, f32_slabs, tm):
    G, R, _ = x.shape
    nt, _, tn = wt.shape
    n32 = len(f32_slabs)
    n16 = nt - n32
    order = tuple(f32_slabs) + tuple(s for s in range(nt) if s not in f32_slabs)
    x_specs, h_scratch, n_i = _lookahead_specs(x, mod, 0, tm, nt)
    assert G * n_i * nt >= W_RING
    return pl.pallas_call(
        functools.partial(_in_proj_kernel, n_i=n_i, n_steps=nt, order=order, total=G * n_i * nt),
        grid=(G, R // tm, nt),
        in_specs=x_specs
        + [pl.BlockSpec((1, D_MODEL), lambda b, i, j: (0, 0)),
           pl.BlockSpec(memory_space=pl.ANY)],
        out_specs=[
            pl.BlockSpec((None, None, tm, tn), lambda b, i, j: (jnp.minimum(j, n32), b, i, 0)),
            pl.BlockSpec((None, None, tm, tn),
                         lambda b, i, j: (jnp.where(j < n32, n16, j - n32), b, i, 0)),
        ],
        out_shape=[jax.ShapeDtypeStruct((n32 + 1, G, R, tn), F32),
                   jax.ShapeDtypeStruct((n16 + 1, G, R, tn), BF16)],
        scratch_shapes=[h_scratch, pltpu.VMEM((W_RING, D_MODEL, tn), BF16),
                        pltpu.SemaphoreType.DMA((W_RING,))],
        compiler_params=_cparams(("arbitrary", "arbitrary", "arbitrary")),
        name="in_proj_prompt",
    )(x, mod, mod, mod, mod, gain, wt)


def _gate_up_kernel(x_ref, sh0_ref, sc0_ref, shn_ref, scn_ref, g_ref, wg_ref, wu_ref, o_ref,
                    h_scr, *, n_i, n_steps):
    cur = _lookahead_slot(x_ref, sh0_ref, sc0_ref, g_ref, h_scr, n_i)
    h = h_scr[cur]
    gate = jnp.dot(h, wg_ref[...], preferred_element_type=F32)
    up = jnp.dot(h, wu_ref[...], preferred_element_type=F32)
    o_ref[...] = (_silu(gate) * up).astype(o_ref.dtype)
    _lookahead_slab(x_ref, shn_ref, scn_ref, g_ref, h_scr, cur, n_steps)


def _gate_up_prompt(x, mod, mod_col, gain, wgt, wut, tm):
    G, R, _ = x.shape
    nt, _, tn = wgt.shape
    x_specs, h_scratch, n_i = _lookahead_specs(x, mod, mod_col, tm, nt)
    w_spec = pl.BlockSpec((None, D_MODEL, tn), lambda b, i, j: (j, 0, 0))
    return pl.pallas_call(
        functools.partial(_gate_up_kernel, n_i=n_i, n_steps=nt),
        grid=(G, R // tm, nt),
        in_specs=x_specs + [pl.BlockSpec((1, D_MODEL), lambda b, i, j: (0, 0)), w_spec, w_spec],
        out_specs=pl.BlockSpec((None, None, tm, tn), lambda b, i, j: (j, b, i, 0)),
        out_shape=jax.ShapeDtypeStruct((nt, G, R, tn), BF16),
        scratch_shapes=[h_scratch],
        compiler_params=_cparams(("arbitrary", "arbitrary", "arbitrary")),
        name="gate_up_prompt",
    )(x, mod, mod, mod, mod, gain, wgt, wut)


HGRN_TB = 2048


def _hgrn_prompt_kernel(q_ref, f_ref, i_ref, g_ref, lb_ref, gn_ref, o_ref, s_ref, st_scr, bk_scr):
    C, D = HGRN_CHUNK, HEAD_DIM
    t_idx = pl.program_id(2)
    consts = _hgrn_consts()
    g_norm = gn_ref[...]
    heads = range(HGRN_HP)
    lane = lambda hh: slice(hh * D, (hh + 1) * D)
    lbs = [lb_ref[:, lane(hh)] for hh in heads]

    @pl.when(t_idx == 0)
    def _():
        st_scr[...] = jnp.zeros_like(st_scr)

    def chunk_body(c, carry):
        rs = pl.ds(pl.multiple_of(c * C, C), C)
        outs, new_states = _hgrn_heads(
            [q_ref[rs, lane(hh)] for hh in heads], [f_ref[rs, lane(hh)] for hh in heads],
            [i_ref[rs, lane(hh)] for hh in heads], [g_ref[rs, lane(hh)] for hh in heads],
            lbs, g_norm, [st_scr[hh] for hh in heads], consts, bk_scr)
        for hh in heads:
            st_scr[hh] = new_states[hh]
            o_ref[rs, lane(hh)] = outs[hh].astype(o_ref.dtype)
        return carry

    lax.fori_loop(0, q_ref.shape[0] // C, chunk_body, 0)

    @pl.when(t_idx == pl.num_programs(2) - 1)
    def _():
        for hh in heads:
            s_ref[hh] = st_scr[hh].T


def _hgrn_prompt(proj32, proj16, lb, g_norm):
    _, B, T, W = proj32.shape
    assert W == HGRN_HP * HEAD_DIM
    seq = lambda slab: pl.BlockSpec((None, None, HGRN_TB, W),
                                    lambda b, h, t: (slab + h, b, t, 0))
    return pl.pallas_call(
        _hgrn_prompt_kernel,
        grid=(B, N_LIN_HEADS // HGRN_HP, T // HGRN_TB),
        in_specs=[
            seq(0), seq(0), seq(2), seq(4),
            pl.BlockSpec((1, W), lambda b, h, t: (0, h)),
            pl.BlockSpec((1, HEAD_DIM), lambda b, h, t: (0, 0)),
        ],
        out_specs=(
            pl.BlockSpec((None, HGRN_TB, W), lambda b, h, t: (b, t, h)),
            pl.BlockSpec((None, HGRN_HP, HEAD_DIM, HEAD_DIM), lambda b, h, t: (b, h, 0, 0)),
        ),
        out_shape=(
            jax.ShapeDtypeStruct((B, T, LIN_WIDTH), BF16),
            jax.ShapeDtypeStruct((B, N_LIN_HEADS, HEAD_DIM, HEAD_DIM), F32),
        ),
        scratch_shapes=[pltpu.VMEM((HGRN_HP, HEAD_DIM, HEAD_DIM), F32),
                        pltpu.VMEM((HGRN_HP, 2, HGRN_CHUNK, HEAD_DIM), F32)],
        compiler_params=_cparams(("parallel", "parallel", "arbitrary")),
        name="hgrn_prompt",
    )(proj16, proj32, proj16, proj16, lb, g_norm)


ATT_QB = 4


def _attn_prompt_kernel(sink_ref, q_ref, kvc_ref, kvp_ref, qn_ref, kn_ref,
                        o_ref, nk_ref, nv_ref):
    n = pl.program_id(1)
    W = WINDOW
    qi = lax.broadcasted_iota(jnp.int32, (W, 2 * W), 0)
    kj = lax.broadcasted_iota(jnp.int32, (W, 2 * W), 1)
    dist = W + qi - kj
    in_window = (dist >= 0) & (dist <= W)
    dist_in = jnp.where(in_window, dist.astype(F32), jnp.inf)
    dist_first = jnp.where(in_window & ((n > 0) | (kj >= W)), dist.astype(F32), jnp.inf)
    q_gain = qn_ref[...] * (ATT_SCALE * LOG2E)
    k_norm = kn_ref[...]

    new_k = []
    for h in range(N_KV_HEADS):
        hs = slice(h * HEAD_DIM, (h + 1) * HEAD_DIM)
        vs = slice(KV_WIDTH + h * HEAD_DIM, KV_WIDTH + (h + 1) * HEAD_DIM)
        kc = _rms_rows(kvc_ref[:, hs], k_norm)
        kp = _rms_rows(kvp_ref[:, hs], k_norm)
        new_k.append(kc[(ATT_QB - 1) * W:, :])
        k_all = jnp.concatenate([kp, kc], axis=0).astype(BF16)
        v_all = jnp.concatenate([kvp_ref[:, vs], kvc_ref[:, vs]], axis=0).astype(BF16)

        for blk in range(ATT_QB):
            rows = slice(blk * W, (blk + 1) * W)
            k2 = k_all[blk * W:(blk + 2) * W, :]
            v2 = v_all[blk * W:(blk + 2) * W, :]
            dist_m = dist_first if blk == 0 else dist_in
            for g in range(GROUP):
                a = h * GROUP + g
                cs = slice(a * HEAD_DIM, (a + 1) * HEAD_DIM)
                qh = _rms_rows(q_ref[h, rows, g * HEAD_DIM:(g + 1) * HEAD_DIM].astype(F32),
                               q_gain).astype(BF16)
                s = lax.dot_general(qh, k2, (((1,), (1,)), ((), ())), preferred_element_type=F32)
                s = s + (-ALIBI_SLOPES[a] * LOG2E) * dist_m
                sink = sink_ref[a] * LOG2E
                m = jnp.maximum(jnp.max(s, axis=-1, keepdims=True), sink)
                p = jnp.exp2(s - m)
                den = jnp.sum(p, axis=-1, keepdims=True) + jnp.exp2(sink - m)
                o = jnp.dot(p.astype(BF16), v2, preferred_element_type=F32) * (1.0 / den)
                o_ref[rows, cs] = o.astype(o_ref.dtype)

    @pl.when(n == pl.num_programs(1) - 1)
    def _():
        for h in range(N_KV_HEADS):
            hs = slice(h * HEAD_DIM, (h + 1) * HEAD_DIM)
            nk_ref[:, hs] = new_k[h]
            nv_ref[:, hs] = kvc_ref[(ATT_QB - 1) * W:,
                                    KV_WIDTH + h * HEAD_DIM:KV_WIDTH + (h + 1) * HEAD_DIM]


def _attn_prompt(proj32, proj16, sinks, q_norm, k_norm):
    _, B, T, W = proj32.shape
    assert W == GROUP * HEAD_DIM == 2 * KV_WIDTH
    rows = ATT_QB * WINDOW
    q_slab = 6 // N_KV_HEADS
    kv_slab = 2
    prev = lambda n: jnp.maximum(ATT_QB * n - 1, 0)
    return pl.pallas_call(
        _attn_prompt_kernel,
        grid=(B, T // rows),
        in_specs=[
            pl.BlockSpec(memory_space=pltpu.SMEM),
            pl.BlockSpec((N_KV_HEADS, None, rows, W), lambda b, n: (q_slab, b, n, 0)),
            pl.BlockSpec((None, None, rows, W), lambda b, n: (kv_slab, b, n, 0)),
            pl.BlockSpec((None, None, WINDOW, W), lambda b, n: (kv_slab, b, prev(n), 0)),
            pl.BlockSpec((1, HEAD_DIM), lambda b, n: (0, 0)),
            pl.BlockSpec((1, HEAD_DIM), lambda b, n: (0, 0)),
        ],
        out_specs=(
            pl.BlockSpec((None, rows, ATT_WIDTH), lambda b, n: (b, n, 0)),
            pl.BlockSpec((None, WINDOW, KV_WIDTH), lambda b, n: (b, 0, 0)),
            pl.BlockSpec((None, WINDOW, KV_WIDTH), lambda b, n: (b, 0, 0)),
        ),
        out_shape=(
            jax.ShapeDtypeStruct((B, T, ATT_WIDTH), BF16),
            jax.ShapeDtypeStruct((B, WINDOW, KV_WIDTH), F32),
            jax.ShapeDtypeStruct((B, WINDOW, KV_WIDTH), F32),
        ),
        compiler_params=_cparams(("parallel", "arbitrary")),
        name="attn_prompt",
    )(sinks, proj16, proj32, proj32, q_norm, k_norm)


def _row_to_col(x_row, eye):
    return jnp.sum(jnp.where(eye, x_row, 0.0), axis=1, keepdims=True)


MIX_RB = 4


def _mix_sample_kernel(sink_ref, p_ref, s_ref, ck_ref, cv_ref, lb_ref,
                       gn_ref, qn_ref, kn_ref, *rest):
    o_ref, ns_ref, nk_ref, nv_ref = rest[-4:]
    for r in range(p_ref.shape[0]):
        _mix_one_request(sink_ref, p_ref.at[r], s_ref.at[r], ck_ref.at[r], cv_ref.at[r], lb_ref,
                         gn_ref, qn_ref, kn_ref, o_ref.at[r], ns_ref.at[r], nk_ref.at[r],
                         nv_ref.at[r])


def _mix_one_request(sink_ref, p_ref, s_ref, ck_ref, cv_ref, lb_ref,
                     gn_ref, qn_ref, kn_ref, o_ref, ns_ref, nk_ref, nv_ref):
    W = WINDOW
    er = lax.broadcasted_iota(jnp.int32, (HEAD_DIM, HEAD_DIM), 0)
    ec = lax.broadcasted_iota(jnp.int32, (HEAD_DIM, HEAD_DIM), 1)
    eye = er == ec
    g_norm = gn_ref[...]
    q_norm = qn_ref[...]
    k_norm = kn_ref[...]

    def cols(off, width=HEAD_DIM):
        return p_ref[:, off:off + width]

    for h in range(N_LIN_HEADS):
        c0 = h * HEAD_DIM
        q = _silu(cols(c0))
        lb = lb_ref[:, c0:c0 + HEAD_DIM]
        _, fp = _forget_gate(cols(LIN_WIDTH + c0), lb)
        f_col = _row_to_col(lb + fp, eye)
        k_col = 1.0 - f_col
        v = cols(2 * LIN_WIDTH + c0)
        s_new = s_ref[h] * f_col + k_col * v
        ns_ref[h] = s_new
        q8 = jnp.broadcast_to(q, (8, HEAD_DIM)).astype(BF16)
        o = jnp.dot(q8, s_new.astype(BF16), preferred_element_type=F32)[0:1, :]
        o = _rms_rows(o, g_norm) * _silu(cols(3 * LIN_WIDTH + c0))
        o_ref[:, c0:c0 + HEAD_DIM] = o

    row8 = lax.broadcasted_iota(jnp.int32, (8, 1), 0)
    row8_k = lax.broadcasted_iota(jnp.int32, (8, HEAD_DIM), 0)
    lane = lax.broadcasted_iota(jnp.int32, (8, W), 1)
    dist_c = (W - lane).astype(F32)
    rows_w = lax.broadcasted_iota(jnp.int32, (W, HEAD_DIM), 0)
    for h in range(N_KV_HEADS):
        hs = slice(h * HEAD_DIM, (h + 1) * HEAD_DIM)
        k_new = _rms_rows(cols(OFF_K_A + h * HEAD_DIM), k_norm)
        v_new = cols(OFF_V_A + h * HEAD_DIM)
        kc = ck_ref[:, h, :]
        vc = cv_ref[:, h, :]
        nk_ref[:, h, :] = jnp.where(rows_w == W - 1, k_new, pltpu.roll(kc, W - 1, 0))
        nv_ref[:, h, :] = jnp.where(rows_w == W - 1, v_new, pltpu.roll(vc, W - 1, 0))

        q4 = jnp.zeros((8, HEAD_DIM), F32)
        slope = jnp.zeros((8, 1), F32)
        sink = jnp.zeros((8, 1), F32)
        for g in range(GROUP):
            a = h * GROUP + g
            qg = _rms_rows(cols(OFF_Q_A + a * HEAD_DIM), q_norm)
            q4 = jnp.where(row8_k == g, qg, q4)
            slope = jnp.where(row8 == g, ALIBI_SLOPES[a], slope)
            sink = jnp.where(row8 == g, sink_ref[a], sink)
        q4b = q4.astype(BF16)
        s_c = lax.dot_general(q4b, kc.astype(BF16), (((1,), (1,)), ((), ())),
                              preferred_element_type=F32)
        s_c = s_c * ATT_SCALE - slope * dist_c
        s_n = jnp.sum(q4 * k_new, axis=-1, keepdims=True) * ATT_SCALE
        m = jnp.maximum(jnp.maximum(jnp.max(s_c, axis=-1, keepdims=True), s_n), sink)
        p_c = jnp.exp(s_c - m)
        p_n = jnp.exp(s_n - m)
        den = jnp.sum(p_c, axis=-1, keepdims=True) + p_n + jnp.exp(sink - m)
        o = jnp.dot(p_c.astype(BF16), vc.astype(BF16), preferred_element_type=F32)
        o = (o + p_n * v_new) / den
        for g in range(GROUP):
            a = h * GROUP + g
            o_ref[:, LIN_WIDTH + a * HEAD_DIM:LIN_WIDTH + (a + 1) * HEAD_DIM] = o[g:g + 1, :]


def _mix_sample(proj, state, cache_k, cache_v, layer, sinks, lb, g_norm, q_norm, k_norm, stacked):
    B = proj.shape[0]
    rb = MIX_RB
    vec = pl.BlockSpec((1, HEAD_DIM), lambda b: (0, 0))
    lbs = pl.BlockSpec((1, LIN_WIDTH), lambda b: (0, 0))
    cache_in = pl.BlockSpec((None, rb, WINDOW, N_KV_HEADS, HEAD_DIM),
                            lambda b: (layer, b, 0, 0, 0))
    st_in = pl.BlockSpec((None, rb, N_LIN_HEADS, HEAD_DIM, HEAD_DIM),
                         lambda b: (layer, b, 0, 0, 0))
    n_in = 9
    aliases = {n_in + k: 1 + k for k in range(len(stacked))}
    return pl.pallas_call(
        _mix_sample_kernel,
        grid=(B // rb,),
        in_specs=[
            pl.BlockSpec(memory_space=pltpu.SMEM),
            pl.BlockSpec((rb, 1, IN_WIDTH), lambda b: (b, 0, 0)),
            st_in, cache_in, cache_in, lbs, vec, vec, vec,
        ] + [pl.BlockSpec(memory_space=pl.ANY)] * len(stacked),
        out_specs=(
            pl.BlockSpec((rb, 1, MIX_WIDTH), lambda b: (b, 0, 0)),
            st_in, cache_in, cache_in,
        ),
        out_shape=(
            jax.ShapeDtypeStruct((B, 1, MIX_WIDTH), F32),
            jax.ShapeDtypeStruct(state.shape, F32),
            jax.ShapeDtypeStruct(cache_k.shape, F32),
            jax.ShapeDtypeStruct(cache_v.shape, F32),
        ),
        input_output_aliases=aliases,
        compiler_params=_cparams(("parallel",)),
        name="mix_sample",
    )(sinks, proj, state, cache_k, cache_v, lb, g_norm, q_norm, k_norm, *stacked)


TILE_N = 512
P_TM = 1024
P_TM_OUT = 2048
F32_SLABS = (LIN_WIDTH // TILE_N, LIN_WIDTH // TILE_N + 1, OFF_K_A // TILE_N)


def kernel(x_prompt, x_sample, cache_k, cache_v, state_hgrn, c_prompt, c_sample, lower_bounds,
           w_ada, b_ada, norm_mix, w_in, q_norm, k_norm, attn_sinks, g_norm, w_out, norm_ffn,
           w_gate, w_up, w_down):
    B, T, _ = x_prompt.shape
    BS = x_sample.shape[0]

    lb_all = _lower_bounds(lower_bounds)

    c_all = jnp.concatenate(
        [c_prompt, c_sample, jnp.zeros((MOD_ROWS - B - BS, D_MODEL), F32)], axis=0)
    mod = _modulation(c_all, w_ada, b_ada)

    yp = x_prompt
    ys = x_sample.reshape(1, BS, D_MODEL)
    pk, pv, ps = [], [], []
    sample_out = tuple(jnp.zeros(a.shape, F32) for a in (state_hgrn, cache_k, cache_v))
    for l in range(DEPTH):
        mod_p = mod[l, :B].reshape(B, 1, 6 * D_MODEL)
        mod_s = mod[l, B:B + BS].reshape(1, BS, 6 * D_MODEL)
        gain_mix = norm_mix[l].reshape(1, D_MODEL)
        gain_ffn = norm_ffn[l].reshape(1, D_MODEL)
        lb = lb_all[l].reshape(1, LIN_WIDTH)
        gn = g_norm[l].reshape(1, HEAD_DIM)
        qn = q_norm[l].reshape(1, HEAD_DIM)
        kn = k_norm[l].reshape(1, HEAD_DIM)
        sinks = attn_sinks[l].astype(F32)

        proj_s, w_in_b = _norm_matmul(ys, mod_s, 0, gain_mix, w_in, l, BS, TILE_N)
        mixed_s, *sample_out = _mix_sample(
            proj_s.reshape(BS, 1, IN_WIDTH), state_hgrn, cache_k, cache_v, l,
            sinks, lb, gn, qn, kn, sample_out)
        ys, w_out_b = _matmul_residual(mixed_s.reshape(1, BS, MIX_WIDTH), w_out, l, ys, mod_s,
                                       2 * D_MODEL, BS, TILE_N)
        ff_s, w_gate_b, w_up_b = _norm_gate_up(ys, mod_s, 3, gain_ffn, w_gate, w_up, l, BS, TILE_N)
        ys, w_down_b = _matmul_residual(ff_s, w_down, l, ys, mod_s, 5 * D_MODEL, BS, TILE_N)

        proj32, proj16 = _in_proj_prompt(yp, mod_p, gain_mix, w_in_b, F32_SLABS, P_TM)
        o_l, s_p = _hgrn_prompt(proj32, proj16, lb, gn)
        o_a, k_p, v_p = _attn_prompt(proj32, proj16, sinks, qn, kn)
        yp = _res_matmul_prompt([o_l, o_a], [(w_out_b, 0), (w_out_b, 1)], yp, mod_p, 2,
                                P_TM_OUT, "out_proj")
        ff = _gate_up_prompt(yp, mod_p, 3, gain_ffn, w_gate_b, w_up_b, P_TM)
        yp = _res_matmul_prompt([ff], [w_down_b], yp, mod_p, 5, P_TM, "down_proj")
        pk.append(k_p.reshape(B, WINDOW, N_KV_HEADS, HEAD_DIM))
        pv.append(v_p.reshape(B, WINDOW, N_KV_HEADS, HEAD_DIM))
        ps.append(s_p)

    s_s, k_s, v_s = sample_out
    return (yp, ys.reshape(BS, 1, D_MODEL), jnp.stack(pk), jnp.stack(pv), jnp.stack(ps),
            k_s, v_s, s_s)
```
